```python
import jax, jax.numpy as jnp
from jax import lax
import numpy as np


D_MODEL = 1024
BATCH = 4
SEQ = 8192
DEPTH = 2
DEC_BATCH = 8
DEC_SEQ = 8192
PAST_LEN = 128

N_MIXERS = 2
N_LAYERS_A = (DEPTH + 1) // 2
N_LAYERS_B = DEPTH // 2
HEAD_DIM = 64
MIX_WIDTH = D_MODEL
MEM_HEADS = 4
MEM_DIM = MEM_HEADS * HEAD_DIM
N_MEM = 256
NA_HEADS = (MIX_WIDTH - MEM_DIM) // HEAD_DIM
NA_DIM = NA_HEADS * HEAD_DIM
FOURIER_DIM = MIX_WIDTH - MEM_DIM
GRID_W = 64
KERNEL_ROWS = 8
KERNEL_COLS = 16
D_FF = ((8 * D_MODEL // 3 + 127) // 128) * 128
EPS = 1e-6
NEG_INF = -1e30

kernel_name = 'hybrid_natten_fnet_memory_encoder'


def _rmsnorm(x, g):
    xf = x.astype(jnp.float32)
    y = xf * lax.rsqrt(jnp.mean(xf * xf, axis=-1, keepdims=True) + EPS)
    return (y * g.astype(jnp.float32)).astype(x.dtype)


def _swiglu(h, w_in, w_out):
    g, u = jnp.split(h @ w_in, 2, axis=-1)
    return (jax.nn.silu(g) * u) @ w_out


def _memory_attention(qm, mem_h, w_kv, gq, gk):
    B, S, _ = qm.shape
    M = mem_h.shape[1]
    k, v = jnp.split(mem_h @ w_kv, 2, axis=-1)
    q = _rmsnorm(qm.reshape(B, S, MEM_HEADS, HEAD_DIM), gq)
    k = _rmsnorm(k.reshape(B, M, MEM_HEADS, HEAD_DIM), gk)
    v = v.reshape(B, M, MEM_HEADS, HEAD_DIM)
    s = jnp.einsum('bshd,bmhd->bhsm', q, k, preferred_element_type=jnp.float32) * (HEAD_DIM ** -0.5)
    p = jax.nn.softmax(s, axis=-1).astype(v.dtype)
    o = jnp.einsum('bhsm,bmhd->bshd', p, v)
    return o.reshape(B, S, MEM_DIM)


def _neighbourhood_attention_seq(q, k, v, rpb):
    S = q.shape[0]
    rows = S // GRID_W
    kr = min(KERNEL_ROWS, rows)
    q = q.reshape(rows, GRID_W, NA_HEADS, HEAD_DIM)
    k = k.reshape(rows, GRID_W, NA_HEADS, HEAD_DIM)
    v = v.reshape(rows, GRID_W, NA_HEADS, HEAD_DIM)
    r = jnp.arange(rows)
    rs = jnp.clip(r - kr // 2, 0, rows - kr)
    row_idx = rs[:, None] + jnp.arange(kr)[None, :]
    k_blk = k[row_idx]
    v_blk = v[row_idx]
    s = jnp.einsum('rqhd,rowhd->rhqow', q, k_blk, preferred_element_type=jnp.float32) * (HEAD_DIM ** -0.5)
    c = jnp.arange(GRID_W)
    cs = jnp.clip(c - KERNEL_COLS // 2, 0, GRID_W - KERNEL_COLS)
    inside = (c[None, :] >= cs[:, None]) & (c[None, :] < cs[:, None] + KERNEL_COLS)
    dr = row_idx - r[:, None] + (KERNEL_ROWS - 1)
    dc = jnp.clip(c[None, :] - c[:, None], -(KERNEL_COLS - 1), KERNEL_COLS - 1) + (KERNEL_COLS - 1)
    bias = rpb.astype(jnp.float32)[:, dr]
    bias = jnp.take(bias, dc, axis=3)
    bias = bias.transpose(1, 0, 3, 2, 4)
    s = jnp.where(inside[None, None, :, None, :], s + bias, NEG_INF)
    p = jax.nn.softmax(s.reshape(rows, NA_HEADS, GRID_W, kr * GRID_W), axis=-1)
    p = p.reshape(rows, NA_HEADS, GRID_W, kr, GRID_W).astype(v.dtype)
    o = jnp.einsum('rhqow,rowhd->rqhd', p, v_blk)
    return o.reshape(S, NA_DIM)


def _mixer_neighbourhood(h, mem_h, w_in, gq, gk, rpb, w_out, w_kv, mgq, mgk):
    B, S, _ = h.shape
    proj = h @ w_in
    q, k, v, qm = jnp.split(proj, [NA_DIM, 2 * NA_DIM, 3 * NA_DIM], axis=-1)
    q = _rmsnorm(q.reshape(B, S, NA_HEADS, HEAD_DIM), gq)
    k = _rmsnorm(k.reshape(B, S, NA_HEADS, HEAD_DIM), gk)
    v = v.reshape(B, S, NA_HEADS, HEAD_DIM)
    na = lax.map(lambda t: _neighbourhood_attention_seq(t[0], t[1], t[2], rpb), (q, k, v))
    mo = _memory_attention(qm, mem_h, w_kv, mgq, mgk)
    return jnp.concatenate([na, mo], axis=-1) @ w_out


def _mixer_fourier(h, mem_h, w_in, w_out, w_kv, mgq, mgk):
    proj = h @ w_in
    z, qm = jnp.split(proj, [FOURIER_DIM], axis=-1)
    f = jnp.fft.fft2(z.astype(jnp.float32), axes=(1, 2), norm='ortho').real.astype(h.dtype)
    mo = _memory_attention(qm, mem_h, w_kv, mgq, mgk)
    return jnp.concatenate([f, mo], axis=-1) @ w_out


def _trunk(x, mem, norm_ffn1, w_ffn1_in, w_ffn1_out, norm_mix, norm_mem, w_mem_kv, mem_q_norm, mem_k_norm,
           w_in_a, na_q_norm, na_k_norm, na_rpb, w_out_a, w_in_b, w_out_b,
           norm_ffn2, w_ffn2_in, w_ffn2_out, norm_out):
    for i in range(DEPTH):
        x = x + 0.5 * _swiglu(_rmsnorm(x, norm_ffn1[i]), w_ffn1_in[i], w_ffn1_out[i])
        h = _rmsnorm(x, norm_mix[i])
        mem_h = _rmsnorm(mem, norm_mem[i])
        j = i // N_MIXERS
        if i % N_MIXERS == 0:
            x = x + _mixer_neighbourhood(h, mem_h, w_in_a[j], na_q_norm[j], na_k_norm[j], na_rpb[j], w_out_a[j],
                                         w_mem_kv[i], mem_q_norm[i], mem_k_norm[i])
        else:
            x = x + _mixer_fourier(h, mem_h, w_in_b[j], w_out_b[j], w_mem_kv[i], mem_q_norm[i], mem_k_norm[i])
        x = x + 0.5 * _swiglu(_rmsnorm(x, norm_ffn2[i]), w_ffn2_in[i], w_ffn2_out[i])
        x = _rmsnorm(x, norm_out[i])
    return x


def setup_inputs(seed: int = 0) -> dict:
    key = jax.random.key(seed)
    ks = jax.random.split(key, 24)
    f32 = jnp.float32

    def nrm(k, shape, scale):
        return jax.random.normal(k, shape, f32) * scale

    def gain(k, shape):
        return 1.0 + 0.01 * jax.random.normal(k, shape, f32)

    D = D_MODEL
    return {
        'x_prompt': nrm(ks[0], (BATCH, SEQ, D), 1.0),
        'x_sample': nrm(ks[1], (DEC_BATCH, DEC_SEQ, D), 1.0),
        'mem_prompt': nrm(ks[2], (BATCH, N_MEM, D), 1.0),
        'mem_sample': nrm(ks[3], (DEC_BATCH, N_MEM, D), 1.0),
        'norm_ffn1': gain(ks[4], (DEPTH, D)),
        'w_ffn1_in': nrm(ks[5], (DEPTH, D, 2 * D_FF), D ** -0.5),
        'w_ffn1_out': nrm(ks[6], (DEPTH, D_FF, D), D_FF ** -0.5),
        'norm_mix': gain(ks[7], (DEPTH, D)),
        'norm_mem': gain(ks[8], (DEPTH, D)),
        'w_mem_kv': nrm(ks[9], (DEPTH, D, 2 * MEM_DIM), D ** -0.5),
        'mem_q_norm': gain(ks[10], (DEPTH, HEAD_DIM)),
        'mem_k_norm': gain(ks[11], (DEPTH, HEAD_DIM)),
        'w_in_a': nrm(ks[12], (N_LAYERS_A, D, 3 * NA_DIM + MEM_DIM), D ** -0.5),
        'na_q_norm': gain(ks[13], (N_LAYERS_A, HEAD_DIM)),
        'na_k_norm': gain(ks[14], (N_LAYERS_A, HEAD_DIM)),
        'na_rpb': nrm(ks[15], (N_LAYERS_A, NA_HEADS, 2 * KERNEL_ROWS - 1, 2 * KERNEL_COLS - 1), 0.1),
        'w_out_a': nrm(ks[16], (N_LAYERS_A, MIX_WIDTH, D), MIX_WIDTH ** -0.5),
        'w_in_b': nrm(ks[17], (N_LAYERS_B, D, FOURIER_DIM + MEM_DIM), D ** -0.5),
        'w_out_b': nrm(ks[18], (N_LAYERS_B, MIX_WIDTH, D), MIX_WIDTH ** -0.5),
        'norm_ffn2': gain(ks[19], (DEPTH, D)),
        'w_ffn2_in': nrm(ks[20], (DEPTH, D, 2 * D_FF), D ** -0.5),
        'w_ffn2_out': nrm(ks[21], (DEPTH, D_FF, D), D_FF ** -0.5),
        'norm_out': gain(ks[22], (DEPTH, D)),
    }


def reference(x_prompt, x_sample, mem_prompt, mem_sample, norm_ffn1, w_ffn1_in, w_ffn1_out, norm_mix, norm_mem,
              w_mem_kv, mem_q_norm, mem_k_norm, w_in_a, na_q_norm, na_k_norm, na_rpb, w_out_a, w_in_b, w_out_b,
              norm_ffn2, w_ffn2_in, w_ffn2_out, norm_out):
    y_prompt = _trunk(x_prompt, mem_prompt, norm_ffn1, w_ffn1_in, w_ffn1_out, norm_mix, norm_mem, w_mem_kv,
                      mem_q_norm, mem_k_norm, w_in_a, na_q_norm, na_k_norm, na_rpb, w_out_a, w_in_b, w_out_b,
                      norm_ffn2, w_ffn2_in, w_ffn2_out, norm_out)
    y_sample = _trunk(x_sample, mem_sample, norm_ffn1, w_ffn1_in, w_ffn1_out, norm_mix, norm_mem, w_mem_kv,
                      mem_q_norm, mem_k_norm, w_in_a, na_q_norm, na_k_norm, na_rpb, w_out_a, w_in_b, w_out_b,
                      norm_ffn2, w_ffn2_in, w_ffn2_out, norm_out)
    return (y_prompt, y_sample)
```

```python
import functools

import numpy as np
import jax
import jax.numpy as jnp
from jax import lax
from jax.experimental import pallas as pl
from jax.experimental.pallas import tpu as pltpu

D_MODEL = 1024
DEPTH = 2
HEAD_DIM = 64
MEM_HEADS = 4
MEM_DIM = MEM_HEADS * HEAD_DIM
N_MEM = 256
NA_HEADS = 12
NA_DIM = NA_HEADS * HEAD_DIM
FOURIER_DIM = 768
GRID_W = 64
KERNEL_ROWS = 8
KERNEL_COLS = 16
D_FF = 2816
EPS = 1e-6
NEG_INF = -1e30
SEQ = 8192
ROWS = SEQ // GRID_W
FFT_N1 = 64
FFT_N2 = 128

F32 = jnp.float32
BF16 = jnp.bfloat16

LANES = 128
FF_CHUNK = 256
N_FF_CHUNKS = D_FF // FF_CHUNK
TOKEN_TILE = 512
VMEM_LIMIT = 56 * 1024 * 1024


def _params(n_axes):
    return pltpu.CompilerParams(dimension_semantics=("arbitrary",) * n_axes,
                                vmem_limit_bytes=VMEM_LIMIT)


def _const_spec(shape):
    n = len(shape)
    return pl.BlockSpec(shape, lambda *_: (0,) * n, pipeline_mode=pl.Buffered(1))


def _rms(x, g):
    ms = jnp.mean(x * x, axis=-1, keepdims=True)
    return x * lax.rsqrt(ms + EPS) * g


def _split_dot(a, e):
    hi = a.astype(BF16)
    lo = (a - hi.astype(F32)).astype(BF16)
    return (jnp.dot(hi, e, preferred_element_type=F32)
            + jnp.dot(lo, e, preferred_element_type=F32))


def _head_rmsnorm(t, gain, e_sum, e_bcast):
    ms = _split_dot(t * t, e_sum)
    inv = lax.rsqrt(ms + EPS)
    return t * _split_dot(inv, e_bcast) * gain


def _head_lane_mask(shape, head, lane_axis):
    lane = lax.broadcasted_iota(jnp.int32, shape, lane_axis)
    return (lane >= head * HEAD_DIM) & (lane < (head + 1) * HEAD_DIM)


def _ffn_kernel(x_ref, g_ref, win_ref, wout_ref, gout_ref, o_ref, xn_ref, acc_ref, *, final_norm):
    x = x_ref[...]
    xn_ref[...] = _rms(x, g_ref[...]).astype(BF16)
    acc_ref[...] = jnp.zeros_like(acc_ref)

    def body(c, carry):
        h = jnp.dot(xn_ref[...], win_ref[c], preferred_element_type=F32)
        gate = h[:, :FF_CHUNK]
        up = h[:, FF_CHUNK:]
        act = (gate * jax.nn.sigmoid(gate) * up).astype(BF16)
        acc_ref[...] += jnp.dot(act, wout_ref[c], preferred_element_type=F32)
        return carry

    lax.fori_loop(0, N_FF_CHUNKS, body, 0)
    y = x + 0.5 * acc_ref[...]
    if final_norm:
        y = _rms(y, gout_ref[...])
    o_ref[...] = y


def _ffn(x2d, g, win_c, wout_c, gout, final_norm):
    t = x2d.shape[0]
    tm = TOKEN_TILE
    return pl.pallas_call(
        functools.partial(_ffn_kernel, final_norm=final_norm),
        out_shape=jax.ShapeDtypeStruct((t, D_MODEL), F32),
        grid=(t // tm,),
        in_specs=[
            pl.BlockSpec((tm, D_MODEL), lambda i: (i, 0)),
            _const_spec((1, D_MODEL)),
            _const_spec((N_FF_CHUNKS, D_MODEL, 2 * FF_CHUNK)),
            _const_spec((N_FF_CHUNKS, FF_CHUNK, D_MODEL)),
            _const_spec((1, D_MODEL)),
        ],
        out_specs=pl.BlockSpec((tm, D_MODEL), lambda i: (i, 0)),
        scratch_shapes=[pltpu.VMEM((tm, D_MODEL), BF16), pltpu.VMEM((tm, D_MODEL), F32)],
        compiler_params=_params(1),
        name="ffn_final" if final_norm else "ffn",
    )(x2d, g, win_c, wout_c, gout)


def _mixin_a_kernel(x_ref, g_ref, w_ref, gq_ref, gk_ref, gqm_ref, es_ref, eb_ref, esm_ref, ebm_ref,
                    q_ref, k_ref, v_ref, qm_ref):
    h = _rms(x_ref[...], g_ref[...]).astype(BF16)
    proj = jnp.dot(h, w_ref[...], preferred_element_type=F32)
    q = proj[:, :NA_DIM]
    k = proj[:, NA_DIM:2 * NA_DIM]
    v = proj[:, 2 * NA_DIM:3 * NA_DIM]
    qm = proj[:, 3 * NA_DIM:]
    scale = HEAD_DIM ** -0.5
    q_ref[...] = (_head_rmsnorm(q, gq_ref[...], es_ref[...], eb_ref[...]) * scale).astype(BF16)
    k_ref[...] = _head_rmsnorm(k, gk_ref[...], es_ref[...], eb_ref[...]).astype(BF16)
    v_ref[...] = v.astype(BF16)
    qm_ref[...] = (_head_rmsnorm(qm, gqm_ref[...], esm_ref[...], ebm_ref[...]) * scale).astype(BF16)


def _mixin_a(x2d, g, w, gq, gk, gqm, consts):
    t = x2d.shape[0]
    tm = TOKEN_TILE
    width = 3 * NA_DIM + MEM_DIM
    tok = lambda n: pl.BlockSpec((tm, n), lambda i: (i, 0))
    return pl.pallas_call(
        _mixin_a_kernel,
        out_shape=(jax.ShapeDtypeStruct((t, NA_DIM), BF16),) * 3 + (jax.ShapeDtypeStruct((t, MEM_DIM), BF16),),
        grid=(t // tm,),
        in_specs=[
            tok(D_MODEL),
            _const_spec((1, D_MODEL)),
            _const_spec((D_MODEL, width)),
            _const_spec((1, NA_DIM)), _const_spec((1, NA_DIM)), _const_spec((1, MEM_DIM)),
            _const_spec((NA_DIM, LANES)), _const_spec((LANES, NA_DIM)),
            _const_spec((MEM_DIM, LANES)), _const_spec((LANES, MEM_DIM)),
        ],
        out_specs=(tok(NA_DIM), tok(NA_DIM), tok(NA_DIM), tok(MEM_DIM)),
        compiler_params=_params(1),
        name="mixin_na",
    )(x2d, g, w, gq, gk, gqm, consts["es"], consts["eb"], consts["esm"], consts["ebm"])


def _mixin_b_kernel(x_ref, g_ref, w_ref, gqm_ref, esm_ref, ebm_ref, dft_ref, a_ref, b_ref, qm_ref):
    h = _rms(x_ref[...], g_ref[...]).astype(BF16)
    proj = jnp.dot(h, w_ref[...], preferred_element_type=F32)
    z = proj[:, :FOURIER_DIM].astype(BF16)
    qm = proj[:, FOURIER_DIM:]
    ab = jnp.dot(z, dft_ref[...], preferred_element_type=F32)
    a_ref[...] = ab[:, :FOURIER_DIM].astype(BF16)
    b_ref[...] = ab[:, FOURIER_DIM:].astype(BF16)
    scale = HEAD_DIM ** -0.5
    qm_ref[...] = (_head_rmsnorm(qm, gqm_ref[...], esm_ref[...], ebm_ref[...]) * scale).astype(BF16)


def _mixin_b(x2d, g, w, gqm, consts):
    t = x2d.shape[0]
    tm = TOKEN_TILE
    tok = lambda n: pl.BlockSpec((tm, n), lambda i: (i, 0))
    return pl.pallas_call(
        _mixin_b_kernel,
        out_shape=(jax.ShapeDtypeStruct((t, FOURIER_DIM), BF16),) * 2 + (jax.ShapeDtypeStruct((t, MEM_DIM), BF16),),
        grid=(t // tm,),
        in_specs=[
            tok(D_MODEL),
            _const_spec((1, D_MODEL)),
            _const_spec((D_MODEL, FOURIER_DIM + MEM_DIM)),
            _const_spec((1, MEM_DIM)),
            _const_spec((MEM_DIM, LANES)), _const_spec((LANES, MEM_DIM)),
            _const_spec((FOURIER_DIM, 2 * FOURIER_DIM)),
        ],
        out_specs=(tok(FOURIER_DIM), tok(FOURIER_DIM), tok(MEM_DIM)),
        compiler_params=_params(1),
        name="mixin_fnet",
    )(x2d, g, w, gqm, consts["esm"], consts["ebm"], consts["dft_feat"])


NA_HEADS_PER_STEP = LANES // HEAD_DIM
NA_KEYS = KERNEL_ROWS * GRID_W


def _na_kernel(q_ref, k_ref, v_ref, bias_ref, o_ref):
    first_head = _head_lane_mask((GRID_W, LANES), 0, 1)

    def body(r, carry):
        rs = jnp.clip(r - KERNEL_ROWS // 2, 0, ROWS - KERNEL_ROWS)
        shift = rs - r + (KERNEL_ROWS - 1)
        q = q_ref[pl.ds(pl.multiple_of(r * GRID_W, GRID_W), GRID_W), :]
        zero = jnp.zeros_like(q)
        q2 = jnp.concatenate([jnp.where(first_head, q, zero), jnp.where(first_head, zero, q)], axis=0)
        key_start = pl.multiple_of(rs * GRID_W, GRID_W)
        kb = k_ref[pl.ds(key_start, NA_KEYS), :]
        vb = v_ref[pl.ds(key_start, NA_KEYS), :]
        s = lax.dot_general(q2, kb, (((1,), (1,)), ((), ())), preferred_element_type=F32)
        s = s + bias_ref[shift]
        m = jnp.max(s, axis=-1, keepdims=True)
        e = jnp.exp(s - m)
        l = jnp.sum(e, axis=-1, keepdims=True)
        o2 = jnp.dot(e.astype(BF16), vb, preferred_element_type=F32) * (1.0 / l)
        o = jnp.where(first_head, o2[:GRID_W], o2[GRID_W:])
        o_ref[pl.ds(pl.multiple_of(r * GRID_W, GRID_W), GRID_W), :] = o.astype(BF16)
        return carry

    lax.fori_loop(0, ROWS, body, 0)


def _na(q3, k3, v3, bias):
    b = q3.shape[0]
    n_pairs = NA_HEADS // NA_HEADS_PER_STEP
    seq_spec = pl.BlockSpec((None, SEQ, LANES), lambda i, j: (i, 0, j))
    return pl.pallas_call(
        _na_kernel,
        out_shape=jax.ShapeDtypeStruct((b, SEQ, NA_DIM), BF16),
        grid=(b, n_pairs),
        in_specs=[seq_spec, seq_spec, seq_spec,
                  pl.BlockSpec((None, KERNEL_ROWS, LANES, NA_KEYS), lambda i, j: (j, 0, 0, 0))],
        out_specs=seq_spec,
        compiler_params=_params(2),
        name="na_attention",
    )(q3, k3, v3, bias)


FFT_G1 = 8
FFT_G2 = 8


def _fft_stage1_kernel(a_ref, b_ref, g_ref, yr_ref, yi_ref):
    for j in range(FFT_G1):
        cols = slice(j * FOURIER_DIM, (j + 1) * FOURIER_DIM)
        x = jnp.concatenate([a_ref[:, cols], b_ref[:, cols]], axis=0)
        y = jnp.dot(g_ref[j], x, preferred_element_type=F32)
        yr_ref[:, cols] = y[:FFT_N1].astype(BF16)
        yi_ref[:, cols] = y[FFT_N1:].astype(BF16)


def _fft_stage2_kernel(yr_ref, yi_ref, h_ref, f_ref):
    for j in range(FFT_G2):
        x = jnp.concatenate([yr_ref[j], yi_ref[j]], axis=0)
        f_ref[:, j * FOURIER_DIM:(j + 1) * FOURIER_DIM] = jnp.dot(
            h_ref[...], x, preferred_element_type=F32).astype(BF16)


def _seq_dft_real(a3, b3, consts):
    b = a3.shape[0]
    wide = FFT_N2 * FOURIER_DIM
    a_v = a3.reshape(b, FFT_N1, wide)
    b_v = b3.reshape(b, FFT_N1, wide)
    blk1 = pl.BlockSpec((None, FFT_N1, FFT_G1 * FOURIER_DIM), lambda i, j: (i, 0, j))
    yr, yi = pl.pallas_call(
        _fft_stage1_kernel,
        out_shape=(jax.ShapeDtypeStruct((b, FFT_N1, wide), BF16),) * 2,
        grid=(b, FFT_N2 // FFT_G1),
        in_specs=[blk1, blk1,
                  pl.BlockSpec((FFT_G1, 2 * FFT_N1, 2 * FFT_N1), lambda i, j: (j, 0, 0))],
        out_specs=(blk1, blk1),
        compiler_params=_params(2),
        name="fft_stage1",
    )(a_v, b_v, consts["dft_seq1"])
    yr = yr.reshape(b, FFT_N1, FFT_N2, FOURIER_DIM)
    yi = yi.reshape(b, FFT_N1, FFT_N2, FOURIER_DIM)
    blk2 = pl.BlockSpec((None, FFT_G2, FFT_N2, FOURIER_DIM), lambda i, j: (i, j, 0, 0))
    f = pl.pallas_call(
        _fft_stage2_kernel,
        out_shape=jax.ShapeDtypeStruct((b, FFT_N2, FFT_N1 * FOURIER_DIM), BF16),
        grid=(b, FFT_N1 // FFT_G2),
        in_specs=[blk2, blk2, _const_spec((FFT_N2, 2 * FFT_N2))],
        out_specs=pl.BlockSpec((None, FFT_N2, FFT_G2 * FOURIER_DIM), lambda i, j: (i, 0, j)),
        compiler_params=_params(2),
        name="fft_stage2",
    )(yr, yi, consts["dft_seq2"])
    return f.reshape(b, SEQ, FOURIER_DIM)


def _memkv_kernel(mem_ref, g_ref, w_ref, gk_ref, esm_ref, ebm_ref, k_ref, v_ref):
    h = _rms(mem_ref[...], g_ref[...]).astype(BF16)
    kv = jnp.dot(h, w_ref[...], preferred_element_type=F32)
    k_ref[...] = _head_rmsnorm(kv[:, :MEM_DIM], gk_ref[...], esm_ref[...], ebm_ref[...]).astype(BF16)
    v_ref[...] = kv[:, MEM_DIM:].astype(BF16)


def _memkv(mem, g, w, gk, consts):
    b = mem.shape[0]
    out_blk = pl.BlockSpec((None, N_MEM, MEM_DIM), lambda i: (i, 0, 0))
    return pl.pallas_call(
        _memkv_kernel,
        out_shape=(jax.ShapeDtypeStruct((b, N_MEM, MEM_DIM), BF16),) * 2,
        grid=(b,),
        in_specs=[
            pl.BlockSpec((None, N_MEM, D_MODEL), lambda i: (i, 0, 0)),
            _const_spec((1, D_MODEL)),
            _const_spec((D_MODEL, 2 * MEM_DIM)),
            _const_spec((1, MEM_DIM)),
            _const_spec((MEM_DIM, LANES)), _const_spec((LANES, MEM_DIM)),
        ],
        out_specs=(out_blk, out_blk),
        compiler_params=_params(1),
        name="memory_kv",
    )(mem, g, w, gk, consts["esm"], consts["ebm"])


def _mixout_kernel(x_ref, a_ref, qm_ref, km_ref, vm_ref, wa_ref, wm_ref, o_ref):
    qm = qm_ref[...]
    km = km_ref[...]
    vm = vm_ref[...]
    zero = jnp.zeros_like(qm)
    mo = jnp.zeros(qm.shape, F32)
    for head in range(MEM_HEADS):
        in_head = _head_lane_mask(qm.shape, head, 1)
        s = lax.dot_general(jnp.where(in_head, qm, zero), km, (((1,), (1,)), ((), ())),
                            preferred_element_type=F32)
        m = jnp.max(s, axis=-1, keepdims=True)
        e = jnp.exp(s - m)
        l = jnp.sum(e, axis=-1, keepdims=True)
        oh = jnp.dot(e.astype(BF16), vm, preferred_element_type=F32) * (1.0 / l)
        mo = jnp.where(in_head, oh, mo)
    y = jnp.dot(a_ref[...], wa_ref[...], preferred_element_type=F32)
    y = y + jnp.dot(mo.astype(BF16), wm_ref[...], preferred_element_type=F32)
    o_ref[...] = x_ref[...] + y


def _mixout(x3, a3, qm3, km, vm, w_a, w_m):
    b = x3.shape[0]
    tm = TOKEN_TILE
    tok = lambda n: pl.BlockSpec((None, tm, n), lambda i, j: (i, j, 0))
    mem_blk = pl.BlockSpec((None, N_MEM, MEM_DIM), lambda i, j: (i, 0, 0))
    mix = a3.shape[-1]
    return pl.pallas_call(
        _mixout_kernel,
        out_shape=jax.ShapeDtypeStruct(x3.shape, F32),
        grid=(b, SEQ // tm),
        in_specs=[tok(D_MODEL), tok(mix), tok(MEM_DIM), mem_blk, mem_blk,
                  _const_spec((mix, D_MODEL)), _const_spec((MEM_DIM, D_MODEL))],
        out_specs=tok(D_MODEL),
        compiler_params=_params(2),
        name="mixer_out",
    )(x3, a3, qm3, km, vm, w_a, w_m)


def _head_sum_matrices(width):
    head = np.arange(width) // HEAD_DIM
    e_sum = (head[:, None] == np.arange(LANES)[None, :]).astype(np.float32) / HEAD_DIM
    e_bcast = (np.arange(LANES)[:, None] == head[None, :]).astype(np.float32)
    return jnp.asarray(e_sum, BF16), jnp.asarray(e_bcast, BF16)


def _dft_constants():
    d = np.arange(FOURIER_DIM)
    ang = 2.0 * np.pi * ((d[:, None] * d[None, :]) % FOURIER_DIM) / FOURIER_DIM
    feat = np.concatenate([np.cos(ang), np.sin(ang)], axis=1) / np.sqrt(FOURIER_DIM)
    k1 = np.arange(FFT_N1)
    n1 = np.arange(FFT_N1)
    n2 = np.arange(FFT_N2)
    n = FFT_N2 * n1[None, None, :] + n2[:, None, None]
    ang1 = 2.0 * np.pi * ((k1[None, :, None] * n) % SEQ) / SEQ
    c1, s1 = np.cos(ang1) / np.sqrt(FFT_N1), np.sin(ang1) / np.sqrt(FFT_N1)
    seq1 = np.concatenate([np.concatenate([c1, -s1], axis=2), np.concatenate([s1, c1], axis=2)], axis=1)
    k2 = np.arange(FFT_N2)
    ang2 = 2.0 * np.pi * ((k2[:, None] * n2[None, :]) % FFT_N2) / FFT_N2
    seq2 = np.concatenate([np.cos(ang2), -np.sin(ang2)], axis=1) / np.sqrt(FFT_N2)
    return (jnp.asarray(feat, BF16), jnp.asarray(seq1, BF16), jnp.asarray(seq2, BF16))


def _constants():
    es, eb = _head_sum_matrices(NA_DIM)
    esm, ebm = _head_sum_matrices(MEM_DIM)
    feat, seq1, seq2 = _dft_constants()
    return dict(es=es, eb=eb, esm=esm, ebm=ebm, dft_feat=feat, dft_seq1=seq1, dft_seq2=seq2)


def _na_bias_table(rpb):
    shift = np.arange(KERNEL_ROWS)
    off = np.arange(KERNEL_ROWS)
    dr = shift[:, None] + off[None, :]
    c = np.arange(GRID_W)
    dc = np.clip(c[None, :] - c[:, None], -(KERNEL_COLS - 1), KERNEL_COLS - 1) + (KERNEL_COLS - 1)
    cs = np.clip(c - KERNEL_COLS // 2, 0, GRID_W - KERNEL_COLS)
    inside = (c[None, :] >= cs[:, None]) & (c[None, :] < cs[:, None] + KERNEL_COLS)
    t = rpb.astype(F32)[:, dr]
    t = jnp.take(t, jnp.asarray(dc), axis=3)
    t = t.transpose(0, 1, 3, 2, 4)
    t = jnp.where(jnp.asarray(inside)[None, None, :, None, :], t, NEG_INF)
    t = t.reshape(NA_HEADS // NA_HEADS_PER_STEP, NA_HEADS_PER_STEP, KERNEL_ROWS, GRID_W, NA_KEYS)
    t = t.transpose(0, 2, 1, 3, 4)
    return t.reshape(NA_HEADS // NA_HEADS_PER_STEP, KERNEL_ROWS, LANES, NA_KEYS)


def _ffn_weights(w_in, w_out):
    gate = w_in[:, :D_FF].reshape(D_MODEL, N_FF_CHUNKS, FF_CHUNK)
    up = w_in[:, D_FF:].reshape(D_MODEL, N_FF_CHUNKS, FF_CHUNK)
    win_c = jnp.concatenate([gate, up], axis=2).transpose(1, 0, 2).astype(BF16)
    wout_c = w_out.reshape(N_FF_CHUNKS, FF_CHUNK, D_MODEL).astype(BF16)
    return win_c, wout_c


def _row(v):
    return v.reshape(1, -1).astype(F32)


def _tile_heads(g, heads):
    return jnp.tile(g.astype(F32), heads).reshape(1, -1)


def _trunk(x, mem, p, consts):
    b = x.shape[0]
    x2 = x.reshape(b * SEQ, D_MODEL)
    for i in range(DEPTH):
        x2 = _ffn(x2, p["norm_ffn1"][i], *p["ffn1"][i], p["norm_out"][i], False)
        km, vm = _memkv(mem, p["norm_mem"][i], p["w_mem_kv"][i], p["mem_k_norm"][i], consts)
        if i % 2 == 0:
            q, k, v, qm = _mixin_a(x2, p["norm_mix"][i], p["w_in_a"], p["na_q_norm"], p["na_k_norm"],
                                   p["mem_q_norm"][i], consts)
            seq3 = lambda t: t.reshape(b, SEQ, t.shape[-1])
            mixed = _na(seq3(q), seq3(k), seq3(v), p["na_bias"])
            w_out = p["w_out_a"]
        else:
            a, bb, qm = _mixin_b(x2, p["norm_mix"][i], p["w_in_b"], p["mem_q_norm"][i], consts)
            mixed = _seq_dft_real(a.reshape(b, SEQ, FOURIER_DIM), bb.reshape(b, SEQ, FOURIER_DIM), consts)
            w_out = p["w_out_b"]
        x3 = _mixout(x2.reshape(b, SEQ, D_MODEL), mixed, qm.reshape(b, SEQ, MEM_DIM), km, vm,
                     w_out[0], w_out[1])
        x2 = x3.reshape(b * SEQ, D_MODEL)
        x2 = _ffn(x2, p["norm_ffn2"][i], *p["ffn2"][i], p["norm_out"][i], True)
    return x2.reshape(b, SEQ, D_MODEL)


def kernel(x_prompt, x_sample, mem_prompt, mem_sample, norm_ffn1, w_ffn1_in, w_ffn1_out, norm_mix, norm_mem,
           w_mem_kv, mem_q_norm, mem_k_norm, w_in_a, na_q_norm, na_k_norm, na_rpb, w_out_a, w_in_b, w_out_b,
           norm_ffn2, w_ffn2_in, w_ffn2_out, norm_out):
    assert x_prompt.shape[1:] == (SEQ, D_MODEL) and x_sample.shape[1:] == (SEQ, D_MODEL)
    consts = _constants()
    split_out = lambda w: (w[:MEM_DIM * 3].astype(BF16), w[MEM_DIM * 3:].astype(BF16))
    p = dict(
        norm_ffn1=[_row(norm_ffn1[i]) for i in range(DEPTH)],
        norm_ffn2=[_row(norm_ffn2[i]) for i in range(DEPTH)],
        norm_mix=[_row(norm_mix[i]) for i in range(DEPTH)],
        norm_mem=[_row(norm_mem[i]) for i in range(DEPTH)],
        norm_out=[_row(norm_out[i]) for i in range(DEPTH)],
        ffn1=[_ffn_weights(w_ffn1_in[i], w_ffn1_out[i]) for i in range(DEPTH)],
        ffn2=[_ffn_weights(w_ffn2_in[i], w_ffn2_out[i]) for i in range(DEPTH)],
        w_mem_kv=[w_mem_kv[i].astype(BF16) for i in range(DEPTH)],
        mem_q_norm=[_tile_heads(mem_q_norm[i], MEM_HEADS) for i in range(DEPTH)],
        mem_k_norm=[_tile_heads(mem_k_norm[i], MEM_HEADS) for i in range(DEPTH)],
        w_in_a=w_in_a[0].astype(BF16),
        na_q_norm=_tile_heads(na_q_norm[0], NA_HEADS),
        na_k_norm=_tile_heads(na_k_norm[0], NA_HEADS),
        na_bias=_na_bias_table(na_rpb[0]),
        w_out_a=split_out(w_out_a[0]),
        w_in_b=w_in_b[0].astype(BF16),
        w_out_b=split_out(w_out_b[0]),
    )
    return (_trunk(x_prompt, mem_prompt, p, consts), _trunk(x_sample, mem_sample, p, consts))
```

```python
import functools

import numpy as np
import jax
import jax.numpy as jnp
from jax import lax
from jax.experimental import pallas as pl
from jax.experimental.pallas import tpu as pltpu

D_MODEL = 1024
DEPTH = 2
HEAD_DIM = 64
MEM_HEADS = 4
MEM_DIM = MEM_HEADS * HEAD_DIM
N_MEM = 256
NA_HEADS = 12
NA_DIM = NA_HEADS * HEAD_DIM
FOURIER_DIM = 768
GRID_W = 64
KERNEL_ROWS = 8
KERNEL_COLS = 16
D_FF = 2816
EPS = 1e-6
NEG_INF = -1e30
SEQ = 8192
ROWS = SEQ // GRID_W
FFT_N1 = 64
FFT_N2 = 128

F32 = jnp.float32
BF16 = jnp.bfloat16

LANES = 128
FF_CHUNK = 256
N_FF_CHUNKS = D_FF // FF_CHUNK
TOKEN_TILE = 512
VMEM_LIMIT = 56 * 1024 * 1024


def _params(n_axes):
    return pltpu.CompilerParams(dimension_semantics=("arbitrary",) * n_axes,
                                vmem_limit_bytes=VMEM_LIMIT)


def _const_spec(shape):
    n = len(shape)
    return pl.BlockSpec(shape, lambda *_: (0,) * n, pipeline_mode=pl.Buffered(1))


def _rms(x, g):
    ms = jnp.mean(x * x, axis=-1, keepdims=True)
    return x * lax.rsqrt(ms + EPS) * g


def _split_dot(a, e):
    hi = a.astype(BF16)
    lo = (a - hi.astype(F32)).astype(BF16)
    return (jnp.dot(hi, e, preferred_element_type=F32)
            + jnp.dot(lo, e, preferred_element_type=F32))


def _head_rmsnorm(t, gain, e_sum, e_bcast):
    ms = jnp.dot((t * t).astype(BF16), e_sum, preferred_element_type=F32)
    inv = lax.rsqrt(ms + EPS)
    return t * _split_dot(inv, e_bcast) * gain


def _head_lane_mask(shape, head, lane_axis):
    lane = lax.broadcasted_iota(jnp.int32, shape, lane_axis)
    return (lane >= head * HEAD_DIM) & (lane < (head + 1) * HEAD_DIM)


def _ffn_kernel(x_ref, g_ref, win_ref, wout_ref, gout_ref, o_ref, xn_ref, acc_ref, *, final_norm):
    xn_ref[...] = _rms(x_ref[...], g_ref[...]).astype(BF16)
    for c in range(N_FF_CHUNKS):
        h = jnp.dot(xn_ref[...], win_ref[c], preferred_element_type=F32)
        gate = h[:, :FF_CHUNK]
        up = h[:, FF_CHUNK:]
        act = (gate * jax.nn.sigmoid(gate) * up).astype(BF16)
        part = jnp.dot(act, wout_ref[c], preferred_element_type=F32)
        if c == 0:
            acc_ref[...] = part
        elif c < N_FF_CHUNKS - 1:
            acc_ref[...] += part
        else:
            y = x_ref[...] + 0.5 * (acc_ref[...] + part)
    if final_norm:
        y = _rms(y, gout_ref[...])
    o_ref[...] = y


def _ffn(x2d, g, win_c, wout_c, gout, final_norm):
    t = x2d.shape[0]
    tm = TOKEN_TILE
    return pl.pallas_call(
        functools.partial(_ffn_kernel, final_norm=final_norm),
        out_shape=jax.ShapeDtypeStruct((t, D_MODEL), F32),
        grid=(t // tm,),
        in_specs=[
            pl.BlockSpec((tm, D_MODEL), lambda i: (i, 0)),
            _const_spec((1, D_MODEL)),
            _const_spec((N_FF_CHUNKS, D_MODEL, 2 * FF_CHUNK)),
            _const_spec((N_FF_CHUNKS, FF_CHUNK, D_MODEL)),
            _const_spec((1, D_MODEL)),
        ],
        out_specs=pl.BlockSpec((tm, D_MODEL), lambda i: (i, 0)),
        scratch_shapes=[pltpu.VMEM((tm, D_MODEL), BF16), pltpu.VMEM((tm, D_MODEL), F32)],
        compiler_params=_params(1),
        name="ffn_final" if final_norm else "ffn",
    )(x2d, g, win_c, wout_c, gout)


def _mixin_a_kernel(x_ref, g_ref, w_ref, gq_ref, gk_ref, gqm_ref, es_ref, eb_ref, esm_ref, ebm_ref,
                    q_ref, k_ref, v_ref, qm_ref):
    h = _rms(x_ref[...], g_ref[...]).astype(BF16)
    proj = jnp.dot(h, w_ref[...], preferred_element_type=F32)
    q = proj[:, :NA_DIM]
    k = proj[:, NA_DIM:2 * NA_DIM]
    v = proj[:, 2 * NA_DIM:3 * NA_DIM]
    qm = proj[:, 3 * NA_DIM:]
    scale = HEAD_DIM ** -0.5
    q_ref[...] = (_head_rmsnorm(q, gq_ref[...], es_ref[...], eb_ref[...]) * scale).astype(BF16)
    k_ref[...] = _head_rmsnorm(k, gk_ref[...], es_ref[...], eb_ref[...]).astype(BF16)
    v_ref[...] = v.astype(BF16)
    qm_ref[...] = (_head_rmsnorm(qm, gqm_ref[...], esm_ref[...], ebm_ref[...]) * scale).astype(BF16)


def _mixin_a(x2d, g, w, gq, gk, gqm, consts):
    t = x2d.shape[0]
    tm = TOKEN_TILE
    width = 3 * NA_DIM + MEM_DIM
    tok = lambda n: pl.BlockSpec((tm, n), lambda i: (i, 0))
    return pl.pallas_call(
        _mixin_a_kernel,
        out_shape=(jax.ShapeDtypeStruct((t, NA_DIM), BF16),) * 3 + (jax.ShapeDtypeStruct((t, MEM_DIM), BF16),),
        grid=(t // tm,),
        in_specs=[
            tok(D_MODEL),
            _const_spec((1, D_MODEL)),
            _const_spec((D_MODEL, width)),
            _const_spec((1, NA_DIM)), _const_spec((1, NA_DIM)), _const_spec((1, MEM_DIM)),
            _const_spec((NA_DIM, LANES)), _const_spec((LANES, NA_DIM)),
            _const_spec((MEM_DIM, LANES)), _const_spec((LANES, MEM_DIM)),
        ],
        out_specs=(tok(NA_DIM), tok(NA_DIM), tok(NA_DIM), tok(MEM_DIM)),
        compiler_params=_params(1),
        name="mixin_na",
    )(x2d, g, w, gq, gk, gqm, consts["es"], consts["eb"], consts["esm"], consts["ebm"])


def _mixin_b_kernel(x_ref, g_ref, w_ref, gqm_ref, esm_ref, ebm_ref, dft_ref, a_ref, b_ref, qm_ref):
    h = _rms(x_ref[...], g_ref[...]).astype(BF16)
    proj = jnp.dot(h, w_ref[...], preferred_element_type=F32)
    z = proj[:, :FOURIER_DIM].astype(BF16)
    qm = proj[:, FOURIER_DIM:]
    ab = jnp.dot(z, dft_ref[...], preferred_element_type=F32)
    a_ref[...] = ab[:, :FOURIER_DIM].astype(BF16)
    b_ref[...] = ab[:, FOURIER_DIM:].astype(BF16)
    scale = HEAD_DIM ** -0.5
    qm_ref[...] = (_head_rmsnorm(qm, gqm_ref[...], esm_ref[...], ebm_ref[...]) * scale).astype(BF16)


def _mixin_b(x2d, g, w, gqm, consts):
    t = x2d.shape[0]
    tm = TOKEN_TILE
    tok = lambda n: pl.BlockSpec((tm, n), lambda i: (i, 0))
    return pl.pallas_call(
        _mixin_b_kernel,
        out_shape=(jax.ShapeDtypeStruct((t, FOURIER_DIM), BF16),) * 2 + (jax.ShapeDtypeStruct((t, MEM_DIM), BF16),),
        grid=(t // tm,),
        in_specs=[
            tok(D_MODEL),
            _const_spec((1, D_MODEL)),
            _const_spec((D_MODEL, FOURIER_DIM + MEM_DIM)),
            _const_spec((1, MEM_DIM)),
            _const_spec((MEM_DIM, LANES)), _const_spec((LANES, MEM_DIM)),
            _const_spec((FOURIER_DIM, 2 * FOURIER_DIM)),
        ],
        out_specs=(tok(FOURIER_DIM), tok(FOURIER_DIM), tok(MEM_DIM)),
        compiler_params=_params(1),
        name="mixin_fnet",
    )(x2d, g, w, gqm, consts["esm"], consts["ebm"], consts["dft_feat"])


NA_HEADS_PER_STEP = LANES // HEAD_DIM
NA_KEYS = KERNEL_ROWS * GRID_W
NA_ROW_UNROLL = 4


def _na_kernel(q_ref, k_ref, v_ref, bias_ref, o_ref):
    first_head = _head_lane_mask((GRID_W, LANES), 0, 1)

    def body(r, carry):
        rs = jnp.clip(r - KERNEL_ROWS // 2, 0, ROWS - KERNEL_ROWS)
        shift = rs - r + (KERNEL_ROWS - 1)
        q = q_ref[pl.ds(pl.multiple_of(r * GRID_W, GRID_W), GRID_W), :]
        zero = jnp.zeros_like(q)
        q2 = jnp.concatenate([jnp.where(first_head, q, zero), jnp.where(first_head, zero, q)], axis=0)
        key_start = pl.multiple_of(rs * GRID_W, GRID_W)
        kb = k_ref[pl.ds(key_start, NA_KEYS), :]
        vb = v_ref[pl.ds(key_start, NA_KEYS), :]
        s = lax.dot_general(q2, kb, (((1,), (1,)), ((), ())), preferred_element_type=F32)
        s = s + bias_ref[shift]
        m = jnp.max(s, axis=-1, keepdims=True)
        e = jnp.exp(s - m)
        l = jnp.sum(e, axis=-1, keepdims=True)
        o2 = jnp.dot(e.astype(BF16), vb, preferred_element_type=F32) * (1.0 / l)
        o = jnp.where(first_head, o2[:GRID_W], o2[GRID_W:])
        o_ref[pl.ds(pl.multiple_of(r * GRID_W, GRID_W), GRID_W), :] = o.astype(BF16)
        return carry

    lax.fori_loop(0, ROWS, body, 0, unroll=NA_ROW_UNROLL)


def _na(q3, k3, v3, bias):
    b = q3.shape[0]
    n_pairs = NA_HEADS // NA_HEADS_PER_STEP
    seq_spec = pl.BlockSpec((None, SEQ, LANES), lambda i, j: (i, 0, j))
    return pl.pallas_call(
        _na_kernel,
        out_shape=jax.ShapeDtypeStruct((b, SEQ, NA_DIM), BF16),
        grid=(b, n_pairs),
        in_specs=[seq_spec, seq_spec, seq_spec,
                  pl.BlockSpec((None, KERNEL_ROWS, LANES, NA_KEYS), lambda i, j: (j, 0, 0, 0))],
        out_specs=seq_spec,
        compiler_params=_params(2),
        name="na_attention",
    )(q3, k3, v3, bias)


FFT_G1 = 8
FFT_G2 = 8


def _fft_stage1_kernel(a_ref, b_ref, g_ref, yr_ref, yi_ref):
    for j in range(FFT_G1):
        cols = slice(j * FOURIER_DIM, (j + 1) * FOURIER_DIM)
        x = jnp.concatenate([a_ref[:, cols], b_ref[:, cols]], axis=0)
        y = jnp.dot(g_ref[j], x, preferred_element_type=F32)
        yr_ref[:, cols] = y[:FFT_N1].astype(BF16)
        yi_ref[:, cols] = y[FFT_N1:].astype(BF16)


def _fft_stage2_kernel(yr_ref, yi_ref, h_ref, f_ref):
    for j in range(FFT_G2):
        x = jnp.concatenate([yr_ref[j], yi_ref[j]], axis=0)
        f_ref[:, j * FOURIER_DIM:(j + 1) * FOURIER_DIM] = jnp.dot(
            h_ref[...], x, preferred_element_type=F32).astype(BF16)


def _seq_dft_real(a3, b3, consts):
    b = a3.shape[0]
    wide = FFT_N2 * FOURIER_DIM
    a_v = a3.reshape(b, FFT_N1, wide)
    b_v = b3.reshape(b, FFT_N1, wide)
    blk1 = pl.BlockSpec((None, FFT_N1, FFT_G1 * FOURIER_DIM), lambda i, j: (i, 0, j))
    yr, yi = pl.pallas_call(
        _fft_stage1_kernel,
        out_shape=(jax.ShapeDtypeStruct((b, FFT_N1, wide), BF16),) * 2,
        grid=(b, FFT_N2 // FFT_G1),
        in_specs=[blk1, blk1,
                  pl.BlockSpec((FFT_G1, 2 * FFT_N1, 2 * FFT_N1), lambda i, j: (j, 0, 0))],
        out_specs=(blk1, blk1),
        compiler_params=_params(2),
        name="fft_stage1",
    )(a_v, b_v, consts["dft_seq1"])
    yr = yr.reshape(b, FFT_N1, FFT_N2, FOURIER_DIM)
    yi = yi.reshape(b, FFT_N1, FFT_N2, FOURIER_DIM)
    blk2 = pl.BlockSpec((None, FFT_G2, FFT_N2, FOURIER_DIM), lambda i, j: (i, j, 0, 0))
    f = pl.pallas_call(
        _fft_stage2_kernel,
        out_shape=jax.ShapeDtypeStruct((b, FFT_N2, FFT_N1 * FOURIER_DIM), BF16),
        grid=(b, FFT_N1 // FFT_G2),
        in_specs=[blk2, blk2, _const_spec((FFT_N2, 2 * FFT_N2))],
        out_specs=pl.BlockSpec((None, FFT_N2, FFT_G2 * FOURIER_DIM), lambda i, j: (i, 0, j)),
        compiler_params=_params(2),
        name="fft_stage2",
    )(yr, yi, consts["dft_seq2"])
    return f.reshape(b, SEQ, FOURIER_DIM)


def _memkv_kernel(mem_ref, g_ref, w_ref, gk_ref, esm_ref, ebm_ref, k_ref, v_ref):
    h = _rms(mem_ref[...], g_ref[...]).astype(BF16)
    kv = jnp.dot(h, w_ref[...], preferred_element_type=F32)
    k = _head_rmsnorm(kv[:, :MEM_DIM], gk_ref[...], esm_ref[...], ebm_ref[...])
    k_ref[...] = k.T.astype(BF16)
    v_ref[...] = kv[:, MEM_DIM:].astype(BF16)


def _memkv(mem, g, w, gk, consts):
    b = mem.shape[0]
    out_blk = pl.BlockSpec((None, N_MEM, MEM_DIM), lambda i: (i, 0, 0))
    return pl.pallas_call(
        _memkv_kernel,
        out_shape=(jax.ShapeDtypeStruct((b, N_MEM, MEM_DIM), BF16),) * 2,
        grid=(b,),
        in_specs=[
            pl.BlockSpec((None, N_MEM, D_MODEL), lambda i: (i, 0, 0)),
            _const_spec((1, D_MODEL)),
            _const_spec((D_MODEL, 2 * MEM_DIM)),
            _const_spec((1, MEM_DIM)),
            _const_spec((MEM_DIM, LANES)), _const_spec((LANES, MEM_DIM)),
        ],
        out_specs=(out_blk, out_blk),
        compiler_params=_params(1),
        name="memory_kv",
    )(mem, g, w, gk, consts["esm"], consts["ebm"])


def _mixout_kernel(x_ref, a_ref, qm_ref, km_ref, vm_ref, wa_ref, wm_ref, o_ref):
    qm = qm_ref[...]
    km = km_ref[...]
    vm = vm_ref[...]
    zero = jnp.zeros_like(qm)
    mo = jnp.zeros(qm.shape, F32)
    for head in range(MEM_HEADS):
        in_head = _head_lane_mask(qm.shape, head, 1)
        s = jnp.dot(jnp.where(in_head, qm, zero), km, preferred_element_type=F32)
        m = jnp.max(s, axis=-1, keepdims=True)
        e = jnp.exp(s - m)
        l = jnp.sum(e, axis=-1, keepdims=True)
        oh = jnp.dot(e.astype(BF16), vm, preferred_element_type=F32) * (1.0 / l)
        mo = jnp.where(in_head, oh, mo)
    y = jnp.dot(a_ref[...], wa_ref[...], preferred_element_type=F32)
    y = y + jnp.dot(mo.astype(BF16), wm_ref[...], preferred_element_type=F32)
    o_ref[...] = x_ref[...] + y


def _mixout(x3, a3, qm3, km, vm, w_a, w_m):
    b = x3.shape[0]
    tm = TOKEN_TILE
    tok = lambda n: pl.BlockSpec((None, tm, n), lambda i, j: (i, j, 0))
    mem_blk = pl.BlockSpec((None, N_MEM, MEM_DIM), lambda i, j: (i, 0, 0))
    mix = a3.shape[-1]
    return pl.pallas_call(
        _mixout_kernel,
        out_shape=jax.ShapeDtypeStruct(x3.shape, F32),
        grid=(b, SEQ // tm),
        in_specs=[tok(D_MODEL), tok(mix), tok(MEM_DIM), mem_blk, mem_blk,
                  _const_spec((mix, D_MODEL)), _const_spec((MEM_DIM, D_MODEL))],
        out_specs=tok(D_MODEL),
        compiler_params=_params(2),
        name="mixer_out",
    )(x3, a3, qm3, km, vm, w_a, w_m)


def _head_sum_matrices(width):
    head = np.arange(width) // HEAD_DIM
    e_sum = (head[:, None] == np.arange(LANES)[None, :]).astype(np.float32) / HEAD_DIM
    e_bcast = (np.arange(LANES)[:, None] == head[None, :]).astype(np.float32)
    return jnp.asarray(e_sum, BF16), jnp.asarray(e_bcast, BF16)


def _dft_constants():
    d = np.arange(FOURIER_DIM)
    ang = 2.0 * np.pi * ((d[:, None] * d[None, :]) % FOURIER_DIM) / FOURIER_DIM
    feat = np.concatenate([np.cos(ang), np.sin(ang)], axis=1) / np.sqrt(FOURIER_DIM)
    k1 = np.arange(FFT_N1)
    n1 = np.arange(FFT_N1)
    n2 = np.arange(FFT_N2)
    n = FFT_N2 * n1[None, None, :] + n2[:, None, None]
    ang1 = 2.0 * np.pi * ((k1[None, :, None] * n) % SEQ) / SEQ
    c1, s1 = np.cos(ang1) / np.sqrt(FFT_N1), np.sin(ang1) / np.sqrt(FFT_N1)
    seq1 = np.concatenate([np.concatenate([c1, -s1], axis=2), np.concatenate([s1, c1], axis=2)], axis=1)
    k2 = np.arange(FFT_N2)
    ang2 = 2.0 * np.pi * ((k2[:, None] * n2[None, :]) % FFT_N2) / FFT_N2
    seq2 = np.concatenate([np.cos(ang2), -np.sin(ang2)], axis=1) / np.sqrt(FFT_N2)
    return (jnp.asarray(feat, BF16), jnp.asarray(seq1, BF16), jnp.asarray(seq2, BF16))


def _constants():
    es, eb = _head_sum_matrices(NA_DIM)
    esm, ebm = _head_sum_matrices(MEM_DIM)
    feat, seq1, seq2 = _dft_constants()
    return dict(es=es, eb=eb, esm=esm, ebm=ebm, dft_feat=feat, dft_seq1=seq1, dft_seq2=seq2)


def _na_bias_table(rpb):
    shift = np.arange(KERNEL_ROWS)
    off = np.arange(KERNEL_ROWS)
    dr = shift[:, None] + off[None, :]
    c = np.arange(GRID_W)
    dc = np.clip(c[None, :] - c[:, None], -(KERNEL_COLS - 1), KERNEL_COLS - 1) + (KERNEL_COLS - 1)
    cs = np.clip(c - KERNEL_COLS // 2, 0, GRID_W - KERNEL_COLS)
    inside = (c[None, :] >= cs[:, None]) & (c[None, :] < cs[:, None] + KERNEL_COLS)
    t = rpb.astype(F32)[:, dr]
    t = jnp.take(t, jnp.asarray(dc), axis=3)
    t = t.transpose(0, 1, 3, 2, 4)
    t = jnp.where(jnp.asarray(inside)[None, None, :, None, :], t, NEG_INF)
    t = t.reshape(NA_HEADS // NA_HEADS_PER_STEP, NA_HEADS_PER_STEP, KERNEL_ROWS, GRID_W, NA_KEYS)
    t = t.transpose(0, 2, 1, 3, 4)
    return t.reshape(NA_HEADS // NA_HEADS_PER_STEP, KERNEL_ROWS, LANES, NA_KEYS)


def _ffn_weights(w_in, w_out):
    gate = w_in[:, :D_FF].reshape(D_MODEL, N_FF_CHUNKS, FF_CHUNK)
    up = w_in[:, D_FF:].reshape(D_MODEL, N_FF_CHUNKS, FF_CHUNK)
    win_c = jnp.concatenate([gate, up], axis=2).transpose(1, 0, 2).astype(BF16)
    wout_c = w_out.reshape(N_FF_CHUNKS, FF_CHUNK, D_MODEL).astype(BF16)
    return win_c, wout_c


def _row(v):
    return v.reshape(1, -1).astype(F32)


def _tile_heads(g, heads):
    return jnp.tile(g.astype(F32), heads).reshape(1, -1)


def _trunk(x, mem, p, consts):
    b = x.shape[0]
    x2 = x.reshape(b * SEQ, D_MODEL)
    for i in range(DEPTH):
        x2 = _ffn(x2, p["norm_ffn1"][i], *p["ffn1"][i], p["norm_out"][i], False)
        km, vm = _memkv(mem, p["norm_mem"][i], p["w_mem_kv"][i], p["mem_k_norm"][i], consts)
        if i % 2 == 0:
            q, k, v, qm = _mixin_a(x2, p["norm_mix"][i], p["w_in_a"], p["na_q_norm"], p["na_k_norm"],
                                   p["mem_q_norm"][i], consts)
            seq3 = lambda t: t.reshape(b, SEQ, t.shape[-1])
            mixed = _na(seq3(q), seq3(k), seq3(v), p["na_bias"])
            w_out = p["w_out_a"]
        else:
            a, bb, qm = _mixin_b(x2, p["norm_mix"][i], p["w_in_b"], p["mem_q_norm"][i], consts)
            mixed = _seq_dft_real(a.reshape(b, SEQ, FOURIER_DIM), bb.reshape(b, SEQ, FOURIER_DIM), consts)
            w_out = p["w_out_b"]
        x3 = _mixout(x2.reshape(b, SEQ, D_MODEL), mixed, qm.reshape(b, SEQ, MEM_DIM), km, vm,
                     w_out[0], w_out[1])
        x2 = x3.reshape(b * SEQ, D_MODEL)
        x2 = _ffn(x2, p["norm_ffn2"][i], *p["ffn2"][i], p["norm_out"][i], True)
    return x2.reshape(b, SEQ, D_MODEL)


def kernel(x_prompt, x_sample, mem_prompt, mem_sample, norm_ffn1, w_ffn1_in, w_ffn1_out, norm_mix, norm_mem,
           w_mem_kv, mem_q_norm, mem_k_norm, w_in_a, na_q_norm, na_k_norm, na_rpb, w_out_a, w_in_b, w_out_b,
           norm_ffn2, w_ffn2_in, w_ffn2_out, norm_out):
    assert x_prompt.shape[1:] == (SEQ, D_MODEL) and x_sample.shape[1:] == (SEQ, D_MODEL)
    consts = _constants()
    split_out = lambda w: (w[:MEM_DIM * 3].astype(BF16), w[MEM_DIM * 3:].astype(BF16))
    p = dict(
        norm_ffn1=[_row(norm_ffn1[i]) for i in range(DEPTH)],
        norm_ffn2=[_row(norm_ffn2[i]) for i in range(DEPTH)],
        norm_mix=[_row(norm_mix[i]) for i in range(DEPTH)],
        norm_mem=[_row(norm_mem[i]) for i in range(DEPTH)],
        norm_out=[_row(norm_out[i]) for i in range(DEPTH)],
        ffn1=[_ffn_weights(w_ffn1_in[i], w_ffn1_out[i]) for i in range(DEPTH)],
        ffn2=[_ffn_weights(w_ffn2_in[i], w_ffn2_out[i]) for i in range(DEPTH)],
        w_mem_kv=[w_mem_kv[i].astype(BF16) for i in range(DEPTH)],
        mem_q_norm=[_tile_heads(mem_q_norm[i], MEM_HEADS) for i in range(DEPTH)],
        mem_k_norm=[_tile_heads(mem_k_norm[i], MEM_HEADS) for i in range(DEPTH)],
        w_in_a=w_in_a[0].astype(BF16),
        na_q_norm=_tile_heads(na_q_norm[0], NA_HEADS),
        na_k_norm=_tile_heads(na_k_norm[0], NA_HEADS),
        na_bias=_na_bias_table(na_rpb[0]),
        w_out_a=split_out(w_out_a[0]),
        w_in_b=w_in_b[0].astype(BF16),
        w_out_b=split_out(w_out_b[0]),
    )
    return (_trunk(x_prompt, mem_prompt, p, consts), _trunk(x_sample, mem_sample, p, consts))
```

```python
import functools

import numpy as np
import jax
import jax.numpy as jnp
from jax import lax
from jax.experimental import pallas as pl
from jax.experimental.pallas import tpu as pltpu

D_MODEL = 1024
DEPTH = 2
HEAD_DIM = 64
MEM_HEADS = 4
MEM_DIM = MEM_HEADS * HEAD_DIM
N_MEM = 256
NA_HEADS = 12
NA_DIM = NA_HEADS * HEAD_DIM
FOURIER_DIM = 768
GRID_W = 64
KERNEL_ROWS = 8
KERNEL_COLS = 16
D_FF = 2816
EPS = 1e-6
NEG_INF = -1e30
SEQ = 8192
ROWS = SEQ // GRID_W
FFT_N1 = 64
FFT_N2 = 128

F32 = jnp.float32
BF16 = jnp.bfloat16

LANES = 128
FF_CHUNK = 256
N_FF_CHUNKS = D_FF // FF_CHUNK
TOKEN_TILE = 512
VMEM_LIMIT = 56 * 1024 * 1024


def _params(n_axes):
    return pltpu.CompilerParams(dimension_semantics=("arbitrary",) * n_axes,
                                vmem_limit_bytes=VMEM_LIMIT)


def _const_spec(shape):
    n = len(shape)
    return pl.BlockSpec(shape, lambda *_: (0,) * n, pipeline_mode=pl.Buffered(1))


def _rms(x, g):
    ms = jnp.mean(x * x, axis=-1, keepdims=True)
    return x * lax.rsqrt(ms + EPS) * g


def _split_dot(a, e):
    hi = a.astype(BF16)
    lo = (a - hi.astype(F32)).astype(BF16)
    return (jnp.dot(hi, e, preferred_element_type=F32)
            + jnp.dot(lo, e, preferred_element_type=F32))


def _head_rmsnorm(t, gain, e_sum, e_bcast):
    ms = jnp.dot((t * t).astype(BF16), e_sum, preferred_element_type=F32)
    inv = lax.rsqrt(ms + EPS)
    return t * _split_dot(inv, e_bcast) * gain


def _head_lane_mask(shape, head, lane_axis):
    lane = lax.broadcasted_iota(jnp.int32, shape, lane_axis)
    return (lane >= head * HEAD_DIM) & (lane < (head + 1) * HEAD_DIM)


def _ffn_kernel(x_ref, g_ref, win_ref, wout_ref, gout_ref, o_ref, xn_ref, acc_ref, *, final_norm):
    xn_ref[...] = _rms(x_ref[...], g_ref[...]).astype(BF16)
    for c in range(N_FF_CHUNKS):
        h = jnp.dot(xn_ref[...], win_ref[c], preferred_element_type=F32)
        gate = h[:, :FF_CHUNK]
        up = h[:, FF_CHUNK:]
        act = (gate * jax.nn.sigmoid(gate) * up).astype(BF16)
        part = jnp.dot(act, wout_ref[c], preferred_element_type=F32)
        if c == 0:
            acc_ref[...] = part
        elif c < N_FF_CHUNKS - 1:
            acc_ref[...] += part
        else:
            y = x_ref[...] + 0.5 * (acc_ref[...] + part)
    if final_norm:
        y = _rms(y, gout_ref[...])
    o_ref[...] = y


def _ffn(x2d, g, win_c, wout_c, gout, final_norm):
    t = x2d.shape[0]
    tm = TOKEN_TILE
    return pl.pallas_call(
        functools.partial(_ffn_kernel, final_norm=final_norm),
        out_shape=jax.ShapeDtypeStruct((t, D_MODEL), F32),
        grid=(t // tm,),
        in_specs=[
            pl.BlockSpec((tm, D_MODEL), lambda i: (i, 0)),
            _const_spec((1, D_MODEL)),
            _const_spec((N_FF_CHUNKS, D_MODEL, 2 * FF_CHUNK)),
            _const_spec((N_FF_CHUNKS, FF_CHUNK, D_MODEL)),
            _const_spec((1, D_MODEL)),
        ],
        out_specs=pl.BlockSpec((tm, D_MODEL), lambda i: (i, 0)),
        scratch_shapes=[pltpu.VMEM((tm, D_MODEL), BF16), pltpu.VMEM((tm, D_MODEL), F32)],
        compiler_params=_params(1),
        name="ffn_final" if final_norm else "ffn",
    )(x2d, g, win_c, wout_c, gout)


def _mixin_a_kernel(x_ref, g_ref, w_ref, gq_ref, gk_ref, gqm_ref, es_ref, eb_ref, esm_ref, ebm_ref,
                    q_ref, k_ref, v_ref, qm_ref):
    h = _rms(x_ref[...], g_ref[...]).astype(BF16)
    proj = jnp.dot(h, w_ref[...], preferred_element_type=F32)
    q = proj[:, :NA_DIM]
    k = proj[:, NA_DIM:2 * NA_DIM]
    v = proj[:, 2 * NA_DIM:3 * NA_DIM]
    qm = proj[:, 3 * NA_DIM:]
    scale = HEAD_DIM ** -0.5
    q_ref[...] = (_head_rmsnorm(q, gq_ref[...], es_ref[...], eb_ref[...]) * scale).astype(BF16)
    k_ref[...] = _head_rmsnorm(k, gk_ref[...], es_ref[...], eb_ref[...]).astype(BF16)
    v_ref[...] = v.astype(BF16)
    qm_ref[...] = (_head_rmsnorm(qm, gqm_ref[...], esm_ref[...], ebm_ref[...]) * scale).astype(BF16)


def _mixin_a(x2d, g, w, gq, gk, gqm, consts):
    t = x2d.shape[0]
    tm = TOKEN_TILE
    width = 3 * NA_DIM + MEM_DIM
    tok = lambda n: pl.BlockSpec((tm, n), lambda i: (i, 0))
    return pl.pallas_call(
        _mixin_a_kernel,
        out_shape=(jax.ShapeDtypeStruct((t, NA_DIM), BF16),) * 3 + (jax.ShapeDtypeStruct((t, MEM_DIM), BF16),),
        grid=(t // tm,),
        in_specs=[
            tok(D_MODEL),
            _const_spec((1, D_MODEL)),
            _const_spec((D_MODEL, width)),
            _const_spec((1, NA_DIM)), _const_spec((1, NA_DIM)), _const_spec((1, MEM_DIM)),
            _const_spec((NA_DIM, LANES)), _const_spec((LANES, NA_DIM)),
            _const_spec((MEM_DIM, LANES)), _const_spec((LANES, MEM_DIM)),
        ],
        out_specs=(tok(NA_DIM), tok(NA_DIM), tok(NA_DIM), tok(MEM_DIM)),
        compiler_params=_params(1),
        name="mixin_na",
    )(x2d, g, w, gq, gk, gqm, consts["es"], consts["eb"], consts["esm"], consts["ebm"])


def _mixin_b_kernel(x_ref, g_ref, w_ref, gqm_ref, esm_ref, ebm_ref, dft_ref, a_ref, b_ref, qm_ref):
    h = _rms(x_ref[...], g_ref[...]).astype(BF16)
    proj = jnp.dot(h, w_ref[...], preferred_element_type=F32)
    z = proj[:, :FOURIER_DIM].astype(BF16)
    qm = proj[:, FOURIER_DIM:]
    ab = jnp.dot(z, dft_ref[...], preferred_element_type=F32)
    a_ref[...] = ab[:, :FOURIER_DIM].astype(BF16)
    b_ref[...] = ab[:, FOURIER_DIM:].astype(BF16)
    scale = HEAD_DIM ** -0.5
    qm_ref[...] = (_head_rmsnorm(qm, gqm_ref[...], esm_ref[...], ebm_ref[...]) * scale).astype(BF16)


def _mixin_b(x2d, g, w, gqm, consts):
    t = x2d.shape[0]
    tm = TOKEN_TILE
    tok = lambda n: pl.BlockSpec((tm, n), lambda i: (i, 0))
    return pl.pallas_call(
        _mixin_b_kernel,
        out_shape=(jax.ShapeDtypeStruct((t, FOURIER_DIM), BF16),) * 2 + (jax.ShapeDtypeStruct((t, MEM_DIM), BF16),),
        grid=(t // tm,),
        in_specs=[
            tok(D_MODEL),
            _const_spec((1, D_MODEL)),
            _const_spec((D_MODEL, FOURIER_DIM + MEM_DIM)),
            _const_spec((1, MEM_DIM)),
            _const_spec((MEM_DIM, LANES)), _const_spec((LANES, MEM_DIM)),
            _const_spec((FOURIER_DIM, 2 * FOURIER_DIM)),
        ],
        out_specs=(tok(FOURIER_DIM), tok(FOURIER_DIM), tok(MEM_DIM)),
        compiler_params=_params(1),
        name="mixin_fnet",
    )(x2d, g, w, gqm, consts["esm"], consts["ebm"], consts["dft_feat"])


NA_HEADS_PER_STEP = LANES // HEAD_DIM
NA_KEYS = KERNEL_ROWS * GRID_W
NA_TICK_ROWS = 4
NA_TICKS = ROWS // NA_TICK_ROWS


def _na_window(r):
    if isinstance(r, int):
        rs = min(max(r - KERNEL_ROWS // 2, 0), ROWS - KERNEL_ROWS)
        return rs * GRID_W, rs - r + (KERNEL_ROWS - 1)
    rs = jnp.clip(r - KERNEL_ROWS // 2, 0, ROWS - KERNEL_ROWS)
    return pl.multiple_of(rs * GRID_W, GRID_W), rs - r + (KERNEL_ROWS - 1)


def _na_row_start(r):
    return r * GRID_W if isinstance(r, int) else pl.multiple_of(r * GRID_W, GRID_W)


def _na_kernel(q_ref, k_ref, v_ref, bias_ref, o_ref, s_ref, e_ref, l_ref):
    first_head = _head_lane_mask((GRID_W, LANES), 0, 1)

    def scores(t, slot):
        for j in range(NA_TICK_ROWS):
            r = t * NA_TICK_ROWS + j
            key_start, shift = _na_window(r)
            q = q_ref[pl.ds(_na_row_start(r), GRID_W), :]
            zero = jnp.zeros_like(q)
            q2 = jnp.concatenate([jnp.where(first_head, q, zero), jnp.where(first_head, zero, q)], axis=0)
            kb = k_ref[pl.ds(key_start, NA_KEYS), :]
            s = lax.dot_general(q2, kb, (((1,), (1,)), ((), ())), preferred_element_type=F32)
            s_ref[slot, j] = s + bias_ref[shift]

    def softmax(slot):
        for j in range(NA_TICK_ROWS):
            s = s_ref[slot, j]
            m = jnp.max(s, axis=-1, keepdims=True)
            e = jnp.exp(s - m)
            l = jnp.sum(e, axis=-1, keepdims=True)
            e_ref[slot, j] = e.astype(BF16)
            l_ref[slot, j] = jnp.broadcast_to(1.0 / l, (LANES, LANES))

    def values(t, slot):
        for j in range(NA_TICK_ROWS):
            r = t * NA_TICK_ROWS + j
            key_start, _ = _na_window(r)
            vb = v_ref[pl.ds(key_start, NA_KEYS), :]
            o2 = jnp.dot(e_ref[slot, j], vb, preferred_element_type=F32) * l_ref[slot, j]
            o = jnp.where(first_head, o2[:GRID_W], o2[GRID_W:])
            o_ref[pl.ds(_na_row_start(r), GRID_W), :] = o.astype(BF16)

    scores(0, 0)
    scores(1, 1)
    softmax(0)

    def body(u, carry):
        t = 2 * u
        scores(t, 0)
        values(t - 2, 0)
        softmax(1)
        scores(t + 1, 1)
        values(t - 1, 1)
        softmax(0)
        return carry

    lax.fori_loop(1, NA_TICKS // 2, body, 0)
    values(NA_TICKS - 2, 0)
    softmax(1)
    values(NA_TICKS - 1, 1)


def _na(q3, k3, v3, bias):
    b = q3.shape[0]
    n_pairs = NA_HEADS // NA_HEADS_PER_STEP
    seq_spec = pl.BlockSpec((None, SEQ, LANES), lambda i, j: (i, 0, j))
    return pl.pallas_call(
        _na_kernel,
        out_shape=jax.ShapeDtypeStruct((b, SEQ, NA_DIM), BF16),
        grid=(b, n_pairs),
        in_specs=[seq_spec, seq_spec, seq_spec,
                  pl.BlockSpec((None, KERNEL_ROWS, LANES, NA_KEYS), lambda i, j: (j, 0, 0, 0))],
        out_specs=seq_spec,
        scratch_shapes=[pltpu.VMEM((2, NA_TICK_ROWS, LANES, NA_KEYS), F32),
                        pltpu.VMEM((2, NA_TICK_ROWS, LANES, NA_KEYS), BF16),
                        pltpu.VMEM((2, NA_TICK_ROWS, LANES, LANES), F32)],
        compiler_params=_params(2),
        name="na_attention",
    )(q3, k3, v3, bias)


FFT_G1 = 8
FFT_G2 = 8


def _fft_stage1_kernel(a_ref, b_ref, g_ref, yr_ref, yi_ref):
    for j in range(FFT_G1):
        cols = slice(j * FOURIER_DIM, (j + 1) * FOURIER_DIM)
        x = jnp.concatenate([a_ref[:, cols], b_ref[:, cols]], axis=0)
        y = jnp.dot(g_ref[j], x, preferred_element_type=F32)
        yr_ref[:, cols] = y[:FFT_N1].astype(BF16)
        yi_ref[:, cols] = y[FFT_N1:].astype(BF16)


def _fft_stage2_kernel(yr_ref, yi_ref, h_ref, f_ref):
    for j in range(FFT_G2):
        x = jnp.concatenate([yr_ref[j], yi_ref[j]], axis=0)
        f_ref[:, j * FOURIER_DIM:(j + 1) * FOURIER_DIM] = jnp.dot(
            h_ref[...], x, preferred_element_type=F32).astype(BF16)


def _seq_dft_real(a3, b3, consts):
    b = a3.shape[0]
    wide = FFT_N2 * FOURIER_DIM
    a_v = a3.reshape(b, FFT_N1, wide)
    b_v = b3.reshape(b, FFT_N1, wide)
    blk1 = pl.BlockSpec((None, FFT_N1, FFT_G1 * FOURIER_DIM), lambda i, j: (i, 0, j))
    yr, yi = pl.pallas_call(
        _fft_stage1_kernel,
        out_shape=(jax.ShapeDtypeStruct((b, FFT_N1, wide), BF16),) * 2,
        grid=(b, FFT_N2 // FFT_G1),
        in_specs=[blk1, blk1,
                  pl.BlockSpec((FFT_G1, 2 * FFT_N1, 2 * FFT_N1), lambda i, j: (j, 0, 0))],
        out_specs=(blk1, blk1),
        compiler_params=_params(2),
        name="fft_stage1",
    )(a_v, b_v, consts["dft_seq1"])
    yr = yr.reshape(b, FFT_N1, FFT_N2, FOURIER_DIM)
    yi = yi.reshape(b, FFT_N1, FFT_N2, FOURIER_DIM)
    blk2 = pl.BlockSpec((None, FFT_G2, FFT_N2, FOURIER_DIM), lambda i, j: (i, j, 0, 0))
    f = pl.pallas_call(
        _fft_stage2_kernel,
        out_shape=jax.ShapeDtypeStruct((b, FFT_N2, FFT_N1 * FOURIER_DIM), BF16),
        grid=(b, FFT_N1 // FFT_G2),
        in_specs=[blk2, blk2, _const_spec((FFT_N2, 2 * FFT_N2))],
        out_specs=pl.BlockSpec((None, FFT_N2, FFT_G2 * FOURIER_DIM), lambda i, j: (i, 0, j)),
        compiler_params=_params(2),
        name="fft_stage2",
    )(yr, yi, consts["dft_seq2"])
    return f.reshape(b, SEQ, FOURIER_DIM)


def _memkv_kernel(mem_ref, g_ref, w_ref, gk_ref, esm_ref, ebm_ref, k_ref, v_ref):
    h = _rms(mem_ref[...], g_ref[...]).astype(BF16)
    kv = jnp.dot(h, w_ref[...], preferred_element_type=F32)
    k = _head_rmsnorm(kv[:, :MEM_DIM], gk_ref[...], esm_ref[...], ebm_ref[...])
    k_ref[...] = k.T.astype(BF16)
    v_ref[...] = kv[:, MEM_DIM:].astype(BF16)


def _memkv(mem, g, w, gk, consts):
    b = mem.shape[0]
    out_blk = pl.BlockSpec((None, N_MEM, MEM_DIM), lambda i: (i, 0, 0))
    return pl.pallas_call(
        _memkv_kernel,
        out_shape=(jax.ShapeDtypeStruct((b, N_MEM, MEM_DIM), BF16),) * 2,
        grid=(b,),
        in_specs=[
            pl.BlockSpec((None, N_MEM, D_MODEL), lambda i: (i, 0, 0)),
            _const_spec((1, D_MODEL)),
            _const_spec((D_MODEL, 2 * MEM_DIM)),
            _const_spec((1, MEM_DIM)),
            _const_spec((MEM_DIM, LANES)), _const_spec((LANES, MEM_DIM)),
        ],
        out_specs=(out_blk, out_blk),
        compiler_params=_params(1),
        name="memory_kv",
    )(mem, g, w, gk, consts["esm"], consts["ebm"])


def _mixout_kernel(x_ref, a_ref, qm_ref, km_ref, vm_ref, wa_ref, wm_ref, o_ref):
    qm = qm_ref[...]
    km = km_ref[...]
    vm = vm_ref[...]
    zero = jnp.zeros_like(qm)
    mo = jnp.zeros(qm.shape, F32)
    for head in range(MEM_HEADS):
        in_head = _head_lane_mask(qm.shape, head, 1)
        s = jnp.dot(jnp.where(in_head, qm, zero), km, preferred_element_type=F32)
        m = jnp.max(s, axis=-1, keepdims=True)
        e = jnp.exp(s - m)
        l = jnp.sum(e, axis=-1, keepdims=True)
        oh = jnp.dot(e.astype(BF16), vm, preferred_element_type=F32) * (1.0 / l)
        mo = jnp.where(in_head, oh, mo)
    y = jnp.dot(a_ref[...], wa_ref[...], preferred_element_type=F32)
    y = y + jnp.dot(mo.astype(BF16), wm_ref[...], preferred_element_type=F32)
    o_ref[...] = x_ref[...] + y


def _mixout(x3, a3, qm3, km, vm, w_a, w_m):
    b = x3.shape[0]
    tm = TOKEN_TILE
    tok = lambda n: pl.BlockSpec((None, tm, n), lambda i, j: (i, j, 0))
    mem_blk = pl.BlockSpec((None, N_MEM, MEM_DIM), lambda i, j: (i, 0, 0))
    mix = a3.shape[-1]
    return pl.pallas_call(
        _mixout_kernel,
        out_shape=jax.ShapeDtypeStruct(x3.shape, F32),
        grid=(b, SEQ // tm),
        in_specs=[tok(D_MODEL), tok(mix), tok(MEM_DIM), mem_blk, mem_blk,
                  _const_spec((mix, D_MODEL)), _const_spec((MEM_DIM, D_MODEL))],
        out_specs=tok(D_MODEL),
        compiler_params=_params(2),
        name="mixer_out",
    )(x3, a3, qm3, km, vm, w_a, w_m)


def _head_sum_matrices(width):
    head = np.arange(width) // HEAD_DIM
    e_sum = (head[:, None] == np.arange(LANES)[None, :]).astype(np.float32) / HEAD_DIM
    e_bcast = (np.arange(LANES)[:, None] == head[None, :]).astype(np.float32)
    return jnp.asarray(e_sum, BF16), jnp.asarray(e_bcast, BF16)


def _dft_constants():
    d = np.arange(FOURIER_DIM)
    ang = 2.0 * np.pi * ((d[:, None] * d[None, :]) % FOURIER_DIM) / FOURIER_DIM
    feat = np.concatenate([np.cos(ang), np.sin(ang)], axis=1) / np.sqrt(FOURIER_DIM)
    k1 = np.arange(FFT_N1)
    n1 = np.arange(FFT_N1)
    n2 = np.arange(FFT_N2)
    n = FFT_N2 * n1[None, None, :] + n2[:, None, None]
    ang1 = 2.0 * np.pi * ((k1[None, :, None] * n) % SEQ) / SEQ
    c1, s1 = np.cos(ang1) / np.sqrt(FFT_N1), np.sin(ang1) / np.sqrt(FFT_N1)
    seq1 = np.concatenate([np.concatenate([c1, -s1], axis=2), np.concatenate([s1, c1], axis=2)], axis=1)
    k2 = np.arange(FFT_N2)
    ang2 = 2.0 * np.pi * ((k2[:, None] * n2[None, :]) % FFT_N2) / FFT_N2
    seq2 = np.concatenate([np.cos(ang2), -np.sin(ang2)], axis=1) / np.sqrt(FFT_N2)
    return (jnp.asarray(feat, BF16), jnp.asarray(seq1, BF16), jnp.asarray(seq2, BF16))


def _constants():
    es, eb = _head_sum_matrices(NA_DIM)
    esm, ebm = _head_sum_matrices(MEM_DIM)
    feat, seq1, seq2 = _dft_constants()
    return dict(es=es, eb=eb, esm=esm, ebm=ebm, dft_feat=feat, dft_seq1=seq1, dft_seq2=seq2)


def _na_bias_table(rpb):
    shift = np.arange(KERNEL_ROWS)
    off = np.arange(KERNEL_ROWS)
    dr = shift[:, None] + off[None, :]
    c = np.arange(GRID_W)
    dc = np.clip(c[None, :] - c[:, None], -(KERNEL_COLS - 1), KERNEL_COLS - 1) + (KERNEL_COLS - 1)
    cs = np.clip(c - KERNEL_COLS // 2, 0, GRID_W - KERNEL_COLS)
    inside = (c[None, :] >= cs[:, None]) & (c[None, :] < cs[:, None] + KERNEL_COLS)
    t = rpb.astype(F32)[:, dr]
    t = jnp.take(t, jnp.asarray(dc), axis=3)
    t = t.transpose(0, 1, 3, 2, 4)
    t = jnp.where(jnp.asarray(inside)[None, None, :, None, :], t, NEG_INF)
    t = t.reshape(NA_HEADS // NA_HEADS_PER_STEP, NA_HEADS_PER_STEP, KERNEL_ROWS, GRID_W, NA_KEYS)
    t = t.transpose(0, 2, 1, 3, 4)
    return t.reshape(NA_HEADS // NA_HEADS_PER_STEP, KERNEL_ROWS, LANES, NA_KEYS)


def _ffn_weights(w_in, w_out):
    gate = w_in[:, :D_FF].reshape(D_MODEL, N_FF_CHUNKS, FF_CHUNK)
    up = w_in[:, D_FF:].reshape(D_MODEL, N_FF_CHUNKS, FF_CHUNK)
    win_c = jnp.concatenate([gate, up], axis=2).transpose(1, 0, 2).astype(BF16)
    wout_c = w_out.reshape(N_FF_CHUNKS, FF_CHUNK, D_MODEL).astype(BF16)
    return win_c, wout_c


def _row(v):
    return v.reshape(1, -1).astype(F32)


def _tile_heads(g, heads):
    return jnp.tile(g.astype(F32), heads).reshape(1, -1)


def _trunk(x, mem, p, consts):
    b = x.shape[0]
    x2 = x.reshape(b * SEQ, D_MODEL)
    for i in range(DEPTH):
        x2 = _ffn(x2, p["norm_ffn1"][i], *p["ffn1"][i], p["norm_out"][i], False)
        km, vm = _memkv(mem, p["norm_mem"][i], p["w_mem_kv"][i], p["mem_k_norm"][i], consts)
        if i % 2 == 0:
            q, k, v, qm = _mixin_a(x2, p["norm_mix"][i], p["w_in_a"], p["na_q_norm"], p["na_k_norm"],
                                   p["mem_q_norm"][i], consts)
            seq3 = lambda t: t.reshape(b, SEQ, t.shape[-1])
            mixed = _na(seq3(q), seq3(k), seq3(v), p["na_bias"])
            w_out = p["w_out_a"]
        else:
            a, bb, qm = _mixin_b(x2, p["norm_mix"][i], p["w_in_b"], p["mem_q_norm"][i], consts)
            mixed = _seq_dft_real(a.reshape(b, SEQ, FOURIER_DIM), bb.reshape(b, SEQ, FOURIER_DIM), consts)
            w_out = p["w_out_b"]
        x3 = _mixout(x2.reshape(b, SEQ, D_MODEL), mixed, qm.reshape(b, SEQ, MEM_DIM), km, vm,
                     w_out[0], w_out[1])
        x2 = x3.reshape(b * SEQ, D_MODEL)
        x2 = _ffn(x2, p["norm_ffn2"][i], *p["ffn2"][i], p["norm_out"][i], True)
    return x2.reshape(b, SEQ, D_MODEL)


def kernel(x_prompt, x_sample, mem_prompt, mem_sample, norm_ffn1, w_ffn1_in, w_ffn1_out, norm_mix, norm_mem,
           w_mem_kv, mem_q_norm, mem_k_norm, w_in_a, na_q_norm, na_k_norm, na_rpb, w_out_a, w_in_b, w_out_b,
           norm_ffn2, w_ffn2_in, w_ffn2_out, norm_out):
    assert x_prompt.shape[1:] == (SEQ, D_MODEL) and x_sample.shape[1:] == (SEQ, D_MODEL)
    consts = _constants()
    split_out = lambda w: (w[:MEM_DIM * 3].astype(BF16), w[MEM_DIM * 3:].astype(BF16))
    p = dict(
        norm_ffn1=[_row(norm_ffn1[i]) for i in range(DEPTH)],
        norm_ffn2=[_row(norm_ffn2[i]) for i in range(DEPTH)],
        norm_mix=[_row(norm_mix[i]) for i in range(DEPTH)],
        norm_mem=[_row(norm_mem[i]) for i in range(DEPTH)],
        norm_out=[_row(norm_out[i]) for i in range(DEPTH)],
        ffn1=[_ffn_weights(w_ffn1_in[i], w_ffn1_out[i]) for i in range(DEPTH)],
        ffn2=[_ffn_weights(w_ffn2_in[i], w_ffn2_out[i]) for i in range(DEPTH)],
        w_mem_kv=[w_mem_kv[i].astype(BF16) for i in range(DEPTH)],
        mem_q_norm=[_tile_heads(mem_q_norm[i], MEM_HEADS) for i in range(DEPTH)],
        mem_k_norm=[_tile_heads(mem_k_norm[i], MEM_HEADS) for i in range(DEPTH)],
        w_in_a=w_in_a[0].astype(BF16),
        na_q_norm=_tile_heads(na_q_norm[0], NA_HEADS),
        na_k_norm=_tile_heads(na_k_norm[0], NA_HEADS),
        na_bias=_na_bias_table(na_rpb[0]),
        w_out_a=split_out(w_out_a[0]),
        w_in_b=w_in_b[0].astype(BF16),
        w_out_b=split_out(w_out_b[0]),
    )
    return (_trunk(x_prompt, mem_prompt, p, consts), _trunk(x_sample, mem_sample, p, consts))
```

```python
import functools

import numpy as np
import jax
import jax.numpy as jnp
from jax import lax
from jax.experimental import pallas as pl
from jax.experimental.pallas import tpu as pltpu

D_MODEL = 1024
DEPTH = 2
HEAD_DIM = 64
MEM_HEADS = 4
MEM_DIM = MEM_HEADS * HEAD_DIM
N_MEM = 256
NA_HEADS = 12
NA_DIM = NA_HEADS * HEAD_DIM
FOURIER_DIM = 768
GRID_W = 64
KERNEL_ROWS = 8
KERNEL_COLS = 16
D_FF = 2816
EPS = 1e-6
NEG_INF = -1e30
SEQ = 8192
ROWS = SEQ // GRID_W
FFT_N1 = 64
FFT_N2 = 128

F32 = jnp.float32
BF16 = jnp.bfloat16

LANES = 128
MXU_WIDTH = 256
BF16_ROWS = 16
FF_CHUNK = MXU_WIDTH
N_FF_CHUNKS = D_FF // FF_CHUNK
TOKEN_TILE = 512
VMEM_LIMIT = 56 * 1024 * 1024


def _params(n_axes):
    return pltpu.CompilerParams(dimension_semantics=("arbitrary",) * n_axes,
                                vmem_limit_bytes=VMEM_LIMIT)


def _const_spec(shape):
    n = len(shape)
    return pl.BlockSpec(shape, lambda *_: (0,) * n, pipeline_mode=pl.Buffered(1))


def _rms(x, g):
    ms = jnp.mean(x * x, axis=-1, keepdims=True)
    return x * lax.rsqrt(ms + EPS) * g


def _head_rmsnorm(t, gain, e_blk):
    sq = (t * t).astype(BF16)
    slabs = [jnp.dot(sq[:, c:c + MXU_WIDTH], e_blk, preferred_element_type=F32)
             for c in range(0, t.shape[-1], MXU_WIDTH)]
    ms = slabs[0] if len(slabs) == 1 else jnp.concatenate(slabs, axis=-1)
    return t * lax.rsqrt(ms + EPS) * gain


def _head_lane_mask(shape, head, lane_axis):
    lane = lax.broadcasted_iota(jnp.int32, shape, lane_axis)
    return (lane >= head * HEAD_DIM) & (lane < (head + 1) * HEAD_DIM)


def _ffn_kernel(x_ref, g_ref, win_ref, wout_ref, gout_ref, o_ref, xn_ref, acc_ref, *, final_norm):
    xn_ref[...] = _rms(x_ref[...], g_ref[...]).astype(BF16)
    for c in range(N_FF_CHUNKS):
        h = jnp.dot(xn_ref[...], win_ref[c], preferred_element_type=F32)
        gate = h[:, :FF_CHUNK]
        up = h[:, FF_CHUNK:]
        act = (gate * jax.nn.sigmoid(gate) * up).astype(BF16)
        part = jnp.dot(act, wout_ref[c], preferred_element_type=F32)
        if c == 0:
            acc_ref[...] = part
        elif c < N_FF_CHUNKS - 1:
            acc_ref[...] += part
        else:
            y = x_ref[...] + 0.5 * (acc_ref[...] + part)
    if final_norm:
        y = _rms(y, gout_ref[...])
    o_ref[...] = y


def _ffn(x2d, g, win_c, wout_c, gout, final_norm):
    t = x2d.shape[0]
    tm = TOKEN_TILE
    return pl.pallas_call(
        functools.partial(_ffn_kernel, final_norm=final_norm),
        out_shape=jax.ShapeDtypeStruct((t, D_MODEL), F32),
        grid=(t // tm,),
        in_specs=[
            pl.BlockSpec((tm, D_MODEL), lambda i: (i, 0)),
            _const_spec((1, D_MODEL)),
            _const_spec((N_FF_CHUNKS, D_MODEL, 2 * FF_CHUNK)),
            _const_spec((N_FF_CHUNKS, FF_CHUNK, D_MODEL)),
            _const_spec((1, D_MODEL)),
        ],
        out_specs=pl.BlockSpec((tm, D_MODEL), lambda i: (i, 0)),
        scratch_shapes=[pltpu.VMEM((tm, D_MODEL), BF16), pltpu.VMEM((tm, D_MODEL), F32)],
        compiler_params=_params(1),
        name="ffn_final" if final_norm else "ffn",
    )(x2d, g, win_c, wout_c, gout)


def _mixin_a_kernel(x_ref, g_ref, w_ref, gq_ref, gk_ref, gqm_ref, e_ref, q_ref, k_ref, v_ref, qm_ref):
    h = _rms(x_ref[...], g_ref[...]).astype(BF16)
    proj = jnp.dot(h, w_ref[...], preferred_element_type=F32)
    q = proj[:, :NA_DIM]
    k = proj[:, NA_DIM:2 * NA_DIM]
    v = proj[:, 2 * NA_DIM:3 * NA_DIM]
    qm = proj[:, 3 * NA_DIM:]
    scale = HEAD_DIM ** -0.5
    e_blk = e_ref[...]
    q_ref[...] = (_head_rmsnorm(q, gq_ref[...], e_blk) * scale).astype(BF16)
    k_ref[...] = _head_rmsnorm(k, gk_ref[...], e_blk).astype(BF16)
    v_ref[...] = v.astype(BF16)
    qm_ref[...] = (_head_rmsnorm(qm, gqm_ref[...], e_blk) * scale).astype(BF16)


def _mixin_a(x2d, g, w, gq, gk, gqm, consts):
    t = x2d.shape[0]
    tm = TOKEN_TILE
    width = 3 * NA_DIM + MEM_DIM
    tok = lambda n: pl.BlockSpec((tm, n), lambda i: (i, 0))
    return pl.pallas_call(
        _mixin_a_kernel,
        out_shape=(jax.ShapeDtypeStruct((t, NA_DIM), BF16),) * 3 + (jax.ShapeDtypeStruct((t, MEM_DIM), BF16),),
        grid=(t // tm,),
        in_specs=[
            tok(D_MODEL),
            _const_spec((1, D_MODEL)),
            _const_spec((D_MODEL, width)),
            _const_spec((1, NA_DIM)), _const_spec((1, NA_DIM)), _const_spec((1, MEM_DIM)),
            _const_spec((MXU_WIDTH, MXU_WIDTH)),
        ],
        out_specs=(tok(NA_DIM), tok(NA_DIM), tok(NA_DIM), tok(MEM_DIM)),
        compiler_params=_params(1),
        name="mixin_na",
    )(x2d, g, w, gq, gk, gqm, consts["e_blk"])


def _mixin_b_kernel(x_ref, g_ref, w_ref, gqm_ref, e_ref, dft_ref, a_ref, b_ref, qm_ref):
    h = _rms(x_ref[...], g_ref[...]).astype(BF16)
    proj = jnp.dot(h, w_ref[...], preferred_element_type=F32)
    z = proj[:, :FOURIER_DIM].astype(BF16)
    qm = proj[:, FOURIER_DIM:]
    ab = jnp.dot(z, dft_ref[...], preferred_element_type=F32)
    a_ref[...] = ab[:, :FOURIER_DIM].astype(BF16)
    b_ref[...] = ab[:, FOURIER_DIM:].astype(BF16)
    scale = HEAD_DIM ** -0.5
    qm_ref[...] = (_head_rmsnorm(qm, gqm_ref[...], e_ref[...]) * scale).astype(BF16)


def _mixin_b(x2d, g, w, gqm, consts):
    t = x2d.shape[0]
    tm = TOKEN_TILE
    tok = lambda n: pl.BlockSpec((tm, n), lambda i: (i, 0))
    return pl.pallas_call(
        _mixin_b_kernel,
        out_shape=(jax.ShapeDtypeStruct((t, FOURIER_DIM), BF16),) * 2 + (jax.ShapeDtypeStruct((t, MEM_DIM), BF16),),
        grid=(t // tm,),
        in_specs=[
            tok(D_MODEL),
            _const_spec((1, D_MODEL)),
            _const_spec((D_MODEL, FOURIER_DIM + MEM_DIM)),
            _const_spec((1, MEM_DIM)),
            _const_spec((MXU_WIDTH, MXU_WIDTH)),
            _const_spec((FOURIER_DIM, 2 * FOURIER_DIM)),
        ],
        out_specs=(tok(FOURIER_DIM), tok(FOURIER_DIM), tok(MEM_DIM)),
        compiler_params=_params(1),
        name="mixin_fnet",
    )(x2d, g, w, gqm, consts["e_blk"], consts["dft_feat"])


NA_HEADS_PER_STEP = LANES // HEAD_DIM
NA_KEYS = KERNEL_ROWS * GRID_W
NA_TICK_ROWS = 4
NA_TICKS = ROWS // NA_TICK_ROWS


def _na_window(r):
    if isinstance(r, int):
        rs = min(max(r - KERNEL_ROWS // 2, 0), ROWS - KERNEL_ROWS)
        return rs * GRID_W, rs - r + (KERNEL_ROWS - 1)
    rs = jnp.clip(r - KERNEL_ROWS // 2, 0, ROWS - KERNEL_ROWS)
    return pl.multiple_of(rs * GRID_W, GRID_W), rs - r + (KERNEL_ROWS - 1)


def _na_row_start(r):
    return r * GRID_W if isinstance(r, int) else pl.multiple_of(r * GRID_W, GRID_W)


def _na_kernel(q_ref, k_ref, v_ref, bias_ref, o_ref, s_ref, e_ref, l_ref):
    first_head = _head_lane_mask((GRID_W, LANES), 0, 1)

    def scores(t, slot):
        for j in range(NA_TICK_ROWS):
            r = t * NA_TICK_ROWS + j
            key_start, shift = _na_window(r)
            q = q_ref[pl.ds(_na_row_start(r), GRID_W), :]
            zero = jnp.zeros_like(q)
            q2 = jnp.concatenate([jnp.where(first_head, q, zero), jnp.where(first_head, zero, q)], axis=0)
            kb = k_ref[pl.ds(key_start, NA_KEYS), :]
            s = lax.dot_general(q2, kb, (((1,), (1,)), ((), ())), preferred_element_type=F32)
            s_ref[slot, j] = s + bias_ref[shift]

    def softmax(slot):
        for j in range(NA_TICK_ROWS):
            s = s_ref[slot, j]
            m = jnp.max(s, axis=-1, keepdims=True)
            e = jnp.exp(s - m)
            l = jnp.sum(e, axis=-1, keepdims=True)
            e_ref[slot, j] = e.astype(BF16)
            l_ref[slot, j] = jnp.broadcast_to(1.0 / l, (LANES, LANES))

    def values(t, slot):
        for j in range(NA_TICK_ROWS):
            r = t * NA_TICK_ROWS + j
            key_start, _ = _na_window(r)
            vb = v_ref[pl.ds(key_start, NA_KEYS), :]
            o2 = jnp.dot(e_ref[slot, j], vb, preferred_element_type=F32) * l_ref[slot, j]
            o = jnp.where(first_head, o2[:GRID_W], o2[GRID_W:])
            o_ref[pl.ds(_na_row_start(r), GRID_W), :] = o.astype(BF16)

    scores(0, 0)
    scores(1, 1)
    softmax(0)

    def body(u, carry):
        t = 2 * u
        scores(t, 0)
        values(t - 2, 0)
        softmax(1)
        scores(t + 1, 1)
        values(t - 1, 1)
        softmax(0)
        return carry

    lax.fori_loop(1, NA_TICKS // 2, body, 0)
    values(NA_TICKS - 2, 0)
    softmax(1)
    values(NA_TICKS - 1, 1)


def _na(q3, k3, v3, bias):
    b = q3.shape[0]
    n_pairs = NA_HEADS // NA_HEADS_PER_STEP
    seq_spec = pl.BlockSpec((None, SEQ, LANES), lambda i, j: (i, 0, j))
    return pl.pallas_call(
        _na_kernel,
        out_shape=jax.ShapeDtypeStruct((b, SEQ, NA_DIM), BF16),
        grid=(b, n_pairs),
        in_specs=[seq_spec, seq_spec, seq_spec,
                  pl.BlockSpec((None, KERNEL_ROWS, LANES, NA_KEYS), lambda i, j: (j, 0, 0, 0))],
        out_specs=seq_spec,
        scratch_shapes=[pltpu.VMEM((2, NA_TICK_ROWS, LANES, NA_KEYS), F32),
                        pltpu.VMEM((2, NA_TICK_ROWS, LANES, NA_KEYS), BF16),
                        pltpu.VMEM((2, NA_TICK_ROWS, LANES, LANES), F32)],
        compiler_params=_params(2),
        name="na_attention",
    )(q3, k3, v3, bias)


FFT_G = BF16_ROWS


def _swap_leading(x):
    return pltpu.einshape("abc->bac", x)


def _fft_stage1_kernel(a_ref, b_ref, g_ref, yr_ref, yi_ref, sr_ref, si_ref):
    a_t = _swap_leading(a_ref[...])
    b_t = _swap_leading(b_ref[...])
    for j in range(FFT_G):
        x = jnp.concatenate([a_t[j], b_t[j]], axis=0)
        y = jnp.dot(g_ref[j], x, preferred_element_type=F32)
        sr_ref[j] = y[:FFT_N1].astype(BF16)
        si_ref[j] = y[FFT_N1:].astype(BF16)
    yr_ref[...] = _swap_leading(sr_ref[...])
    yi_ref[...] = _swap_leading(si_ref[...])


def _fft_stage2_kernel(yr_ref, yi_ref, h_ref, f_ref, s_ref):
    for j in range(FFT_G):
        x = jnp.concatenate([yr_ref[j], yi_ref[j]], axis=0)
        s_ref[j] = jnp.dot(h_ref[...], x, preferred_element_type=F32).astype(BF16)
    f_ref[...] = _swap_leading(s_ref[...])


def _seq_dft_real(a3, b3, consts):
    b = a3.shape[0]
    n2_groups = FFT_N2 // FFT_G
    k1_groups = FFT_N1 // FFT_G
    split1 = (b, FFT_N1, n2_groups, FFT_G, FOURIER_DIM)
    blk1 = pl.BlockSpec((None, FFT_N1, None, FFT_G, FOURIER_DIM), lambda i, j: (i, 0, j, 0, 0))
    stage = pltpu.VMEM((FFT_G, FFT_N1, FOURIER_DIM), BF16)
    yr, yi = pl.pallas_call(
        _fft_stage1_kernel,
        out_shape=(jax.ShapeDtypeStruct(split1, BF16),) * 2,
        grid=(b, n2_groups),
        in_specs=[blk1, blk1,
                  pl.BlockSpec((FFT_G, 2 * FFT_N1, 2 * FFT_N1), lambda i, j: (j, 0, 0))],
        out_specs=(blk1, blk1),
        scratch_shapes=[stage, stage],
        compiler_params=_params(2),
        name="fft_stage1",
    )(a3.reshape(split1), b3.reshape(split1), consts["dft_seq1"])
    yr = yr.reshape(b, FFT_N1, FFT_N2, FOURIER_DIM)
    yi = yi.reshape(b, FFT_N1, FFT_N2, FOURIER_DIM)
    blk2 = pl.BlockSpec((None, FFT_G, FFT_N2, FOURIER_DIM), lambda i, j: (i, j, 0, 0))
    f = pl.pallas_call(
        _fft_stage2_kernel,
        out_shape=jax.ShapeDtypeStruct((b, FFT_N2, k1_groups, FFT_G, FOURIER_DIM), BF16),
        grid=(b, k1_groups),
        in_specs=[blk2, blk2, _const_spec((FFT_N2, 2 * FFT_N2))],
        out_specs=pl.BlockSpec((None, FFT_N2, None, FFT_G, FOURIER_DIM), lambda i, j: (i, 0, j, 0, 0)),
        scratch_shapes=[pltpu.VMEM((FFT_G, FFT_N2, FOURIER_DIM), BF16)],
        compiler_params=_params(2),
        name="fft_stage2",
    )(yr, yi, consts["dft_seq2"])
    return f.reshape(b, SEQ, FOURIER_DIM)


def _memkv_kernel(mem_ref, g_ref, w_ref, gk_ref, e_ref, k_ref, v_ref):
    h = _rms(mem_ref[...], g_ref[...]).astype(BF16)
    kv = jnp.dot(h, w_ref[...], preferred_element_type=F32)
    k = _head_rmsnorm(kv[:, :MEM_DIM], gk_ref[...], e_ref[...])
    k_ref[...] = k.T.astype(BF16)
    v_ref[...] = kv[:, MEM_DIM:].astype(BF16)


def _memkv(mem, g, w, gk, consts):
    b = mem.shape[0]
    out_blk = pl.BlockSpec((None, N_MEM, MEM_DIM), lambda i: (i, 0, 0))
    return pl.pallas_call(
        _memkv_kernel,
        out_shape=(jax.ShapeDtypeStruct((b, N_MEM, MEM_DIM), BF16),) * 2,
        grid=(b,),
        in_specs=[
            pl.BlockSpec((None, N_MEM, D_MODEL), lambda i: (i, 0, 0)),
            _const_spec((1, D_MODEL)),
            _const_spec((D_MODEL, 2 * MEM_DIM)),
            _const_spec((1, MEM_DIM)),
            _const_spec((MXU_WIDTH, MXU_WIDTH)),
        ],
        out_specs=(out_blk, out_blk),
        compiler_params=_params(1),
        name="memory_kv",
    )(mem, g, w, gk, consts["e_blk"])


def _mixout_kernel(x_ref, a_ref, qm_ref, km_ref, vm_ref, wa_ref, wm_ref, o_ref):
    qm = qm_ref[...]
    km = km_ref[...]
    vm = vm_ref[...]
    zero = jnp.zeros_like(qm)
    mo = jnp.zeros(qm.shape, F32)
    for head in range(MEM_HEADS):
        in_head = _head_lane_mask(qm.shape, head, 1)
        s = jnp.dot(jnp.where(in_head, qm, zero), km, preferred_element_type=F32)
        m = jnp.max(s, axis=-1, keepdims=True)
        e = jnp.exp(s - m)
        l = jnp.sum(e, axis=-1, keepdims=True)
        oh = jnp.dot(e.astype(BF16), vm, preferred_element_type=F32) * (1.0 / l)
        mo = jnp.where(in_head, oh, mo)
    y = jnp.dot(a_ref[...], wa_ref[...], preferred_element_type=F32)
    y = y + jnp.dot(mo.astype(BF16), wm_ref[...], preferred_element_type=F32)
    o_ref[...] = x_ref[...] + y


def _mixout(x3, a3, qm3, km, vm, w_a, w_m):
    b = x3.shape[0]
    tm = TOKEN_TILE
    tok = lambda n: pl.BlockSpec((None, tm, n), lambda i, j: (i, j, 0))
    mem_blk = pl.BlockSpec((None, N_MEM, MEM_DIM), lambda i, j: (i, 0, 0))
    mix = a3.shape[-1]
    return pl.pallas_call(
        _mixout_kernel,
        out_shape=jax.ShapeDtypeStruct(x3.shape, F32),
        grid=(b, SEQ // tm),
        in_specs=[tok(D_MODEL), tok(mix), tok(MEM_DIM), mem_blk, mem_blk,
                  _const_spec((mix, D_MODEL)), _const_spec((MEM_DIM, D_MODEL))],
        out_specs=tok(D_MODEL),
        compiler_params=_params(2),
        name="mixer_out",
    )(x3, a3, qm3, km, vm, w_a, w_m)


def _head_mean_matrix():
    head = np.arange(MXU_WIDTH) // HEAD_DIM
    return jnp.asarray((head[:, None] == head[None, :]).astype(np.float32) / HEAD_DIM, BF16)


def _dft_constants():
    d = np.arange(FOURIER_DIM)
    ang = 2.0 * np.pi * ((d[:, None] * d[None, :]) % FOURIER_DIM) / FOURIER_DIM
    feat = np.concatenate([np.cos(ang), np.sin(ang)], axis=1) / np.sqrt(FOURIER_DIM)
    k1 = np.arange(FFT_N1)
    n1 = np.arange(FFT_N1)
    n2 = np.arange(FFT_N2)
    n = FFT_N2 * n1[None, None, :] + n2[:, None, None]
    ang1 = 2.0 * np.pi * ((k1[None, :, None] * n) % SEQ) / SEQ
    c1, s1 = np.cos(ang1) / np.sqrt(FFT_N1), np.sin(ang1) / np.sqrt(FFT_N1)
    seq1 = np.concatenate([np.concatenate([c1, -s1], axis=2), np.concatenate([s1, c1], axis=2)], axis=1)
    k2 = np.arange(FFT_N2)
    ang2 = 2.0 * np.pi * ((k2[:, None] * n2[None, :]) % FFT_N2) / FFT_N2
    seq2 = np.concatenate([np.cos(ang2), -np.sin(ang2)], axis=1) / np.sqrt(FFT_N2)
    return (jnp.asarray(feat, BF16), jnp.asarray(seq1, BF16), jnp.asarray(seq2, BF16))


def _constants():
    feat, seq1, seq2 = _dft_constants()
    return dict(e_blk=_head_mean_matrix(), dft_feat=feat, dft_seq1=seq1, dft_seq2=seq2)


def _na_bias_table(rpb):
    shift = np.arange(KERNEL_ROWS)
    off = np.arange(KERNEL_ROWS)
    dr = shift[:, None] + off[None, :]
    c = np.arange(GRID_W)
    dc = np.clip(c[None, :] - c[:, None], -(KERNEL_COLS - 1), KERNEL_COLS - 1) + (KERNEL_COLS - 1)
    cs = np.clip(c - KERNEL_COLS // 2, 0, GRID_W - KERNEL_COLS)
    inside = (c[None, :] >= cs[:, None]) & (c[None, :] < cs[:, None] + KERNEL_COLS)
    t = rpb.astype(F32)[:, dr]
    t = jnp.take(t, jnp.asarray(dc), axis=3)
    t = t.transpose(0, 1, 3, 2, 4)
    t = jnp.where(jnp.asarray(inside)[None, None, :, None, :], t, NEG_INF)
    t = t.reshape(NA_HEADS // NA_HEADS_PER_STEP, NA_HEADS_PER_STEP, KERNEL_ROWS, GRID_W, NA_KEYS)
    t = t.transpose(0, 2, 1, 3, 4)
    return t.reshape(NA_HEADS // NA_HEADS_PER_STEP, KERNEL_ROWS, LANES, NA_KEYS)


def _ffn_weights(w_in, w_out):
    gate = w_in[:, :D_FF].reshape(D_MODEL, N_FF_CHUNKS, FF_CHUNK)
    up = w_in[:, D_FF:].reshape(D_MODEL, N_FF_CHUNKS, FF_CHUNK)
    win_c = jnp.concatenate([gate, up], axis=2).transpose(1, 0, 2).astype(BF16)
    wout_c = w_out.reshape(N_FF_CHUNKS, FF_CHUNK, D_MODEL).astype(BF16)
    return win_c, wout_c


def _row(v):
    return v.reshape(1, -1).astype(F32)


def _tile_heads(g, heads):
    return jnp.tile(g.astype(F32), heads).reshape(1, -1)


def _trunk(x, mem, p, consts):
    b = x.shape[0]
    x2 = x.reshape(b * SEQ, D_MODEL)
    for i in range(DEPTH):
        x2 = _ffn(x2, p["norm_ffn1"][i], *p["ffn1"][i], p["norm_out"][i], False)
        km, vm = _memkv(mem, p["norm_mem"][i], p["w_mem_kv"][i], p["mem_k_norm"][i], consts)
        if i % 2 == 0:
            q, k, v, qm = _mixin_a(x2, p["norm_mix"][i], p["w_in_a"], p["na_q_norm"], p["na_k_norm"],
                                   p["mem_q_norm"][i], consts)
            seq3 = lambda t: t.reshape(b, SEQ, t.shape[-1])
            mixed = _na(seq3(q), seq3(k), seq3(v), p["na_bias"])
            w_out = p["w_out_a"]
        else:
            a, bb, qm = _mixin_b(x2, p["norm_mix"][i], p["w_in_b"], p["mem_q_norm"][i], consts)
            mixed = _seq_dft_real(a.reshape(b, SEQ, FOURIER_DIM), bb.reshape(b, SEQ, FOURIER_DIM), consts)
            w_out = p["w_out_b"]
        x3 = _mixout(x2.reshape(b, SEQ, D_MODEL), mixed, qm.reshape(b, SEQ, MEM_DIM), km, vm,
                     w_out[0], w_out[1])
        x2 = x3.reshape(b * SEQ, D_MODEL)
        x2 = _ffn(x2, p["norm_ffn2"][i], *p["ffn2"][i], p["norm_out"][i], True)
    return x2.reshape(b, SEQ, D_MODEL)


def kernel(x_prompt, x_sample, mem_prompt, mem_sample, norm_ffn1, w_ffn1_in, w_ffn1_out, norm_mix, norm_mem,
           w_mem_kv, mem_q_norm, mem_k_norm, w_in_a, na_q_norm, na_k_norm, na_rpb, w_out_a, w_in_b, w_out_b,
           norm_ffn2, w_ffn2_in, w_ffn2_out, norm_out):
    assert x_prompt.shape[1:] == (SEQ, D_MODEL) and x_sample.shape[1:] == (SEQ, D_MODEL)
    consts = _constants()
    split_out = lambda w: (w[:MEM_DIM * 3].astype(BF16), w[MEM_DIM * 3:].astype(BF16))
    p = dict(
        norm_ffn1=[_row(norm_ffn1[i]) for i in range(DEPTH)],
        norm_ffn2=[_row(norm_ffn2[i]) for i in range(DEPTH)],
        norm_mix=[_row(norm_mix[i]) for i in range(DEPTH)],
        norm_mem=[_row(norm_mem[i]) for i in range(DEPTH)],
        norm_out=[_row(norm_out[i]) for i in range(DEPTH)],
        ffn1=[_ffn_weights(w_ffn1_in[i], w_ffn1_out[i]) for i in range(DEPTH)],
        ffn2=[_ffn_weights(w_ffn2_in[i], w_ffn2_out[i]) for i in range(DEPTH)],
        w_mem_kv=[w_mem_kv[i].astype(BF16) for i in range(DEPTH)],
        mem_q_norm=[_tile_heads(mem_q_norm[i], MEM_HEADS) for i in range(DEPTH)],
        mem_k_norm=[_tile_heads(mem_k_norm[i], MEM_HEADS) for i in range(DEPTH)],
        w_in_a=w_in_a[0].astype(BF16),
        na_q_norm=_tile_heads(na_q_norm[0], NA_HEADS),
        na_k_norm=_tile_heads(na_k_norm[0], NA_HEADS),
        na_bias=_na_bias_table(na_rpb[0]),
        w_out_a=split_out(w_out_a[0]),
        w_in_b=w_in_b[0].astype(BF16),
        w_out_b=split_out(w_out_b[0]),
    )
    return (_trunk(x_prompt, mem_prompt, p, consts), _trunk(x_sample, mem_sample, p, consts))
```

```python
import functools

import numpy as np
import jax
import jax.numpy as jnp
from jax import lax
from jax.experimental import pallas as pl
from jax.experimental.pallas import tpu as pltpu

D_MODEL = 1024
DEPTH = 2
HEAD_DIM = 64
MEM_HEADS = 4
MEM_DIM = MEM_HEADS * HEAD_DIM
N_MEM = 256
NA_HEADS = 12
NA_DIM = NA_HEADS * HEAD_DIM
FOURIER_DIM = 768
GRID_W = 64
KERNEL_ROWS = 8
KERNEL_COLS = 16
D_FF = 2816
EPS = 1e-6
NEG_INF = -1e30
SEQ = 8192
ROWS = SEQ // GRID_W
FFT_N1 = 64
FFT_N2 = 128

F32 = jnp.float32
BF16 = jnp.bfloat16

LANES = 128
MXU_WIDTH = 256
BF16_ROWS = 16
FF_CHUNK = MXU_WIDTH
N_FF_CHUNKS = D_FF // FF_CHUNK
SUBTILE_ROWS = 512
SUBTILES = 2
TOKEN_TILE = SUBTILES * SUBTILE_ROWS
VMEM_LIMIT = 56 * 1024 * 1024


def _params(n_axes):
    return pltpu.CompilerParams(dimension_semantics=("arbitrary",) * n_axes,
                                vmem_limit_bytes=VMEM_LIMIT)


def _const_spec(shape):
    n = len(shape)
    return pl.BlockSpec(shape, lambda *_: (0,) * n, pipeline_mode=pl.Buffered(1))


def _rms(x, g):
    ms = jnp.mean(x * x, axis=-1, keepdims=True)
    return x * lax.rsqrt(ms + EPS) * g


def _head_rmsnorm(t, gain, e_blk):
    sq = (t * t).astype(BF16)
    slabs = [jnp.dot(sq[:, c:c + MXU_WIDTH], e_blk, preferred_element_type=F32)
             for c in range(0, t.shape[-1], MXU_WIDTH)]
    ms = slabs[0] if len(slabs) == 1 else jnp.concatenate(slabs, axis=-1)
    return t * lax.rsqrt(ms + EPS) * gain


def _head_lane_mask(shape, head, lane_axis):
    lane = lax.broadcasted_iota(jnp.int32, shape, lane_axis)
    return (lane >= head * HEAD_DIM) & (lane < (head + 1) * HEAD_DIM)


def _ffn_kernel(x_ref, g_ref, win_ref, wout_ref, gout_ref, o_ref, xn_ref, acc_ref, *, final_norm):
    for s in range(SUBTILES):
        rows = slice(s * SUBTILE_ROWS, (s + 1) * SUBTILE_ROWS)
        xn_ref[rows] = _rms(x_ref[rows], g_ref[...]).astype(BF16)
        for c in range(N_FF_CHUNKS):
            h = jnp.dot(xn_ref[rows], win_ref[c], preferred_element_type=F32)
            gate = h[:, :FF_CHUNK]
            up = h[:, FF_CHUNK:]
            act = (gate * jax.nn.sigmoid(gate) * up).astype(BF16)
            part = jnp.dot(act, wout_ref[c], preferred_element_type=F32)
            if c == 0:
                acc_ref[rows] = part
            elif c < N_FF_CHUNKS - 1:
                acc_ref[rows] += part
            else:
                y = x_ref[rows] + 0.5 * (acc_ref[rows] + part)
        if final_norm:
            y = _rms(y, gout_ref[...])
        o_ref[rows] = y


def _ffn(x2d, g, win_c, wout_c, gout, final_norm):
    t = x2d.shape[0]
    tm = TOKEN_TILE
    return pl.pallas_call(
        functools.partial(_ffn_kernel, final_norm=final_norm),
        out_shape=jax.ShapeDtypeStruct((t, D_MODEL), F32),
        grid=(t // tm,),
        in_specs=[
            pl.BlockSpec((tm, D_MODEL), lambda i: (i, 0)),
            _const_spec((1, D_MODEL)),
            _const_spec((N_FF_CHUNKS, D_MODEL, 2 * FF_CHUNK)),
            _const_spec((N_FF_CHUNKS, FF_CHUNK, D_MODEL)),
            _const_spec((1, D_MODEL)),
        ],
        out_specs=pl.BlockSpec((tm, D_MODEL), lambda i: (i, 0)),
        scratch_shapes=[pltpu.VMEM((tm, D_MODEL), BF16), pltpu.VMEM((tm, D_MODEL), F32)],
        compiler_params=_params(1),
        name="ffn_final" if final_norm else "ffn",
    )(x2d, g, win_c, wout_c, gout)


def _mixin_a_kernel(x_ref, g_ref, w_ref, gq_ref, gk_ref, gqm_ref, e_ref, q_ref, k_ref, v_ref, qm_ref):
    scale = HEAD_DIM ** -0.5
    e_blk = e_ref[...]
    for s in range(SUBTILES):
        rows = slice(s * SUBTILE_ROWS, (s + 1) * SUBTILE_ROWS)
        h = _rms(x_ref[rows], g_ref[...]).astype(BF16)
        proj = jnp.dot(h, w_ref[...], preferred_element_type=F32)
        q = proj[:, :NA_DIM]
        k = proj[:, NA_DIM:2 * NA_DIM]
        v = proj[:, 2 * NA_DIM:3 * NA_DIM]
        qm = proj[:, 3 * NA_DIM:]
        q_ref[rows] = (_head_rmsnorm(q, gq_ref[...], e_blk) * scale).astype(BF16)
        k_ref[rows] = _head_rmsnorm(k, gk_ref[...], e_blk).astype(BF16)
        v_ref[rows] = v.astype(BF16)
        qm_ref[rows] = (_head_rmsnorm(qm, gqm_ref[...], e_blk) * scale).astype(BF16)


def _mixin_a(x2d, g, w, gq, gk, gqm, consts):
    t = x2d.shape[0]
    tm = TOKEN_TILE
    width = 3 * NA_DIM + MEM_DIM
    tok = lambda n: pl.BlockSpec((tm, n), lambda i: (i, 0))
    return pl.pallas_call(
        _mixin_a_kernel,
        out_shape=(jax.ShapeDtypeStruct((t, NA_DIM), BF16),) * 3 + (jax.ShapeDtypeStruct((t, MEM_DIM), BF16),),
        grid=(t // tm,),
        in_specs=[
            tok(D_MODEL),
            _const_spec((1, D_MODEL)),
            _const_spec((D_MODEL, width)),
            _const_spec((1, NA_DIM)), _const_spec((1, NA_DIM)), _const_spec((1, MEM_DIM)),
            _const_spec((MXU_WIDTH, MXU_WIDTH)),
        ],
        out_specs=(tok(NA_DIM), tok(NA_DIM), tok(NA_DIM), tok(MEM_DIM)),
        compiler_params=_params(1),
        name="mixin_na",
    )(x2d, g, w, gq, gk, gqm, consts["e_blk"])


def _mixin_b_kernel(x_ref, g_ref, w_ref, gqm_ref, e_ref, dft_ref, a_ref, b_ref, qm_ref):
    scale = HEAD_DIM ** -0.5
    for s in range(SUBTILES):
        rows = slice(s * SUBTILE_ROWS, (s + 1) * SUBTILE_ROWS)
        h = _rms(x_ref[rows], g_ref[...]).astype(BF16)
        proj = jnp.dot(h, w_ref[...], preferred_element_type=F32)
        z = proj[:, :FOURIER_DIM].astype(BF16)
        qm = proj[:, FOURIER_DIM:]
        ab = jnp.dot(z, dft_ref[...], preferred_element_type=F32)
        a_ref[rows] = ab[:, :FOURIER_DIM].astype(BF16)
        b_ref[rows] = ab[:, FOURIER_DIM:].astype(BF16)
        qm_ref[rows] = (_head_rmsnorm(qm, gqm_ref[...], e_ref[...]) * scale).astype(BF16)


def _mixin_b(x2d, g, w, gqm, consts):
    t = x2d.shape[0]
    tm = TOKEN_TILE
    tok = lambda n: pl.BlockSpec((tm, n), lambda i: (i, 0))
    return pl.pallas_call(
        _mixin_b_kernel,
        out_shape=(jax.ShapeDtypeStruct((t, FOURIER_DIM), BF16),) * 2 + (jax.ShapeDtypeStruct((t, MEM_DIM), BF16),),
        grid=(t // tm,),
        in_specs=[
            tok(D_MODEL),
            _const_spec((1, D_MODEL)),
            _const_spec((D_MODEL, FOURIER_DIM + MEM_DIM)),
            _const_spec((1, MEM_DIM)),
            _const_spec((MXU_WIDTH, MXU_WIDTH)),
            _const_spec((FOURIER_DIM, 2 * FOURIER_DIM)),
        ],
        out_specs=(tok(FOURIER_DIM), tok(FOURIER_DIM), tok(MEM_DIM)),
        compiler_params=_params(1),
        name="mixin_fnet",
    )(x2d, g, w, gqm, consts["e_blk"], consts["dft_feat"])


NA_HEADS_PER_STEP = LANES // HEAD_DIM
NA_KEYS = KERNEL_ROWS * GRID_W
NA_TICK_ROWS = 4
NA_TICKS = ROWS // NA_TICK_ROWS


def _na_window(r):
    if isinstance(r, int):
        rs = min(max(r - KERNEL_ROWS // 2, 0), ROWS - KERNEL_ROWS)
        return rs * GRID_W, rs - r + (KERNEL_ROWS - 1)
    rs = jnp.clip(r - KERNEL_ROWS // 2, 0, ROWS - KERNEL_ROWS)
    return pl.multiple_of(rs * GRID_W, GRID_W), rs - r + (KERNEL_ROWS - 1)


def _na_row_start(r):
    return r * GRID_W if isinstance(r, int) else pl.multiple_of(r * GRID_W, GRID_W)


def _na_kernel(q_ref, k_ref, v_ref, bias_ref, o_ref, s_ref, e_ref, l_ref):
    first_head = _head_lane_mask((GRID_W, LANES), 0, 1)

    def scores(t, slot):
        for j in range(NA_TICK_ROWS):
            r = t * NA_TICK_ROWS + j
            key_start, shift = _na_window(r)
            q = q_ref[pl.ds(_na_row_start(r), GRID_W), :]
            zero = jnp.zeros_like(q)
            q2 = jnp.concatenate([jnp.where(first_head, q, zero), jnp.where(first_head, zero, q)], axis=0)
            kb = k_ref[pl.ds(key_start, NA_KEYS), :]
            s = lax.dot_general(q2, kb, (((1,), (1,)), ((), ())), preferred_element_type=F32)
            s_ref[slot, j] = s + bias_ref[shift]

    def softmax(slot):
        for j in range(NA_TICK_ROWS):
            s = s_ref[slot, j]
            m = jnp.max(s, axis=-1, keepdims=True)
            e = jnp.exp(s - m)
            l = jnp.sum(e, axis=-1, keepdims=True)
            e_ref[slot, j] = e.astype(BF16)
            l_ref[slot, j] = jnp.broadcast_to(1.0 / l, (LANES, LANES))

    def values(t, slot):
        for j in range(NA_TICK_ROWS):
            r = t * NA_TICK_ROWS + j
            key_start, _ = _na_window(r)
            vb = v_ref[pl.ds(key_start, NA_KEYS), :]
            o2 = jnp.dot(e_ref[slot, j], vb, preferred_element_type=F32) * l_ref[slot, j]
            o = jnp.where(first_head, o2[:GRID_W], o2[GRID_W:])
            o_ref[pl.ds(_na_row_start(r), GRID_W), :] = o.astype(BF16)

    scores(0, 0)
    scores(1, 1)
    softmax(0)

    def body(u, carry):
        t = 2 * u
        scores(t, 0)
        values(t - 2, 0)
        softmax(1)
        scores(t + 1, 1)
        values(t - 1, 1)
        softmax(0)
        return carry

    lax.fori_loop(1, NA_TICKS // 2, body, 0)
    values(NA_TICKS - 2, 0)
    softmax(1)
    values(NA_TICKS - 1, 1)


def _na(q3, k3, v3, bias):
    b = q3.shape[0]
    n_pairs = NA_HEADS // NA_HEADS_PER_STEP
    seq_spec = pl.BlockSpec((None, SEQ, LANES), lambda i, j: (i, 0, j))
    return pl.pallas_call(
        _na_kernel,
        out_shape=jax.ShapeDtypeStruct((b, SEQ, NA_DIM), BF16),
        grid=(b, n_pairs),
        in_specs=[seq_spec, seq_spec, seq_spec,
                  pl.BlockSpec((None, KERNEL_ROWS, LANES, NA_KEYS), lambda i, j: (j, 0, 0, 0))],
        out_specs=seq_spec,
        scratch_shapes=[pltpu.VMEM((2, NA_TICK_ROWS, LANES, NA_KEYS), F32),
                        pltpu.VMEM((2, NA_TICK_ROWS, LANES, NA_KEYS), BF16),
                        pltpu.VMEM((2, NA_TICK_ROWS, LANES, LANES), F32)],
        compiler_params=_params(2),
        name="na_attention",
    )(q3, k3, v3, bias)


FFT_G = BF16_ROWS


def _swap_leading(x):
    return pltpu.einshape("abc->bac", x)


def _fft_stage1_kernel(a_ref, b_ref, g_ref, yr_ref, yi_ref, sr_ref, si_ref):
    a_t = _swap_leading(a_ref[...])
    b_t = _swap_leading(b_ref[...])
    for j in range(FFT_G):
        x = jnp.concatenate([a_t[j], b_t[j]], axis=0)
        y = jnp.dot(g_ref[j], x, preferred_element_type=F32)
        sr_ref[j] = y[:FFT_N1].astype(BF16)
        si_ref[j] = y[FFT_N1:].astype(BF16)
    yr_ref[...] = _swap_leading(sr_ref[...])
    yi_ref[...] = _swap_leading(si_ref[...])


def _fft_stage2_kernel(yr_ref, yi_ref, h_ref, f_ref, s_ref):
    for j in range(FFT_G):
        x = jnp.concatenate([yr_ref[j], yi_ref[j]], axis=0)
        s_ref[j] = jnp.dot(h_ref[...], x, preferred_element_type=F32).astype(BF16)
    f_ref[...] = _swap_leading(s_ref[...])


def _seq_dft_real(a3, b3, consts):
    b = a3.shape[0]
    n2_groups = FFT_N2 // FFT_G
    k1_groups = FFT_N1 // FFT_G
    split1 = (b, FFT_N1, n2_groups, FFT_G, FOURIER_DIM)
    blk1 = pl.BlockSpec((None, FFT_N1, None, FFT_G, FOURIER_DIM), lambda i, j: (i, 0, j, 0, 0))
    stage = pltpu.VMEM((FFT_G, FFT_N1, FOURIER_DIM), BF16)
    yr, yi = pl.pallas_call(
        _fft_stage1_kernel,
        out_shape=(jax.ShapeDtypeStruct(split1, BF16),) * 2,
        grid=(b, n2_groups),
        in_specs=[blk1, blk1,
                  pl.BlockSpec((FFT_G, 2 * FFT_N1, 2 * FFT_N1), lambda i, j: (j, 0, 0))],
        out_specs=(blk1, blk1),
        scratch_shapes=[stage, stage],
        compiler_params=_params(2),
        name="fft_stage1",
    )(a3.reshape(split1), b3.reshape(split1), consts["dft_seq1"])
    yr = yr.reshape(b, FFT_N1, FFT_N2, FOURIER_DIM)
    yi = yi.reshape(b, FFT_N1, FFT_N2, FOURIER_DIM)
    blk2 = pl.BlockSpec((None, FFT_G, FFT_N2, FOURIER_DIM), lambda i, j: (i, j, 0, 0))
    f = pl.pallas_call(
        _fft_stage2_kernel,
        out_shape=jax.ShapeDtypeStruct((b, FFT_N2, k1_groups, FFT_G, FOURIER_DIM), BF16),
        grid=(b, k1_groups),
        in_specs=[blk2, blk2, _const_spec((FFT_N2, 2 * FFT_N2))],
        out_specs=pl.BlockSpec((None, FFT_N2, None, FFT_G, FOURIER_DIM), lambda i, j: (i, 0, j, 0, 0)),
        scratch_shapes=[pltpu.VMEM((FFT_G, FFT_N2, FOURIER_DIM), BF16)],
        compiler_params=_params(2),
        name="fft_stage2",
    )(yr, yi, consts["dft_seq2"])
    return f.reshape(b, SEQ, FOURIER_DIM)


def _memkv_kernel(mem_ref, g_ref, w_ref, gk_ref, e_ref, k_ref, v_ref):
    h = _rms(mem_ref[...], g_ref[...]).astype(BF16)
    kv = jnp.dot(h, w_ref[...], preferred_element_type=F32)
    k = _head_rmsnorm(kv[:, :MEM_DIM], gk_ref[...], e_ref[...])
    k_ref[...] = k.T.astype(BF16)
    v_ref[...] = kv[:, MEM_DIM:].astype(BF16)


def _memkv(mem, g, w, gk, consts):
    b = mem.shape[0]
    out_blk = pl.BlockSpec((None, N_MEM, MEM_DIM), lambda i: (i, 0, 0))
    return pl.pallas_call(
        _memkv_kernel,
        out_shape=(jax.ShapeDtypeStruct((b, N_MEM, MEM_DIM), BF16),) * 2,
        grid=(b,),
        in_specs=[
            pl.BlockSpec((None, N_MEM, D_MODEL), lambda i: (i, 0, 0)),
            _const_spec((1, D_MODEL)),
            _const_spec((D_MODEL, 2 * MEM_DIM)),
            _const_spec((1, MEM_DIM)),
            _const_spec((MXU_WIDTH, MXU_WIDTH)),
        ],
        out_specs=(out_blk, out_blk),
        compiler_params=_params(1),
        name="memory_kv",
    )(mem, g, w, gk, consts["e_blk"])


def _mixout_kernel(x_ref, a_ref, qm_ref, km_ref, vm_ref, wa_ref, wm_ref, o_ref):
    km = km_ref[...]
    vm = vm_ref[...]
    for s in range(SUBTILES):
        rows = slice(s * SUBTILE_ROWS, (s + 1) * SUBTILE_ROWS)
        qm = qm_ref[rows]
        zero = jnp.zeros_like(qm)
        mo = jnp.zeros(qm.shape, F32)
        for head in range(MEM_HEADS):
            in_head = _head_lane_mask(qm.shape, head, 1)
            sc = jnp.dot(jnp.where(in_head, qm, zero), km, preferred_element_type=F32)
            e = jnp.exp(sc - jnp.max(sc, axis=-1, keepdims=True))
            l = jnp.sum(e, axis=-1, keepdims=True)
            oh = jnp.dot(e.astype(BF16), vm, preferred_element_type=F32) * (1.0 / l)
            mo = jnp.where(in_head, oh, mo)
        y = jnp.dot(a_ref[rows], wa_ref[...], preferred_element_type=F32)
        y = y + jnp.dot(mo.astype(BF16), wm_ref[...], preferred_element_type=F32)
        o_ref[rows] = x_ref[rows] + y


def _mixout(x3, a3, qm3, km, vm, w_a, w_m):
    b = x3.shape[0]
    tm = TOKEN_TILE
    tok = lambda n: pl.BlockSpec((None, tm, n), lambda i, j: (i, j, 0))
    mem_blk = pl.BlockSpec((None, N_MEM, MEM_DIM), lambda i, j: (i, 0, 0))
    mix = a3.shape[-1]
    return pl.pallas_call(
        _mixout_kernel,
        out_shape=jax.ShapeDtypeStruct(x3.shape, F32),
        grid=(b, SEQ // tm),
        in_specs=[tok(D_MODEL), tok(mix), tok(MEM_DIM), mem_blk, mem_blk,
                  _const_spec((mix, D_MODEL)), _const_spec((MEM_DIM, D_MODEL))],
        out_specs=tok(D_MODEL),
        compiler_params=_params(2),
        name="mixer_out",
    )(x3, a3, qm3, km, vm, w_a, w_m)


def _head_mean_matrix():
    head = np.arange(MXU_WIDTH) // HEAD_DIM
    return jnp.asarray((head[:, None] == head[None, :]).astype(np.float32) / HEAD_DIM, BF16)


def _dft_constants():
    d = np.arange(FOURIER_DIM)
    ang = 2.0 * np.pi * ((d[:, None] * d[None, :]) % FOURIER_DIM) / FOURIER_DIM
    feat = np.concatenate([np.cos(ang), np.sin(ang)], axis=1) / np.sqrt(FOURIER_DIM)
    k1 = np.arange(FFT_N1)
    n1 = np.arange(FFT_N1)
    n2 = np.arange(FFT_N2)
    n = FFT_N2 * n1[None, None, :] + n2[:, None, None]
    ang1 = 2.0 * np.pi * ((k1[None, :, None] * n) % SEQ) / SEQ
    c1, s1 = np.cos(ang1) / np.sqrt(FFT_N1), np.sin(ang1) / np.sqrt(FFT_N1)
    seq1 = np.concatenate([np.concatenate([c1, -s1], axis=2), np.concatenate([s1, c1], axis=2)], axis=1)
    k2 = np.arange(FFT_N2)
    ang2 = 2.0 * np.pi * ((k2[:, None] * n2[None, :]) % FFT_N2) / FFT_N2
    seq2 = np.concatenate([np.cos(ang2), -np.sin(ang2)], axis=1) / np.sqrt(FFT_N2)
    return (jnp.asarray(feat, BF16), jnp.asarray(seq1, BF16), jnp.asarray(seq2, BF16))


def _constants():
    feat, seq1, seq2 = _dft_constants()
    return dict(e_blk=_head_mean_matrix(), dft_feat=feat, dft_seq1=seq1, dft_seq2=seq2)


def _na_bias_table(rpb):
    shift = np.arange(KERNEL_ROWS)
    off = np.arange(KERNEL_ROWS)
    dr = shift[:, None] + off[None, :]
    c = np.arange(GRID_W)
    dc = np.clip(c[None, :] - c[:, None], -(KERNEL_COLS - 1), KERNEL_COLS - 1) + (KERNEL_COLS - 1)
    cs = np.clip(c - KERNEL_COLS // 2, 0, GRID_W - KERNEL_COLS)
    inside = (c[None, :] >= cs[:, None]) & (c[None, :] < cs[:, None] + KERNEL_COLS)
    t = rpb.astype(F32)[:, dr]
    t = jnp.take(t, jnp.asarray(dc), axis=3)
    t = t.transpose(0, 1, 3, 2, 4)
    t = jnp.where(jnp.asarray(inside)[None, None, :, None, :], t, NEG_INF)
    t = t.reshape(NA_HEADS // NA_HEADS_PER_STEP, NA_HEADS_PER_STEP, KERNEL_ROWS, GRID_W, NA_KEYS)
    t = t.transpose(0, 2, 1, 3, 4)
    return t.reshape(NA_HEADS // NA_HEADS_PER_STEP, KERNEL_ROWS, LANES, NA_KEYS)


def _ffn_weights(w_in, w_out):
    gate = w_in[:, :D_FF].reshape(D_MODEL, N_FF_CHUNKS, FF_CHUNK)
    up = w_in[:, D_FF:].reshape(D_MODEL, N_FF_CHUNKS, FF_CHUNK)
    win_c = jnp.concatenate([gate, up], axis=2).transpose(1, 0, 2).astype(BF16)
    wout_c = w_out.reshape(N_FF_CHUNKS, FF_CHUNK, D_MODEL).astype(BF16)
    return win_c, wout_c


def _row(v):
    return v.reshape(1, -1).astype(F32)


def _tile_heads(g, heads):
    return jnp.tile(g.astype(F32), heads).reshape(1, -1)


def _trunk(x, mem, p, consts):
    b = x.shape[0]
    x2 = x.reshape(b * SEQ, D_MODEL)
    for i in range(DEPTH):
        x2 = _ffn(x2, p["norm_ffn1"][i], *p["ffn1"][i], p["norm_out"][i], False)
        km, vm = _memkv(mem, p["norm_mem"][i], p["w_mem_kv"][i], p["mem_k_norm"][i], consts)
        if i % 2 == 0:
            q, k, v, qm = _mixin_a(x2, p["norm_mix"][i], p["w_in_a"], p["na_q_norm"], p["na_k_norm"],
                                   p["mem_q_norm"][i], consts)
            seq3 = lambda t: t.reshape(b, SEQ, t.shape[-1])
            mixed = _na(seq3(q), seq3(k), seq3(v), p["na_bias"])
            w_out = p["w_out_a"]
        else:
            a, bb, qm = _mixin_b(x2, p["norm_mix"][i], p["w_in_b"], p["mem_q_norm"][i], consts)
            mixed = _seq_dft_real(a.reshape(b, SEQ, FOURIER_DIM), bb.reshape(b, SEQ, FOURIER_DIM), consts)
            w_out = p["w_out_b"]
        x3 = _mixout(x2.reshape(b, SEQ, D_MODEL), mixed, qm.reshape(b, SEQ, MEM_DIM), km, vm,
                     w_out[0], w_out[1])
        x2 = x3.reshape(b * SEQ, D_MODEL)
        x2 = _ffn(x2, p["norm_ffn2"][i], *p["ffn2"][i], p["norm_out"][i], True)
    return x2.reshape(b, SEQ, D_MODEL)


def kernel(x_prompt, x_sample, mem_prompt, mem_sample, norm_ffn1, w_ffn1_in, w_ffn1_out, norm_mix, norm_mem,
           w_mem_kv, mem_q_norm, mem_k_norm, w_in_a, na_q_norm, na_k_norm, na_rpb, w_out_a, w_in_b, w_out_b,
           norm_ffn2, w_ffn2_in, w_ffn2_out, norm_out):
    assert x_prompt.shape[1:] == (SEQ, D_MODEL) and x_sample.shape[1:] == (SEQ, D_MODEL)
    consts = _constants()
    split_out = lambda w: (w[:NA_DIM].astype(BF16), w[NA_DIM:].astype(BF16))
    p = dict(
        norm_ffn1=[_row(norm_ffn1[i]) for i in range(DEPTH)],
        norm_ffn2=[_row(norm_ffn2[i]) for i in range(DEPTH)],
        norm_mix=[_row(norm_mix[i]) for i in range(DEPTH)],
        norm_mem=[_row(norm_mem[i]) for i in range(DEPTH)],
        norm_out=[_row(norm_out[i]) for i in range(DEPTH)],
        ffn1=[_ffn_weights(w_ffn1_in[i], w_ffn1_out[i]) for i in range(DEPTH)],
        ffn2=[_ffn_weights(w_ffn2_in[i], w_ffn2_out[i]) for i in range(DEPTH)],
        w_mem_kv=[w_mem_kv[i].astype(BF16) for i in range(DEPTH)],
        mem_q_norm=[_tile_heads(mem_q_norm[i], MEM_HEADS) for i in range(DEPTH)],
        mem_k_norm=[_tile_heads(mem_k_norm[i], MEM_HEADS) for i in range(DEPTH)],
        w_in_a=w_in_a[0].astype(BF16),
        na_q_norm=_tile_heads(na_q_norm[0], NA_HEADS),
        na_k_norm=_tile_heads(na_k_norm[0], NA_HEADS),
        na_bias=_na_bias_table(na_rpb[0]),
        w_out_a=split_out(w_out_a[0]),
        w_in_b=w_in_b[0].astype(BF16),
        w_out_b=split_out(w_out_b[0]),
    )
    return (_trunk(x_prompt, mem_prompt, p, consts), _trunk(x_sample, mem_sample, p, consts))
```

```python
import functools

import numpy as np
import jax
import jax.numpy as jnp
from jax import lax
from jax.experimental import pallas as pl
from jax.experimental.pallas import tpu as pltpu

D_MODEL = 1024
DEPTH = 2
HEAD_DIM = 64
MEM_HEADS = 4
MEM_DIM = MEM_HEADS * HEAD_DIM
N_MEM = 256
NA_HEADS = 12
NA_DIM = NA_HEADS * HEAD_DIM
FOURIER_DIM = 768
GRID_W = 64
KERNEL_ROWS = 8
KERNEL_COLS = 16
D_FF = 2816
EPS = 1e-6
NEG_INF = -1e30
LOG2_E = 1.4426950408889634
SEQ = 8192
ROWS = SEQ // GRID_W
FFT_N1 = 64
FFT_N2 = 128

F32 = jnp.float32
BF16 = jnp.bfloat16

LANES = 128
MXU_WIDTH = 256
BF16_ROWS = 16
FF_CHUNK = MXU_WIDTH
N_FF_CHUNKS = D_FF // FF_CHUNK
SUBTILE_ROWS = 512
SUBTILES = 2
TOKEN_TILE = SUBTILES * SUBTILE_ROWS
VMEM_LIMIT = 56 * 1024 * 1024


def _params(n_axes):
    return pltpu.CompilerParams(dimension_semantics=("arbitrary",) * n_axes,
                                vmem_limit_bytes=VMEM_LIMIT)


def _const_spec(shape):
    n = len(shape)
    return pl.BlockSpec(shape, lambda *_: (0,) * n, pipeline_mode=pl.Buffered(1))


def _rms(x, g):
    ms = jnp.mean(x * x, axis=-1, keepdims=True)
    return x * lax.rsqrt(ms + EPS) * g


def _head_rmsnorm(t, gain, e_blk):
    sq = (t * t).astype(BF16)
    slabs = [jnp.dot(sq[:, c:c + MXU_WIDTH], e_blk, preferred_element_type=F32)
             for c in range(0, t.shape[-1], MXU_WIDTH)]
    ms = slabs[0] if len(slabs) == 1 else jnp.concatenate(slabs, axis=-1)
    return t * lax.rsqrt(ms + EPS) * gain


def _head_lane_mask(shape, head, lane_axis):
    lane = lax.broadcasted_iota(jnp.int32, shape, lane_axis)
    return (lane >= head * HEAD_DIM) & (lane < (head + 1) * HEAD_DIM)


def _ffn_kernel(x_ref, g_ref, win_ref, wout_ref, gout_ref, o_ref, xn_ref, acc_ref, *, final_norm):
    for s in range(SUBTILES):
        rows = slice(s * SUBTILE_ROWS, (s + 1) * SUBTILE_ROWS)
        xn_ref[rows] = _rms(x_ref[rows], g_ref[...]).astype(BF16)
        for c in range(N_FF_CHUNKS):
            cols = slice(c * FF_CHUNK, (c + 1) * FF_CHUNK)
            up_cols = slice(D_FF + c * FF_CHUNK, D_FF + (c + 1) * FF_CHUNK)
            gate = jnp.dot(xn_ref[rows], win_ref[:, cols], preferred_element_type=F32)
            up = jnp.dot(xn_ref[rows], win_ref[:, up_cols], preferred_element_type=F32)
            act = (gate * jax.nn.sigmoid(gate) * up).astype(BF16)
            part = jnp.dot(act, wout_ref[cols, :], preferred_element_type=F32)
            if c == 0:
                acc_ref[rows] = part
            elif c < N_FF_CHUNKS - 1:
                acc_ref[rows] += part
            else:
                y = x_ref[rows] + 0.5 * (acc_ref[rows] + part)
        if final_norm:
            y = _rms(y, gout_ref[...])
        o_ref[rows] = y


def _ffn(x2d, g, win_c, wout_c, gout, final_norm):
    t = x2d.shape[0]
    tm = TOKEN_TILE
    return pl.pallas_call(
        functools.partial(_ffn_kernel, final_norm=final_norm),
        out_shape=jax.ShapeDtypeStruct((t, D_MODEL), F32),
        grid=(t // tm,),
        in_specs=[
            pl.BlockSpec((tm, D_MODEL), lambda i: (i, 0)),
            _const_spec((1, D_MODEL)),
            _const_spec((D_MODEL, 2 * D_FF)),
            _const_spec((D_FF, D_MODEL)),
            _const_spec((1, D_MODEL)),
        ],
        out_specs=pl.BlockSpec((tm, D_MODEL), lambda i: (i, 0)),
        scratch_shapes=[pltpu.VMEM((tm, D_MODEL), BF16), pltpu.VMEM((tm, D_MODEL), F32)],
        compiler_params=_params(1),
        name="ffn_final" if final_norm else "ffn",
    )(x2d, g, win_c, wout_c, gout)


def _mixin_a_kernel(x_ref, g_ref, w_ref, gq_ref, gk_ref, gqm_ref, e_ref, q_ref, k_ref, v_ref, qm_ref):
    scale = HEAD_DIM ** -0.5
    e_blk = e_ref[...]
    for s in range(SUBTILES):
        rows = slice(s * SUBTILE_ROWS, (s + 1) * SUBTILE_ROWS)
        h = _rms(x_ref[rows], g_ref[...]).astype(BF16)
        proj = jnp.dot(h, w_ref[...], preferred_element_type=F32)
        q = proj[:, :NA_DIM]
        k = proj[:, NA_DIM:2 * NA_DIM]
        v = proj[:, 2 * NA_DIM:3 * NA_DIM]
        qm = proj[:, 3 * NA_DIM:]
        q_ref[rows] = (_head_rmsnorm(q, gq_ref[...], e_blk) * (scale * LOG2_E)).astype(BF16)
        k_ref[rows] = _head_rmsnorm(k, gk_ref[...], e_blk).astype(BF16)
        v_ref[rows] = v.astype(BF16)
        qm_ref[rows] = (_head_rmsnorm(qm, gqm_ref[...], e_blk) * scale).astype(BF16)


def _mixin_a(x2d, g, w, gq, gk, gqm, consts):
    t = x2d.shape[0]
    tm = TOKEN_TILE
    width = 3 * NA_DIM + MEM_DIM
    tok = lambda n: pl.BlockSpec((tm, n), lambda i: (i, 0))
    return pl.pallas_call(
        _mixin_a_kernel,
        out_shape=(jax.ShapeDtypeStruct((t, NA_DIM), BF16),) * 3 + (jax.ShapeDtypeStruct((t, MEM_DIM), BF16),),
        grid=(t // tm,),
        in_specs=[
            tok(D_MODEL),
            _const_spec((1, D_MODEL)),
            _const_spec((D_MODEL, width)),
            _const_spec((1, NA_DIM)), _const_spec((1, NA_DIM)), _const_spec((1, MEM_DIM)),
            _const_spec((MXU_WIDTH, MXU_WIDTH)),
        ],
        out_specs=(tok(NA_DIM), tok(NA_DIM), tok(NA_DIM), tok(MEM_DIM)),
        compiler_params=_params(1),
        name="mixin_na",
    )(x2d, g, w, gq, gk, gqm, consts["e_blk"])


def _mixin_b_kernel(x_ref, g_ref, w_ref, gqm_ref, e_ref, dft_ref, a_ref, b_ref, qm_ref):
    scale = HEAD_DIM ** -0.5
    for s in range(SUBTILES):
        rows = slice(s * SUBTILE_ROWS, (s + 1) * SUBTILE_ROWS)
        h = _rms(x_ref[rows], g_ref[...]).astype(BF16)
        proj = jnp.dot(h, w_ref[...], preferred_element_type=F32)
        z = proj[:, :FOURIER_DIM].astype(BF16)
        qm = proj[:, FOURIER_DIM:]
        ab = jnp.dot(z, dft_ref[...], preferred_element_type=F32)
        a_ref[rows] = ab[:, :FOURIER_DIM].astype(BF16)
        b_ref[rows] = ab[:, FOURIER_DIM:].astype(BF16)
        qm_ref[rows] = (_head_rmsnorm(qm, gqm_ref[...], e_ref[...]) * scale).astype(BF16)


def _mixin_b(x2d, g, w, gqm, consts):
    t = x2d.shape[0]
    tm = TOKEN_TILE
    tok = lambda n: pl.BlockSpec((tm, n), lambda i: (i, 0))
    return pl.pallas_call(
        _mixin_b_kernel,
        out_shape=(jax.ShapeDtypeStruct((t, FOURIER_DIM), BF16),) * 2 + (jax.ShapeDtypeStruct((t, MEM_DIM), BF16),),
        grid=(t // tm,),
        in_specs=[
            tok(D_MODEL),
            _const_spec((1, D_MODEL)),
            _const_spec((D_MODEL, FOURIER_DIM + MEM_DIM)),
            _const_spec((1, MEM_DIM)),
            _const_spec((MXU_WIDTH, MXU_WIDTH)),
            _const_spec((FOURIER_DIM, 2 * FOURIER_DIM)),
        ],
        out_specs=(tok(FOURIER_DIM), tok(FOURIER_DIM), tok(MEM_DIM)),
        compiler_params=_params(1),
        name="mixin_fnet",
    )(x2d, g, w, gqm, consts["e_blk"], consts["dft_feat"])


NA_HEADS_PER_STEP = LANES // HEAD_DIM
NA_KEYS = KERNEL_ROWS * GRID_W
NA_TICK_ROWS = 4
NA_TICKS = ROWS // NA_TICK_ROWS


def _na_window(r):
    if isinstance(r, int):
        rs = min(max(r - KERNEL_ROWS // 2, 0), ROWS - KERNEL_ROWS)
        return rs * GRID_W, rs - r + (KERNEL_ROWS - 1)
    rs = jnp.clip(r - KERNEL_ROWS // 2, 0, ROWS - KERNEL_ROWS)
    return pl.multiple_of(rs * GRID_W, GRID_W), rs - r + (KERNEL_ROWS - 1)


def _na_row_start(r):
    return r * GRID_W if isinstance(r, int) else pl.multiple_of(r * GRID_W, GRID_W)


def _na_kernel(q_ref, k_ref, v_ref, bias_ref, o_ref, s_ref, e_ref, l_ref):
    first_head = _head_lane_mask((GRID_W, LANES), 0, 1)

    def scores(t, slot):
        for j in range(NA_TICK_ROWS):
            r = t * NA_TICK_ROWS + j
            key_start, shift = _na_window(r)
            q = q_ref[pl.ds(_na_row_start(r), GRID_W), :]
            zero = jnp.zeros_like(q)
            q2 = jnp.concatenate([jnp.where(first_head, q, zero), jnp.where(first_head, zero, q)], axis=0)
            kb = k_ref[pl.ds(key_start, NA_KEYS), :]
            s = lax.dot_general(q2, kb, (((1,), (1,)), ((), ())), preferred_element_type=F32)
            s_ref[slot, j] = s + bias_ref[shift]

    def softmax(slot):
        for j in range(NA_TICK_ROWS):
            s = s_ref[slot, j]
            m = jnp.max(s, axis=-1, keepdims=True)
            e = jnp.exp2(s - m)
            l = jnp.sum(e, axis=-1, keepdims=True)
            e_ref[slot, j] = e.astype(BF16)
            l_ref[slot, j] = jnp.broadcast_to(1.0 / l, (LANES, LANES))

    def values(t, slot):
        for j in range(NA_TICK_ROWS):
            r = t * NA_TICK_ROWS + j
            key_start, _ = _na_window(r)
            vb = v_ref[pl.ds(key_start, NA_KEYS), :]
            o2 = jnp.dot(e_ref[slot, j], vb, preferred_element_type=F32) * l_ref[slot, j]
            o = jnp.where(first_head, o2[:GRID_W], o2[GRID_W:])
            o_ref[pl.ds(_na_row_start(r), GRID_W), :] = o.astype(BF16)

    scores(0, 0)
    scores(1, 1)
    softmax(0)

    def body(u, carry):
        t = 2 * u
        scores(t, 0)
        values(t - 2, 0)
        softmax(1)
        scores(t + 1, 1)
        values(t - 1, 1)
        softmax(0)
        return carry

    lax.fori_loop(1, NA_TICKS // 2, body, 0)
    values(NA_TICKS - 2, 0)
    softmax(1)
    values(NA_TICKS - 1, 1)


def _na(q3, k3, v3, bias):
    b = q3.shape[0]
    n_pairs = NA_HEADS // NA_HEADS_PER_STEP
    seq_spec = pl.BlockSpec((None, SEQ, LANES), lambda i, j: (i, 0, j))
    return pl.pallas_call(
        _na_kernel,
        out_shape=jax.ShapeDtypeStruct((b, SEQ, NA_DIM), BF16),
        grid=(b, n_pairs),
        in_specs=[seq_spec, seq_spec, seq_spec,
                  pl.BlockSpec((None, KERNEL_ROWS, LANES, NA_KEYS), lambda i, j: (j, 0, 0, 0))],
        out_specs=seq_spec,
        scratch_shapes=[pltpu.VMEM((2, NA_TICK_ROWS, LANES, NA_KEYS), F32),
                        pltpu.VMEM((2, NA_TICK_ROWS, LANES, NA_KEYS), BF16),
                        pltpu.VMEM((2, NA_TICK_ROWS, LANES, LANES), F32)],
        compiler_params=_params(2),
        name="na_attention",
    )(q3, k3, v3, bias)


FFT_G = BF16_ROWS


def _swap_leading(x):
    return pltpu.einshape("abc->bac", x)


def _fft_stage1_kernel(a_ref, b_ref, g_ref, yr_ref, yi_ref, sr_ref, si_ref):
    a_t = _swap_leading(a_ref[...])
    b_t = _swap_leading(b_ref[...])
    for j in range(FFT_G):
        x = jnp.concatenate([a_t[j], b_t[j]], axis=0)
        y = jnp.dot(g_ref[j], x, preferred_element_type=F32)
        sr_ref[j] = y[:FFT_N1].astype(BF16)
        si_ref[j] = y[FFT_N1:].astype(BF16)
    yr_ref[...] = _swap_leading(sr_ref[...])
    yi_ref[...] = _swap_leading(si_ref[...])


def _fft_stage2_kernel(yr_ref, yi_ref, h_ref, f_ref, s_ref):
    for j in range(FFT_G):
        x = jnp.concatenate([yr_ref[j], yi_ref[j]], axis=0)
        s_ref[j] = jnp.dot(h_ref[...], x, preferred_element_type=F32).astype(BF16)
    f_ref[...] = _swap_leading(s_ref[...])


def _seq_dft_real(a3, b3, consts):
    b = a3.shape[0]
    n2_groups = FFT_N2 // FFT_G
    k1_groups = FFT_N1 // FFT_G
    split1 = (b, FFT_N1, n2_groups, FFT_G, FOURIER_DIM)
    blk1 = pl.BlockSpec((None, FFT_N1, None, FFT_G, FOURIER_DIM), lambda i, j: (i, 0, j, 0, 0))
    stage = pltpu.VMEM((FFT_G, FFT_N1, FOURIER_DIM), BF16)
    yr, yi = pl.pallas_call(
        _fft_stage1_kernel,
        out_shape=(jax.ShapeDtypeStruct(split1, BF16),) * 2,
        grid=(b, n2_groups),
        in_specs=[blk1, blk1,
                  pl.BlockSpec((FFT_G, 2 * FFT_N1, 2 * FFT_N1), lambda i, j: (j, 0, 0))],
        out_specs=(blk1, blk1),
        scratch_shapes=[stage, stage],
        compiler_params=_params(2),
        name="fft_stage1",
    )(a3.reshape(split1), b3.reshape(split1), consts["dft_seq1"])
    yr = yr.reshape(b, FFT_N1, FFT_N2, FOURIER_DIM)
    yi = yi.reshape(b, FFT_N1, FFT_N2, FOURIER_DIM)
    blk2 = pl.BlockSpec((None, FFT_G, FFT_N2, FOURIER_DIM), lambda i, j: (i, j, 0, 0))
    f = pl.pallas_call(
        _fft_stage2_kernel,
        out_shape=jax.ShapeDtypeStruct((b, FFT_N2, k1_groups, FFT_G, FOURIER_DIM), BF16),
        grid=(b, k1_groups),
        in_specs=[blk2, blk2, _const_spec((FFT_N2, 2 * FFT_N2))],
        out_specs=pl.BlockSpec((None, FFT_N2, None, FFT_G, FOURIER_DIM), lambda i, j: (i, 0, j, 0, 0)),
        scratch_shapes=[pltpu.VMEM((FFT_G, FFT_N2, FOURIER_DIM), BF16)],
        compiler_params=_params(2),
        name="fft_stage2",
    )(yr, yi, consts["dft_seq2"])
    return f.reshape(b, SEQ, FOURIER_DIM)


def _memkv_kernel(mem_ref, g_ref, w_ref, gk_ref, e_ref, k_ref, v_ref):
    h = _rms(mem_ref[...], g_ref[...]).astype(BF16)
    kv = jnp.dot(h, w_ref[...], preferred_element_type=F32)
    k = _head_rmsnorm(kv[:, :MEM_DIM], gk_ref[...], e_ref[...])
    k_ref[...] = k.T.astype(BF16)
    v_ref[...] = kv[:, MEM_DIM:].astype(BF16)


def _memkv(mem, g, w, gk, consts):
    b = mem.shape[0]
    out_blk = pl.BlockSpec((None, N_MEM, MEM_DIM), lambda i: (i, 0, 0))
    return pl.pallas_call(
        _memkv_kernel,
        out_shape=(jax.ShapeDtypeStruct((b, N_MEM, MEM_DIM), BF16),) * 2,
        grid=(b,),
        in_specs=[
            pl.BlockSpec((None, N_MEM, D_MODEL), lambda i: (i, 0, 0)),
            _const_spec((1, D_MODEL)),
            _const_spec((D_MODEL, 2 * MEM_DIM)),
            _const_spec((1, MEM_DIM)),
            _const_spec((MXU_WIDTH, MXU_WIDTH)),
        ],
        out_specs=(out_blk, out_blk),
        compiler_params=_params(1),
        name="memory_kv",
    )(mem, g, w, gk, consts["e_blk"])


def _mixout_kernel(x_ref, a_ref, qm_ref, km_ref, vm_ref, wa_ref, wm_ref, o_ref):
    km = km_ref[...]
    vm = vm_ref[...]
    for s in range(SUBTILES):
        rows = slice(s * SUBTILE_ROWS, (s + 1) * SUBTILE_ROWS)
        qm = qm_ref[rows]
        zero = jnp.zeros_like(qm)
        mo = jnp.zeros(qm.shape, F32)
        for head in range(MEM_HEADS):
            in_head = _head_lane_mask(qm.shape, head, 1)
            sc = jnp.dot(jnp.where(in_head, qm, zero), km, preferred_element_type=F32)
            e = jnp.exp(sc - jnp.max(sc, axis=-1, keepdims=True))
            l = jnp.sum(e, axis=-1, keepdims=True)
            oh = jnp.dot(e.astype(BF16), vm, preferred_element_type=F32) * (1.0 / l)
            mo = jnp.where(in_head, oh, mo)
        y = jnp.dot(a_ref[rows], wa_ref[...], preferred_element_type=F32)
        y = y + jnp.dot(mo.astype(BF16), wm_ref[...], preferred_element_type=F32)
        o_ref[rows] = x_ref[rows] + y


def _mixout(x3, a3, qm3, km, vm, w_a, w_m):
    b = x3.shape[0]
    tm = TOKEN_TILE
    tok = lambda n: pl.BlockSpec((None, tm, n), lambda i, j: (i, j, 0))
    mem_blk = pl.BlockSpec((None, N_MEM, MEM_DIM), lambda i, j: (i, 0, 0))
    mix = a3.shape[-1]
    return pl.pallas_call(
        _mixout_kernel,
        out_shape=jax.ShapeDtypeStruct(x3.shape, F32),
        grid=(b, SEQ // tm),
        in_specs=[tok(D_MODEL), tok(mix), tok(MEM_DIM), mem_blk, mem_blk,
                  _const_spec((mix, D_MODEL)), _const_spec((MEM_DIM, D_MODEL))],
        out_specs=tok(D_MODEL),
        compiler_params=_params(2),
        name="mixer_out",
    )(x3, a3, qm3, km, vm, w_a, w_m)


def _head_mean_matrix():
    head = np.arange(MXU_WIDTH) // HEAD_DIM
    return jnp.asarray((head[:, None] == head[None, :]).astype(np.float32) / HEAD_DIM, BF16)


def _dft_constants():
    d = np.arange(FOURIER_DIM)
    ang = 2.0 * np.pi * ((d[:, None] * d[None, :]) % FOURIER_DIM) / FOURIER_DIM
    feat = np.concatenate([np.cos(ang), np.sin(ang)], axis=1) / np.sqrt(FOURIER_DIM)
    k1 = np.arange(FFT_N1)
    n1 = np.arange(FFT_N1)
    n2 = np.arange(FFT_N2)
    n = FFT_N2 * n1[None, None, :] + n2[:, None, None]
    ang1 = 2.0 * np.pi * ((k1[None, :, None] * n) % SEQ) / SEQ
    c1, s1 = np.cos(ang1) / np.sqrt(FFT_N1), np.sin(ang1) / np.sqrt(FFT_N1)
    seq1 = np.concatenate([np.concatenate([c1, -s1], axis=2), np.concatenate([s1, c1], axis=2)], axis=1)
    k2 = np.arange(FFT_N2)
    ang2 = 2.0 * np.pi * ((k2[:, None] * n2[None, :]) % FFT_N2) / FFT_N2
    seq2 = np.concatenate([np.cos(ang2), -np.sin(ang2)], axis=1) / np.sqrt(FFT_N2)
    return (jnp.asarray(feat, BF16), jnp.asarray(seq1, BF16), jnp.asarray(seq2, BF16))


def _constants():
    feat, seq1, seq2 = _dft_constants()
    return dict(e_blk=_head_mean_matrix(), dft_feat=feat, dft_seq1=seq1, dft_seq2=seq2)


def _na_bias_table(rpb):
    c = np.arange(GRID_W)
    dc = np.clip(c[None, :] - c[:, None], -(KERNEL_COLS - 1), KERNEL_COLS - 1) + (KERNEL_COLS - 1)
    cs = np.clip(c - KERNEL_COLS // 2, 0, GRID_W - KERNEL_COLS)
    inside = (c[None, :] >= cs[:, None]) & (c[None, :] < cs[:, None] + KERNEL_COLS)
    t = jnp.take(rpb.astype(F32) * LOG2_E, jnp.asarray(dc), axis=2)
    t = jnp.where(jnp.asarray(inside)[None, None], t, NEG_INF)
    t = jnp.stack([t[:, s:s + KERNEL_ROWS] for s in range(KERNEL_ROWS)], axis=1)
    t = t.transpose(0, 1, 3, 2, 4)
    t = t.reshape(NA_HEADS // NA_HEADS_PER_STEP, NA_HEADS_PER_STEP, KERNEL_ROWS, GRID_W, NA_KEYS)
    t = t.transpose(0, 2, 1, 3, 4)
    return t.reshape(NA_HEADS // NA_HEADS_PER_STEP, KERNEL_ROWS, LANES, NA_KEYS)


def _row(v):
    return v.reshape(1, -1).astype(F32)


def _tile_heads(g, heads):
    return jnp.tile(g.astype(F32), heads).reshape(1, -1)


def _trunk(x, mem, p, consts):
    b = x.shape[0]
    x2 = x.reshape(b * SEQ, D_MODEL)
    for i in range(DEPTH):
        x2 = _ffn(x2, p["norm_ffn1"][i], *p["ffn1"][i], p["norm_out"][i], False)
        km, vm = _memkv(mem, p["norm_mem"][i], p["w_mem_kv"][i], p["mem_k_norm"][i], consts)
        if i % 2 == 0:
            q, k, v, qm = _mixin_a(x2, p["norm_mix"][i], p["w_in_a"], p["na_q_norm"], p["na_k_norm"],
                                   p["mem_q_norm"][i], consts)
            seq3 = lambda t: t.reshape(b, SEQ, t.shape[-1])
            mixed = _na(seq3(q), seq3(k), seq3(v), p["na_bias"])
            w_out = p["w_out_a"]
        else:
            a, bb, qm = _mixin_b(x2, p["norm_mix"][i], p["w_in_b"], p["mem_q_norm"][i], consts)
            mixed = _seq_dft_real(a.reshape(b, SEQ, FOURIER_DIM), bb.reshape(b, SEQ, FOURIER_DIM), consts)
            w_out = p["w_out_b"]
        x3 = _mixout(x2.reshape(b, SEQ, D_MODEL), mixed, qm.reshape(b, SEQ, MEM_DIM), km, vm,
                     w_out[0], w_out[1])
        x2 = x3.reshape(b * SEQ, D_MODEL)
        x2 = _ffn(x2, p["norm_ffn2"][i], *p["ffn2"][i], p["norm_out"][i], True)
    return x2.reshape(b, SEQ, D_MODEL)


def kernel(x_prompt, x_sample, mem_prompt, mem_sample, norm_ffn1, w_ffn1_in, w_ffn1_out, norm_mix, norm_mem,
           w_mem_kv, mem_q_norm, mem_k_norm, w_in_a, na_q_norm, na_k_norm, na_rpb, w_out_a, w_in_b, w_out_b,
           norm_ffn2, w_ffn2_in, w_ffn2_out, norm_out):
    assert x_prompt.shape[1:] == (SEQ, D_MODEL) and x_sample.shape[1:] == (SEQ, D_MODEL)
    consts = _constants()
    split_out = lambda w: (w[:NA_DIM].astype(BF16), w[NA_DIM:].astype(BF16))
    p = dict(
        norm_ffn1=[_row(norm_ffn1[i]) for i in range(DEPTH)],
        norm_ffn2=[_row(norm_ffn2[i]) for i in range(DEPTH)],
        norm_mix=[_row(norm_mix[i]) for i in range(DEPTH)],
        norm_mem=[_row(norm_mem[i]) for i in range(DEPTH)],
        norm_out=[_row(norm_out[i]) for i in range(DEPTH)],
        ffn1=[(w_ffn1_in[i].astype(BF16), w_ffn1_out[i].astype(BF16)) for i in range(DEPTH)],
        ffn2=[(w_ffn2_in[i].astype(BF16), w_ffn2_out[i].astype(BF16)) for i in range(DEPTH)],
        w_mem_kv=[w_mem_kv[i].astype(BF16) for i in range(DEPTH)],
        mem_q_norm=[_tile_heads(mem_q_norm[i], MEM_HEADS) for i in range(DEPTH)],
        mem_k_norm=[_tile_heads(mem_k_norm[i], MEM_HEADS) for i in range(DEPTH)],
        w_in_a=w_in_a[0].astype(BF16),
        na_q_norm=_tile_heads(na_q_norm[0], NA_HEADS),
        na_k_norm=_tile_heads(na_k_norm[0], NA_HEADS),
        na_bias=_na_bias_table(na_rpb[0]),
        w_out_a=split_out(w_out_a[0]),
        w_in_b=w_in_b[0].astype(BF16),
        w_out_b=split_out(w_out_b[0]),
    )
    return (_trunk(x_prompt, mem_prompt, p, consts), _trunk(x_sample, mem_sample, p, consts))
```

```python
import functools

import numpy as np
import jax
import jax.numpy as jnp
from jax import lax
from jax.experimental import pallas as pl
from jax.experimental.pallas import tpu as pltpu

D_MODEL = 1024
DEPTH = 2
HEAD_DIM = 64
MEM_HEADS = 4
MEM_DIM = MEM_HEADS * HEAD_DIM
N_MEM = 256
NA_HEADS = 12
NA_DIM = NA_HEADS * HEAD_DIM
FOURIER_DIM = 768
GRID_W = 64
KERNEL_ROWS = 8
KERNEL_COLS = 16
D_FF = 2816
EPS = 1e-6
NEG_INF = -1e30
LOG2_E = 1.4426950408889634
SEQ = 8192
ROWS = SEQ // GRID_W
FFT_N1 = 64
FFT_N2 = 128

F32 = jnp.float32
BF16 = jnp.bfloat16

LANES = 128
MXU_WIDTH = 256
BF16_ROWS = 16
FF_CHUNK = MXU_WIDTH
N_FF_CHUNKS = D_FF // FF_CHUNK
SUBTILE_ROWS = 512
SUBTILES = 2
TOKEN_TILE = SUBTILES * SUBTILE_ROWS
VMEM_LIMIT = 56 * 1024 * 1024


def _params(n_axes):
    return pltpu.CompilerParams(dimension_semantics=("arbitrary",) * n_axes,
                                vmem_limit_bytes=VMEM_LIMIT)


def _const_spec(shape):
    n = len(shape)
    return pl.BlockSpec(shape, lambda *_: (0,) * n, pipeline_mode=pl.Buffered(1))


def _rms(x, g):
    ms = jnp.mean(x * x, axis=-1, keepdims=True)
    return x * lax.rsqrt(ms + EPS) * g


def _head_rmsnorm(t, gain, e_blk):
    sq = (t * t).astype(BF16)
    slabs = [jnp.dot(sq[:, c:c + MXU_WIDTH], e_blk, preferred_element_type=F32)
             for c in range(0, t.shape[-1], MXU_WIDTH)]
    ms = slabs[0] if len(slabs) == 1 else jnp.concatenate(slabs, axis=-1)
    return t * lax.rsqrt(ms + EPS) * gain


def _head_lane_mask(shape, head, lane_axis):
    lane = lax.broadcasted_iota(jnp.int32, shape, lane_axis)
    return (lane >= head * HEAD_DIM) & (lane < (head + 1) * HEAD_DIM)


def _ffn_kernel(x_ref, g_ref, win_ref, wout_ref, gout_ref, o_ref, xn_ref, acc_ref, *, final_norm):
    for s in range(SUBTILES):
        rows = slice(s * SUBTILE_ROWS, (s + 1) * SUBTILE_ROWS)
        xn_ref[rows] = _rms(x_ref[rows], g_ref[...]).astype(BF16)
        for c in range(N_FF_CHUNKS):
            cols = slice(c * FF_CHUNK, (c + 1) * FF_CHUNK)
            up_cols = slice(D_FF + c * FF_CHUNK, D_FF + (c + 1) * FF_CHUNK)
            gate = jnp.dot(xn_ref[rows], win_ref[:, cols], preferred_element_type=F32)
            up = jnp.dot(xn_ref[rows], win_ref[:, up_cols], preferred_element_type=F32)
            act = (gate * jax.nn.sigmoid(gate) * up).astype(BF16)
            part = jnp.dot(act, wout_ref[cols, :], preferred_element_type=F32)
            if c == 0:
                acc_ref[rows] = part
            elif c < N_FF_CHUNKS - 1:
                acc_ref[rows] += part
            else:
                y = x_ref[rows] + 0.5 * (acc_ref[rows] + part)
        if final_norm:
            y = _rms(y, gout_ref[...])
        o_ref[rows] = y


def _ffn(x2d, g, win_c, wout_c, gout, final_norm):
    t = x2d.shape[0]
    tm = TOKEN_TILE
    return pl.pallas_call(
        functools.partial(_ffn_kernel, final_norm=final_norm),
        out_shape=jax.ShapeDtypeStruct((t, D_MODEL), F32),
        grid=(t // tm,),
        in_specs=[
            pl.BlockSpec((tm, D_MODEL), lambda i: (i, 0)),
            _const_spec((1, D_MODEL)),
            _const_spec((D_MODEL, 2 * D_FF)),
            _const_spec((D_FF, D_MODEL)),
            _const_spec((1, D_MODEL)),
        ],
        out_specs=pl.BlockSpec((tm, D_MODEL), lambda i: (i, 0)),
        scratch_shapes=[pltpu.VMEM((tm, D_MODEL), BF16), pltpu.VMEM((tm, D_MODEL), F32)],
        compiler_params=_params(1),
        name="ffn_final" if final_norm else "ffn",
    )(x2d, g, win_c, wout_c, gout)


def _mixin_a_kernel(x_ref, g_ref, w_ref, gq_ref, gk_ref, gqm_ref, e_ref, q_ref, k_ref, v_ref, qm_ref):
    scale = HEAD_DIM ** -0.5
    e_blk = e_ref[...]
    for s in range(SUBTILES):
        rows = slice(s * SUBTILE_ROWS, (s + 1) * SUBTILE_ROWS)
        h = _rms(x_ref[rows], g_ref[...]).astype(BF16)
        proj = jnp.dot(h, w_ref[...], preferred_element_type=F32)
        q = proj[:, :NA_DIM]
        k = proj[:, NA_DIM:2 * NA_DIM]
        v = proj[:, 2 * NA_DIM:3 * NA_DIM]
        qm = proj[:, 3 * NA_DIM:]
        q_ref[rows] = (_head_rmsnorm(q, gq_ref[...], e_blk) * (scale * LOG2_E)).astype(BF16)
        k_ref[rows] = _head_rmsnorm(k, gk_ref[...], e_blk).astype(BF16)
        v_ref[rows] = v.astype(BF16)
        qm_ref[rows] = (_head_rmsnorm(qm, gqm_ref[...], e_blk) * scale).astype(BF16)


def _mixin_a(x2d, g, w, gq, gk, gqm, consts):
    t = x2d.shape[0]
    tm = TOKEN_TILE
    width = 3 * NA_DIM + MEM_DIM
    tok = lambda n: pl.BlockSpec((tm, n), lambda i: (i, 0))
    return pl.pallas_call(
        _mixin_a_kernel,
        out_shape=(jax.ShapeDtypeStruct((t, NA_DIM), BF16),) * 3 + (jax.ShapeDtypeStruct((t, MEM_DIM), BF16),),
        grid=(t // tm,),
        in_specs=[
            tok(D_MODEL),
            _const_spec((1, D_MODEL)),
            _const_spec((D_MODEL, width)),
            _const_spec((1, NA_DIM)), _const_spec((1, NA_DIM)), _const_spec((1, MEM_DIM)),
            _const_spec((MXU_WIDTH, MXU_WIDTH)),
        ],
        out_specs=(tok(NA_DIM), tok(NA_DIM), tok(NA_DIM), tok(MEM_DIM)),
        compiler_params=_params(1),
        name="mixin_na",
    )(x2d, g, w, gq, gk, gqm, consts["e_blk"])


def _mixin_b_kernel(x_ref, g_ref, w_ref, gqm_ref, e_ref, dft_ref, a_ref, b_ref, qm_ref):
    scale = HEAD_DIM ** -0.5
    for s in range(SUBTILES):
        rows = slice(s * SUBTILE_ROWS, (s + 1) * SUBTILE_ROWS)
        h = _rms(x_ref[rows], g_ref[...]).astype(BF16)
        proj = jnp.dot(h, w_ref[...], preferred_element_type=F32)
        z = proj[:, :FOURIER_DIM].astype(BF16)
        qm = proj[:, FOURIER_DIM:]
        ab = jnp.dot(z, dft_ref[...], preferred_element_type=F32)
        a_ref[rows] = ab[:, :FOURIER_DIM].astype(BF16)
        b_ref[rows] = ab[:, FOURIER_DIM:].astype(BF16)
        qm_ref[rows] = (_head_rmsnorm(qm, gqm_ref[...], e_ref[...]) * scale).astype(BF16)


def _mixin_b(x2d, g, w, gqm, consts):
    t = x2d.shape[0]
    tm = TOKEN_TILE
    tok = lambda n: pl.BlockSpec((tm, n), lambda i: (i, 0))
    return pl.pallas_call(
        _mixin_b_kernel,
        out_shape=(jax.ShapeDtypeStruct((t, FOURIER_DIM), BF16),) * 2 + (jax.ShapeDtypeStruct((t, MEM_DIM), BF16),),
        grid=(t // tm,),
        in_specs=[
            tok(D_MODEL),
            _const_spec((1, D_MODEL)),
            _const_spec((D_MODEL, FOURIER_DIM + MEM_DIM)),
            _const_spec((1, MEM_DIM)),
            _const_spec((MXU_WIDTH, MXU_WIDTH)),
            _const_spec((FOURIER_DIM, 2 * FOURIER_DIM)),
        ],
        out_specs=(tok(FOURIER_DIM), tok(FOURIER_DIM), tok(MEM_DIM)),
        compiler_params=_params(1),
        name="mixin_fnet",
    )(x2d, g, w, gqm, consts["e_blk"], consts["dft_feat"])


NA_HEADS_PER_STEP = LANES // HEAD_DIM
NA_KEYS = KERNEL_ROWS * GRID_W
NA_TICK_ROWS = 4
NA_TICKS = ROWS // NA_TICK_ROWS
NA_LOOP_TICKS = 30


def _na_window(r):
    if isinstance(r, int):
        rs = min(max(r - KERNEL_ROWS // 2, 0), ROWS - KERNEL_ROWS)
        return rs * GRID_W, rs - r + (KERNEL_ROWS - 1)
    rs = jnp.clip(r - KERNEL_ROWS // 2, 0, ROWS - KERNEL_ROWS)
    return pl.multiple_of(rs * GRID_W, GRID_W), rs - r + (KERNEL_ROWS - 1)


def _na_row_start(r):
    return r * GRID_W if isinstance(r, int) else pl.multiple_of(r * GRID_W, GRID_W)


def _na_kernel(q_ref, k_ref, v_ref, bias_ref, o_ref, s_ref, e_ref, l_ref):
    first_head = _head_lane_mask((GRID_W, LANES), 0, 1)

    def scores(t, slot):
        for j in range(NA_TICK_ROWS):
            r = t * NA_TICK_ROWS + j
            key_start, shift = _na_window(r)
            q = q_ref[pl.ds(_na_row_start(r), GRID_W), :]
            zero = jnp.zeros_like(q)
            q2 = jnp.concatenate([jnp.where(first_head, q, zero), jnp.where(first_head, zero, q)], axis=0)
            kb = k_ref[pl.ds(key_start, NA_KEYS), :]
            s = lax.dot_general(q2, kb, (((1,), (1,)), ((), ())), preferred_element_type=F32)
            s_ref[slot, j] = s + bias_ref[shift]

    def softmax(slot):
        for j in range(NA_TICK_ROWS):
            s = s_ref[slot, j]
            m = jnp.max(s, axis=-1, keepdims=True)
            e = jnp.exp2(s - m)
            l = jnp.sum(e, axis=-1, keepdims=True)
            e_ref[slot, j] = e.astype(BF16)
            l_ref[slot, j] = jnp.broadcast_to(1.0 / l, (LANES, LANES))

    def values(t, slot):
        for j in range(NA_TICK_ROWS):
            r = t * NA_TICK_ROWS + j
            key_start, _ = _na_window(r)
            vb = v_ref[pl.ds(key_start, NA_KEYS), :]
            o2 = jnp.dot(e_ref[slot, j], vb, preferred_element_type=F32) * l_ref[slot, j]
            o = jnp.where(first_head, o2[:GRID_W], o2[GRID_W:])
            o_ref[pl.ds(_na_row_start(r), GRID_W), :] = o.astype(BF16)

    def tick(t, parity, do_scores=True, do_values=True, do_softmax=True):
        if do_scores:
            scores(t, parity)
        if do_values:
            values(t - 2, parity)
        if do_softmax:
            softmax(1 - parity)

    tick(0, 0, do_values=False, do_softmax=False)
    tick(1, 1, do_values=False)

    def body(u, carry):
        for i in range(NA_LOOP_TICKS):
            tick(2 + NA_LOOP_TICKS * u + i, i % 2)
        return carry

    n_iter = (NA_TICKS - 2) // NA_LOOP_TICKS
    lax.fori_loop(0, n_iter, body, 0)
    for t in range(2 + n_iter * NA_LOOP_TICKS, NA_TICKS):
        tick(t, t % 2)
    tick(NA_TICKS, 0, do_scores=False)
    tick(NA_TICKS + 1, 1, do_scores=False, do_softmax=False)


def _na(q3, k3, v3, bias):
    b = q3.shape[0]
    n_pairs = NA_HEADS // NA_HEADS_PER_STEP
    seq_spec = pl.BlockSpec((None, SEQ, LANES), lambda i, j: (i, 0, j))
    return pl.pallas_call(
        _na_kernel,
        out_shape=jax.ShapeDtypeStruct((b, SEQ, NA_DIM), BF16),
        grid=(b, n_pairs),
        in_specs=[seq_spec, seq_spec, seq_spec,
                  pl.BlockSpec((None, KERNEL_ROWS, LANES, NA_KEYS), lambda i, j: (j, 0, 0, 0))],
        out_specs=seq_spec,
        scratch_shapes=[pltpu.VMEM((2, NA_TICK_ROWS, LANES, NA_KEYS), F32),
                        pltpu.VMEM((2, NA_TICK_ROWS, LANES, NA_KEYS), BF16),
                        pltpu.VMEM((2, NA_TICK_ROWS, LANES, LANES), F32)],
        compiler_params=_params(2),
        name="na_attention",
    )(q3, k3, v3, bias)


FFT_G = BF16_ROWS


def _swap_leading(x):
    return pltpu.einshape("abc->bac", x)


def _fft_stage1_kernel(a_ref, b_ref, g_ref, yr_ref, yi_ref, sr_ref, si_ref):
    a_t = _swap_leading(a_ref[...])
    b_t = _swap_leading(b_ref[...])
    for j in range(FFT_G):
        x = jnp.concatenate([a_t[j], b_t[j]], axis=0)
        y = jnp.dot(g_ref[j], x, preferred_element_type=F32)
        sr_ref[j] = y[:FFT_N1].astype(BF16)
        si_ref[j] = y[FFT_N1:].astype(BF16)
    yr_ref[...] = _swap_leading(sr_ref[...])
    yi_ref[...] = _swap_leading(si_ref[...])


def _fft_stage2_kernel(yr_ref, yi_ref, h_ref, f_ref, s_ref):
    for j in range(FFT_G):
        x = jnp.concatenate([yr_ref[j], yi_ref[j]], axis=0)
        s_ref[j] = jnp.dot(h_ref[...], x, preferred_element_type=F32).astype(BF16)
    f_ref[...] = _swap_leading(s_ref[...])


def _seq_dft_real(a3, b3, consts):
    b = a3.shape[0]
    n2_groups = FFT_N2 // FFT_G
    k1_groups = FFT_N1 // FFT_G
    split1 = (b, FFT_N1, n2_groups, FFT_G, FOURIER_DIM)
    blk1 = pl.BlockSpec((None, FFT_N1, None, FFT_G, FOURIER_DIM), lambda i, j: (i, 0, j, 0, 0))
    stage = pltpu.VMEM((FFT_G, FFT_N1, FOURIER_DIM), BF16)
    yr, yi = pl.pallas_call(
        _fft_stage1_kernel,
        out_shape=(jax.ShapeDtypeStruct(split1, BF16),) * 2,
        grid=(b, n2_groups),
        in_specs=[blk1, blk1,
                  pl.BlockSpec((FFT_G, 2 * FFT_N1, 2 * FFT_N1), lambda i, j: (j, 0, 0))],
        out_specs=(blk1, blk1),
        scratch_shapes=[stage, stage],
        compiler_params=_params(2),
        name="fft_stage1",
    )(a3.reshape(split1), b3.reshape(split1), consts["dft_seq1"])
    yr = yr.reshape(b, FFT_N1, FFT_N2, FOURIER_DIM)
    yi = yi.reshape(b, FFT_N1, FFT_N2, FOURIER_DIM)
    blk2 = pl.BlockSpec((None, FFT_G, FFT_N2, FOURIER_DIM), lambda i, j: (i, j, 0, 0))
    f = pl.pallas_call(
        _fft_stage2_kernel,
        out_shape=jax.ShapeDtypeStruct((b, FFT_N2, k1_groups, FFT_G, FOURIER_DIM), BF16),
        grid=(b, k1_groups),
        in_specs=[blk2, blk2, _const_spec((FFT_N2, 2 * FFT_N2))],
        out_specs=pl.BlockSpec((None, FFT_N2, None, FFT_G, FOURIER_DIM), lambda i, j: (i, 0, j, 0, 0)),
        scratch_shapes=[pltpu.VMEM((FFT_G, FFT_N2, FOURIER_DIM), BF16)],
        compiler_params=_params(2),
        name="fft_stage2",
    )(yr, yi, consts["dft_seq2"])
    return f.reshape(b, SEQ, FOURIER_DIM)


def _memkv_kernel(mem_ref, g_ref, w_ref, gk_ref, e_ref, k_ref, v_ref):
    h = _rms(mem_ref[...], g_ref[...]).astype(BF16)
    kv = jnp.dot(h, w_ref[...], preferred_element_type=F32)
    k = _head_rmsnorm(kv[:, :MEM_DIM], gk_ref[...], e_ref[...])
    k_ref[...] = k.T.astype(BF16)
    v_ref[...] = kv[:, MEM_DIM:].astype(BF16)


def _memkv(mem, g, w, gk, consts):
    b = mem.shape[0]
    out_blk = pl.BlockSpec((None, N_MEM, MEM_DIM), lambda i: (i, 0, 0))
    return pl.pallas_call(
        _memkv_kernel,
        out_shape=(jax.ShapeDtypeStruct((b, N_MEM, MEM_DIM), BF16),) * 2,
        grid=(b,),
        in_specs=[
            pl.BlockSpec((None, N_MEM, D_MODEL), lambda i: (i, 0, 0)),
            _const_spec((1, D_MODEL)),
            _const_spec((D_MODEL, 2 * MEM_DIM)),
            _const_spec((1, MEM_DIM)),
            _const_spec((MXU_WIDTH, MXU_WIDTH)),
        ],
        out_specs=(out_blk, out_blk),
        compiler_params=_params(1),
        name="memory_kv",
    )(mem, g, w, gk, consts["e_blk"])


def _mixout_kernel(x_ref, a_ref, qm_ref, km_ref, vm_ref, wa_ref, wm_ref, o_ref):
    km = km_ref[...]
    vm = vm_ref[...]
    for s in range(SUBTILES):
        rows = slice(s * SUBTILE_ROWS, (s + 1) * SUBTILE_ROWS)
        qm = qm_ref[rows]
        zero = jnp.zeros_like(qm)
        mo = jnp.zeros(qm.shape, F32)
        for head in range(MEM_HEADS):
            in_head = _head_lane_mask(qm.shape, head, 1)
            sc = jnp.dot(jnp.where(in_head, qm, zero), km, preferred_element_type=F32)
            e = jnp.exp(sc - jnp.max(sc, axis=-1, keepdims=True))
            l = jnp.sum(e, axis=-1, keepdims=True)
            oh = jnp.dot(e.astype(BF16), vm, preferred_element_type=F32) * (1.0 / l)
            mo = jnp.where(in_head, oh, mo)
        y = jnp.dot(a_ref[rows], wa_ref[...], preferred_element_type=F32)
        y = y + jnp.dot(mo.astype(BF16), wm_ref[...], preferred_element_type=F32)
        o_ref[rows] = x_ref[rows] + y


def _mixout(x3, a3, qm3, km, vm, w_a, w_m):
    b = x3.shape[0]
    tm = TOKEN_TILE
    tok = lambda n: pl.BlockSpec((None, tm, n), lambda i, j: (i, j, 0))
    mem_blk = pl.BlockSpec((None, N_MEM, MEM_DIM), lambda i, j: (i, 0, 0))
    mix = a3.shape[-1]
    return pl.pallas_call(
        _mixout_kernel,
        out_shape=jax.ShapeDtypeStruct(x3.shape, F32),
        grid=(b, SEQ // tm),
        in_specs=[tok(D_MODEL), tok(mix), tok(MEM_DIM), mem_blk, mem_blk,
                  _const_spec((mix, D_MODEL)), _const_spec((MEM_DIM, D_MODEL))],
        out_specs=tok(D_MODEL),
        compiler_params=_params(2),
        name="mixer_out",
    )(x3, a3, qm3, km, vm, w_a, w_m)


def _head_mean_matrix():
    head = np.arange(MXU_WIDTH) // HEAD_DIM
    return jnp.asarray((head[:, None] == head[None, :]).astype(np.float32) / HEAD_DIM, BF16)


def _dft_constants():
    d = np.arange(FOURIER_DIM)
    ang = 2.0 * np.pi * ((d[:, None] * d[None, :]) % FOURIER_DIM) / FOURIER_DIM
    feat = np.concatenate([np.cos(ang), np.sin(ang)], axis=1) / np.sqrt(FOURIER_DIM)
    k1 = np.arange(FFT_N1)
    n1 = np.arange(FFT_N1)
    n2 = np.arange(FFT_N2)
    n = FFT_N2 * n1[None, None, :] + n2[:, None, None]
    ang1 = 2.0 * np.pi * ((k1[None, :, None] * n) % SEQ) / SEQ
    c1, s1 = np.cos(ang1) / np.sqrt(FFT_N1), np.sin(ang1) / np.sqrt(FFT_N1)
    seq1 = np.concatenate([np.concatenate([c1, -s1], axis=2), np.concatenate([s1, c1], axis=2)], axis=1)
    k2 = np.arange(FFT_N2)
    ang2 = 2.0 * np.pi * ((k2[:, None] * n2[None, :]) % FFT_N2) / FFT_N2
    seq2 = np.concatenate([np.cos(ang2), -np.sin(ang2)], axis=1) / np.sqrt(FFT_N2)
    return (jnp.asarray(feat, BF16), jnp.asarray(seq1, BF16), jnp.asarray(seq2, BF16))


def _constants():
    feat, seq1, seq2 = _dft_constants()
    return dict(e_blk=_head_mean_matrix(), dft_feat=feat, dft_seq1=seq1, dft_seq2=seq2)


def _na_bias_table(rpb):
    c = np.arange(GRID_W)
    dc = np.clip(c[None, :] - c[:, None], -(KERNEL_COLS - 1), KERNEL_COLS - 1) + (KERNEL_COLS - 1)
    cs = np.clip(c - KERNEL_COLS // 2, 0, GRID_W - KERNEL_COLS)
    inside = (c[None, :] >= cs[:, None]) & (c[None, :] < cs[:, None] + KERNEL_COLS)
    t = jnp.take(rpb.astype(F32) * LOG2_E, jnp.asarray(dc), axis=2)
    t = jnp.where(jnp.asarray(inside)[None, None], t, NEG_INF)
    t = jnp.stack([t[:, s:s + KERNEL_ROWS] for s in range(KERNEL_ROWS)], axis=1)
    t = t.transpose(0, 1, 3, 2, 4)
    t = t.reshape(NA_HEADS // NA_HEADS_PER_STEP, NA_HEADS_PER_STEP, KERNEL_ROWS, GRID_W, NA_KEYS)
    t = t.transpose(0, 2, 1, 3, 4)
    return t.reshape(NA_HEADS // NA_HEADS_PER_STEP, KERNEL_ROWS, LANES, NA_KEYS)


def _row(v):
    return v.reshape(1, -1).astype(F32)


def _tile_heads(g, heads):
    return jnp.tile(g.astype(F32), heads).reshape(1, -1)


def _trunk(x, mem, p, consts):
    b = x.shape[0]
    x2 = x.reshape(b * SEQ, D_MODEL)
    for i in range(DEPTH):
        x2 = _ffn(x2, p["norm_ffn1"][i], *p["ffn1"][i], p["norm_out"][i], False)
        km, vm = _memkv(mem, p["norm_mem"][i], p["w_mem_kv"][i], p["mem_k_norm"][i], consts)
        if i % 2 == 0:
            q, k, v, qm = _mixin_a(x2, p["norm_mix"][i], p["w_in_a"], p["na_q_norm"], p["na_k_norm"],
                                   p["mem_q_norm"][i], consts)
            seq3 = lambda t: t.reshape(b, SEQ, t.shape[-1])
            mixed = _na(seq3(q), seq3(k), seq3(v), p["na_bias"])
            w_out = p["w_out_a"]
        else:
            a, bb, qm = _mixin_b(x2, p["norm_mix"][i], p["w_in_b"], p["mem_q_norm"][i], consts)
            mixed = _seq_dft_real(a.reshape(b, SEQ, FOURIER_DIM), bb.reshape(b, SEQ, FOURIER_DIM), consts)
            w_out = p["w_out_b"]
        x3 = _mixout(x2.reshape(b, SEQ, D_MODEL), mixed, qm.reshape(b, SEQ, MEM_DIM), km, vm,
                     w_out[0], w_out[1])
        x2 = x3.reshape(b * SEQ, D_MODEL)
        x2 = _ffn(x2, p["norm_ffn2"][i], *p["ffn2"][i], p["norm_out"][i], True)
    return x2.reshape(b, SEQ, D_MODEL)


def kernel(x_prompt, x_sample, mem_prompt, mem_sample, norm_ffn1, w_ffn1_in, w_ffn1_out, norm_mix, norm_mem,
           w_mem_kv, mem_q_norm, mem_k_norm, w_in_a, na_q_norm, na_k_norm, na_rpb, w_out_a, w_in_b, w_out_b,
           norm_ffn2, w_ffn2_in, w_ffn2_out, norm_out):
    assert x_prompt.shape[1:] == (SEQ, D_MODEL) and x_sample.shape[1:] == (SEQ, D_MODEL)
    consts = _constants()
    split_out = lambda w: (w[:NA_DIM].astype(BF16), w[NA_DIM:].astype(BF16))
    p = dict(
        norm_ffn1=[_row(norm_ffn1[i]) for i in range(DEPTH)],
        norm_ffn2=[_row(norm_ffn2[i]) for i in range(DEPTH)],
        norm_mix=[_row(norm_mix[i]) for i in range(DEPTH)],
        norm_mem=[_row(norm_mem[i]) for i in range(DEPTH)],
        norm_out=[_row(norm_out[i]) for i in range(DEPTH)],
        ffn1=[(w_ffn1_in[i].astype(BF16), w_ffn1_out[i].astype(BF16)) for i in range(DEPTH)],
        ffn2=[(w_ffn2_in[i].astype(BF16), w_ffn2_out[i].astype(BF16)) for i in range(DEPTH)],
        w_mem_kv=[w_mem_kv[i].astype(BF16) for i in range(DEPTH)],
        mem_q_norm=[_tile_heads(mem_q_norm[i], MEM_HEADS) for i in range(DEPTH)],
        mem_k_norm=[_tile_heads(mem_k_norm[i], MEM_HEADS) for i in range(DEPTH)],
        w_in_a=w_in_a[0].astype(BF16),
        na_q_norm=_tile_heads(na_q_norm[0], NA_HEADS),
        na_k_norm=_tile_heads(na_k_norm[0], NA_HEADS),
        na_bias=_na_bias_table(na_rpb[0]),
        w_out_a=split_out(w_out_a[0]),
        w_in_b=w_in_b[0].astype(BF16),
        w_out_b=split_out(w_out_b[0]),
    )
    return (_trunk(x_prompt, mem_prompt, p, consts), _trunk(x_sample, mem_sample, p, consts))
```

```python
import functools

import numpy as np
import jax
import jax.numpy as jnp
from jax import lax
from jax.experimental import pallas as pl
from jax.experimental.pallas import tpu as pltpu

D_MODEL = 1024
DEPTH = 2
HEAD_DIM = 64
MEM_HEADS = 4
MEM_DIM = MEM_HEADS * HEAD_DIM
N_MEM = 256
NA_HEADS = 12
NA_DIM = NA_HEADS * HEAD_DIM
FOURIER_DIM = 768
GRID_W = 64
KERNEL_ROWS = 8
KERNEL_COLS = 16
D_FF = 2816
EPS = 1e-6
NEG_INF = -1e30
LOG2_E = 1.4426950408889634
SEQ = 8192
ROWS = SEQ // GRID_W
FFT_N1 = 64
FFT_N2 = 128

F32 = jnp.float32
BF16 = jnp.bfloat16

LANES = 128
MXU_WIDTH = 256
BF16_ROWS = 16
FF_CHUNK = MXU_WIDTH
N_FF_CHUNKS = D_FF // FF_CHUNK
SUBTILE_ROWS = 512
SUBTILES = 2
TOKEN_TILE = SUBTILES * SUBTILE_ROWS
LIGHT_SUBTILES = 4
LIGHT_TILE = LIGHT_SUBTILES * SUBTILE_ROWS
VMEM_LIMIT = 56 * 1024 * 1024


def _params(n_axes):
    return pltpu.CompilerParams(dimension_semantics=("arbitrary",) * n_axes,
                                vmem_limit_bytes=VMEM_LIMIT)


def _const_spec(shape):
    n = len(shape)
    return pl.BlockSpec(shape, lambda *_: (0,) * n, pipeline_mode=pl.Buffered(1))


def _rms(x, g):
    ms = jnp.mean(x * x, axis=-1, keepdims=True)
    return x * lax.rsqrt(ms + EPS) * g


def _head_rmsnorm(t, gain, e_blk):
    sq = (t * t).astype(BF16)
    slabs = [jnp.dot(sq[:, c:c + MXU_WIDTH], e_blk, preferred_element_type=F32)
             for c in range(0, t.shape[-1], MXU_WIDTH)]
    ms = slabs[0] if len(slabs) == 1 else jnp.concatenate(slabs, axis=-1)
    return t * lax.rsqrt(ms + EPS) * gain


def _head_lane_mask(shape, head, lane_axis):
    lane = lax.broadcasted_iota(jnp.int32, shape, lane_axis)
    return (lane >= head * HEAD_DIM) & (lane < (head + 1) * HEAD_DIM)


def _ffn_kernel(x_ref, g_ref, win_ref, wout_ref, gout_ref, o_ref, xn_ref, acc_ref, *, final_norm):
    for s in range(SUBTILES):
        rows = slice(s * SUBTILE_ROWS, (s + 1) * SUBTILE_ROWS)
        xn_ref[rows] = _rms(x_ref[rows], g_ref[...]).astype(BF16)
        for c in range(N_FF_CHUNKS):
            cols = slice(c * FF_CHUNK, (c + 1) * FF_CHUNK)
            up_cols = slice(D_FF + c * FF_CHUNK, D_FF + (c + 1) * FF_CHUNK)
            gate = jnp.dot(xn_ref[rows], win_ref[:, cols], preferred_element_type=F32)
            up = jnp.dot(xn_ref[rows], win_ref[:, up_cols], preferred_element_type=F32)
            act = (gate * jax.nn.sigmoid(gate) * up).astype(BF16)
            part = jnp.dot(act, wout_ref[cols, :], preferred_element_type=F32)
            if c == 0:
                acc_ref[rows] = part
            elif c < N_FF_CHUNKS - 1:
                acc_ref[rows] += part
            else:
                y = x_ref[rows] + 0.5 * (acc_ref[rows] + part)
        if final_norm:
            y = _rms(y, gout_ref[...])
        o_ref[rows] = y


def _ffn(x2d, g, win_c, wout_c, gout, final_norm):
    t = x2d.shape[0]
    tm = TOKEN_TILE
    return pl.pallas_call(
        functools.partial(_ffn_kernel, final_norm=final_norm),
        out_shape=jax.ShapeDtypeStruct((t, D_MODEL), F32),
        grid=(t // tm,),
        in_specs=[
            pl.BlockSpec((tm, D_MODEL), lambda i: (i, 0)),
            _const_spec((1, D_MODEL)),
            _const_spec((D_MODEL, 2 * D_FF)),
            _const_spec((D_FF, D_MODEL)),
            _const_spec((1, D_MODEL)),
        ],
        out_specs=pl.BlockSpec((tm, D_MODEL), lambda i: (i, 0)),
        scratch_shapes=[pltpu.VMEM((tm, D_MODEL), BF16), pltpu.VMEM((tm, D_MODEL), F32)],
        compiler_params=_params(1),
        name="ffn_final" if final_norm else "ffn",
    )(x2d, g, win_c, wout_c, gout)


def _mixin_a_kernel(x_ref, g_ref, w_ref, gq_ref, gk_ref, gqm_ref, e_ref, q_ref, k_ref, v_ref, qm_ref):
    scale = HEAD_DIM ** -0.5
    e_blk = e_ref[...]
    for s in range(LIGHT_SUBTILES):
        rows = slice(s * SUBTILE_ROWS, (s + 1) * SUBTILE_ROWS)
        h = _rms(x_ref[rows], g_ref[...]).astype(BF16)
        proj = jnp.dot(h, w_ref[...], preferred_element_type=F32)
        q = proj[:, :NA_DIM]
        k = proj[:, NA_DIM:2 * NA_DIM]
        v = proj[:, 2 * NA_DIM:3 * NA_DIM]
        qm = proj[:, 3 * NA_DIM:]
        q_ref[rows] = (_head_rmsnorm(q, gq_ref[...], e_blk) * (scale * LOG2_E)).astype(BF16)
        k_ref[rows] = _head_rmsnorm(k, gk_ref[...], e_blk).astype(BF16)
        v_ref[rows] = v.astype(BF16)
        qm_ref[rows] = (_head_rmsnorm(qm, gqm_ref[...], e_blk) * scale).astype(BF16)


def _mixin_a(x2d, g, w, gq, gk, gqm, consts):
    t = x2d.shape[0]
    tm = LIGHT_TILE
    width = 3 * NA_DIM + MEM_DIM
    tok = lambda n: pl.BlockSpec((tm, n), lambda i: (i, 0))
    return pl.pallas_call(
        _mixin_a_kernel,
        out_shape=(jax.ShapeDtypeStruct((t, NA_DIM), BF16),) * 3 + (jax.ShapeDtypeStruct((t, MEM_DIM), BF16),),
        grid=(t // tm,),
        in_specs=[
            tok(D_MODEL),
            _const_spec((1, D_MODEL)),
            _const_spec((D_MODEL, width)),
            _const_spec((1, NA_DIM)), _const_spec((1, NA_DIM)), _const_spec((1, MEM_DIM)),
            _const_spec((MXU_WIDTH, MXU_WIDTH)),
        ],
        out_specs=(tok(NA_DIM), tok(NA_DIM), tok(NA_DIM), tok(MEM_DIM)),
        compiler_params=_params(1),
        name="mixin_na",
    )(x2d, g, w, gq, gk, gqm, consts["e_blk"])


def _mixin_b_kernel(x_ref, g_ref, w_ref, gqm_ref, e_ref, dft_ref, a_ref, b_ref, qm_ref):
    scale = HEAD_DIM ** -0.5
    for s in range(LIGHT_SUBTILES):
        rows = slice(s * SUBTILE_ROWS, (s + 1) * SUBTILE_ROWS)
        h = _rms(x_ref[rows], g_ref[...]).astype(BF16)
        proj = jnp.dot(h, w_ref[...], preferred_element_type=F32)
        z = proj[:, :FOURIER_DIM].astype(BF16)
        qm = proj[:, FOURIER_DIM:]
        ab = jnp.dot(z, dft_ref[...], preferred_element_type=F32)
        a_ref[rows] = ab[:, :FOURIER_DIM].astype(BF16)
        b_ref[rows] = ab[:, FOURIER_DIM:].astype(BF16)
        qm_ref[rows] = (_head_rmsnorm(qm, gqm_ref[...], e_ref[...]) * scale).astype(BF16)


def _mixin_b(x2d, g, w, gqm, consts):
    t = x2d.shape[0]
    tm = LIGHT_TILE
    tok = lambda n: pl.BlockSpec((tm, n), lambda i: (i, 0))
    return pl.pallas_call(
        _mixin_b_kernel,
        out_shape=(jax.ShapeDtypeStruct((t, FOURIER_DIM), BF16),) * 2 + (jax.ShapeDtypeStruct((t, MEM_DIM), BF16),),
        grid=(t // tm,),
        in_specs=[
            tok(D_MODEL),
            _const_spec((1, D_MODEL)),
            _const_spec((D_MODEL, FOURIER_DIM + MEM_DIM)),
            _const_spec((1, MEM_DIM)),
            _const_spec((MXU_WIDTH, MXU_WIDTH)),
            _const_spec((FOURIER_DIM, 2 * FOURIER_DIM)),
        ],
        out_specs=(tok(FOURIER_DIM), tok(FOURIER_DIM), tok(MEM_DIM)),
        compiler_params=_params(1),
        name="mixin_fnet",
    )(x2d, g, w, gqm, consts["e_blk"], consts["dft_feat"])


NA_HEADS_PER_STEP = LANES // HEAD_DIM
NA_KEYS = KERNEL_ROWS * GRID_W
NA_TICK_ROWS = 2
NA_TICKS = ROWS // NA_TICK_ROWS
NA_LOOP_TICKS = 62


def _na_window(r):
    if isinstance(r, int):
        rs = min(max(r - KERNEL_ROWS // 2, 0), ROWS - KERNEL_ROWS)
        return rs * GRID_W, rs - r + (KERNEL_ROWS - 1)
    rs = jnp.clip(r - KERNEL_ROWS // 2, 0, ROWS - KERNEL_ROWS)
    return pl.multiple_of(rs * GRID_W, GRID_W), rs - r + (KERNEL_ROWS - 1)


def _na_row_start(r):
    return r * GRID_W if isinstance(r, int) else pl.multiple_of(r * GRID_W, GRID_W)


def _na_kernel(q_ref, k_ref, v_ref, bias_ref, o_ref, s_ref, e_ref, l_ref):
    first_head = _head_lane_mask((GRID_W, LANES), 0, 1)

    def scores(t, slot):
        for j in range(NA_TICK_ROWS):
            r = t * NA_TICK_ROWS + j
            key_start, shift = _na_window(r)
            q = q_ref[pl.ds(_na_row_start(r), GRID_W), :]
            zero = jnp.zeros_like(q)
            q2 = jnp.concatenate([jnp.where(first_head, q, zero), jnp.where(first_head, zero, q)], axis=0)
            kb = k_ref[pl.ds(key_start, NA_KEYS), :]
            s = lax.dot_general(q2, kb, (((1,), (1,)), ((), ())), preferred_element_type=F32)
            s_ref[slot, j] = s + bias_ref[shift]

    def softmax(slot):
        for j in range(NA_TICK_ROWS):
            s = s_ref[slot, j]
            m = jnp.max(s, axis=-1, keepdims=True)
            e = jnp.exp2(s - m)
            l = jnp.sum(e, axis=-1, keepdims=True)
            e_ref[slot, j] = e.astype(BF16)
            l_ref[slot, j] = jnp.broadcast_to(1.0 / l, (LANES, LANES))

    def values(t, slot):
        for j in range(NA_TICK_ROWS):
            r = t * NA_TICK_ROWS + j
            key_start, _ = _na_window(r)
            vb = v_ref[pl.ds(key_start, NA_KEYS), :]
            o2 = jnp.dot(e_ref[slot, j], vb, preferred_element_type=F32) * l_ref[slot, j]
            o = jnp.where(first_head, o2[:GRID_W], o2[GRID_W:])
            o_ref[pl.ds(_na_row_start(r), GRID_W), :] = o.astype(BF16)

    def tick(t, parity, do_scores=True, do_values=True, do_softmax=True):
        if do_values:
            values(t - 2, parity)
        if do_scores:
            scores(t, parity)
        if do_softmax:
            softmax(1 - parity)

    tick(0, 0, do_values=False, do_softmax=False)
    tick(1, 1, do_values=False)

    def body(u, carry):
        for i in range(NA_LOOP_TICKS):
            tick(2 + NA_LOOP_TICKS * u + i, i % 2)
        return carry

    n_iter = (NA_TICKS - 2) // NA_LOOP_TICKS
    lax.fori_loop(0, n_iter, body, 0)
    for t in range(2 + n_iter * NA_LOOP_TICKS, NA_TICKS):
        tick(t, t % 2)
    tick(NA_TICKS, 0, do_scores=False)
    tick(NA_TICKS + 1, 1, do_scores=False, do_softmax=False)


def _na(q3, k3, v3, bias):
    b = q3.shape[0]
    n_pairs = NA_HEADS // NA_HEADS_PER_STEP
    seq_spec = pl.BlockSpec((None, SEQ, LANES), lambda i, j: (i, 0, j))
    return pl.pallas_call(
        _na_kernel,
        out_shape=jax.ShapeDtypeStruct((b, SEQ, NA_DIM), BF16),
        grid=(b, n_pairs),
        in_specs=[seq_spec, seq_spec, seq_spec,
                  pl.BlockSpec((None, KERNEL_ROWS, LANES, NA_KEYS), lambda i, j: (j, 0, 0, 0))],
        out_specs=seq_spec,
        scratch_shapes=[pltpu.VMEM((2, NA_TICK_ROWS, LANES, NA_KEYS), F32),
                        pltpu.VMEM((2, NA_TICK_ROWS, LANES, NA_KEYS), BF16),
                        pltpu.VMEM((2, NA_TICK_ROWS, LANES, LANES), F32)],
        compiler_params=_params(2),
        name="na_attention",
    )(q3, k3, v3, bias)


FFT_G = BF16_ROWS


def _swap_leading(x):
    return pltpu.einshape("abc->bac", x)


def _fft_stage1_kernel(a_ref, b_ref, g_ref, yr_ref, yi_ref, sr_ref, si_ref):
    a_t = _swap_leading(a_ref[...])
    b_t = _swap_leading(b_ref[...])
    for j in range(FFT_G):
        x = jnp.concatenate([a_t[j], b_t[j]], axis=0)
        y = jnp.dot(g_ref[j], x, preferred_element_type=F32)
        sr_ref[j] = y[:FFT_N1].astype(BF16)
        si_ref[j] = y[FFT_N1:].astype(BF16)
    yr_ref[...] = _swap_leading(sr_ref[...])
    yi_ref[...] = _swap_leading(si_ref[...])


def _fft_stage2_kernel(yr_ref, yi_ref, h_ref, f_ref, s_ref):
    for j in range(FFT_G):
        x = jnp.concatenate([yr_ref[j], yi_ref[j]], axis=0)
        s_ref[j] = jnp.dot(h_ref[...], x, preferred_element_type=F32).astype(BF16)
    f_ref[...] = _swap_leading(s_ref[...])


def _seq_dft_real(a3, b3, consts):
    b = a3.shape[0]
    n2_groups = FFT_N2 // FFT_G
    k1_groups = FFT_N1 // FFT_G
    split1 = (b, FFT_N1, n2_groups, FFT_G, FOURIER_DIM)
    blk1 = pl.BlockSpec((None, FFT_N1, None, FFT_G, FOURIER_DIM), lambda i, j: (i, 0, j, 0, 0))
    stage = pltpu.VMEM((FFT_G, FFT_N1, FOURIER_DIM), BF16)
    yr, yi = pl.pallas_call(
        _fft_stage1_kernel,
        out_shape=(jax.ShapeDtypeStruct(split1, BF16),) * 2,
        grid=(b, n2_groups),
        in_specs=[blk1, blk1,
                  pl.BlockSpec((FFT_G, 2 * FFT_N1, 2 * FFT_N1), lambda i, j: (j, 0, 0))],
        out_specs=(blk1, blk1),
        scratch_shapes=[stage, stage],
        compiler_params=_params(2),
        name="fft_stage1",
    )(a3.reshape(split1), b3.reshape(split1), consts["dft_seq1"])
    yr = yr.reshape(b, FFT_N1, FFT_N2, FOURIER_DIM)
    yi = yi.reshape(b, FFT_N1, FFT_N2, FOURIER_DIM)
    blk2 = pl.BlockSpec((None, FFT_G, FFT_N2, FOURIER_DIM), lambda i, j: (i, j, 0, 0))
    f = pl.pallas_call(
        _fft_stage2_kernel,
        out_shape=jax.ShapeDtypeStruct((b, FFT_N2, k1_groups, FFT_G, FOURIER_DIM), BF16),
        grid=(b, k1_groups),
        in_specs=[blk2, blk2, _const_spec((FFT_N2, 2 * FFT_N2))],
        out_specs=pl.BlockSpec((None, FFT_N2, None, FFT_G, FOURIER_DIM), lambda i, j: (i, 0, j, 0, 0)),
        scratch_shapes=[pltpu.VMEM((FFT_G, FFT_N2, FOURIER_DIM), BF16)],
        compiler_params=_params(2),
        name="fft_stage2",
    )(yr, yi, consts["dft_seq2"])
    return f.reshape(b, SEQ, FOURIER_DIM)


def _memkv_kernel(mem_ref, g_ref, w_ref, gk_ref, e_ref, k_ref, v_ref):
    h = _rms(mem_ref[...], g_ref[...]).astype(BF16)
    kv = jnp.dot(h, w_ref[...], preferred_element_type=F32)
    k = _head_rmsnorm(kv[:, :MEM_DIM], gk_ref[...], e_ref[...])
    k_ref[...] = k.T.astype(BF16)
    v_ref[...] = kv[:, MEM_DIM:].astype(BF16)


def _memkv(mem, g, w, gk, consts):
    b = mem.shape[0]
    out_blk = pl.BlockSpec((None, N_MEM, MEM_DIM), lambda i: (i, 0, 0))
    return pl.pallas_call(
        _memkv_kernel,
        out_shape=(jax.ShapeDtypeStruct((b, N_MEM, MEM_DIM), BF16),) * 2,
        grid=(b,),
        in_specs=[
            pl.BlockSpec((None, N_MEM, D_MODEL), lambda i: (i, 0, 0)),
            _const_spec((1, D_MODEL)),
            _const_spec((D_MODEL, 2 * MEM_DIM)),
            _const_spec((1, MEM_DIM)),
            _const_spec((MXU_WIDTH, MXU_WIDTH)),
        ],
        out_specs=(out_blk, out_blk),
        compiler_params=_params(1),
        name="memory_kv",
    )(mem, g, w, gk, consts["e_blk"])


def _mixout_kernel(x_ref, a_ref, qm_ref, km_ref, vm_ref, wa_ref, wm_ref, o_ref):
    km = km_ref[...]
    vm = vm_ref[...]
    for s in range(LIGHT_SUBTILES):
        rows = slice(s * SUBTILE_ROWS, (s + 1) * SUBTILE_ROWS)
        qm = qm_ref[rows]
        zero = jnp.zeros_like(qm)
        mo = jnp.zeros(qm.shape, F32)
        for head in range(MEM_HEADS):
            in_head = _head_lane_mask(qm.shape, head, 1)
            sc = jnp.dot(jnp.where(in_head, qm, zero), km, preferred_element_type=F32)
            e = jnp.exp(sc - jnp.max(sc, axis=-1, keepdims=True))
            l = jnp.sum(e, axis=-1, keepdims=True)
            oh = jnp.dot(e.astype(BF16), vm, preferred_element_type=F32) * (1.0 / l)
            mo = jnp.where(in_head, oh, mo)
        y = jnp.dot(a_ref[rows], wa_ref[...], preferred_element_type=F32)
        y = y + jnp.dot(mo.astype(BF16), wm_ref[...], preferred_element_type=F32)
        o_ref[rows] = x_ref[rows] + y


def _mixout(x3, a3, qm3, km, vm, w_a, w_m):
    b = x3.shape[0]
    tm = LIGHT_TILE
    tok = lambda n: pl.BlockSpec((None, tm, n), lambda i, j: (i, j, 0))
    mem_blk = pl.BlockSpec((None, N_MEM, MEM_DIM), lambda i, j: (i, 0, 0))
    mix = a3.shape[-1]
    return pl.pallas_call(
        _mixout_kernel,
        out_shape=jax.ShapeDtypeStruct(x3.shape, F32),
        grid=(b, SEQ // tm),
        in_specs=[tok(D_MODEL), tok(mix), tok(MEM_DIM), mem_blk, mem_blk,
                  _const_spec((mix, D_MODEL)), _const_spec((MEM_DIM, D_MODEL))],
        out_specs=tok(D_MODEL),
        compiler_params=_params(2),
        name="mixer_out",
    )(x3, a3, qm3, km, vm, w_a, w_m)


def _head_mean_matrix():
    head = np.arange(MXU_WIDTH) // HEAD_DIM
    return jnp.asarray((head[:, None] == head[None, :]).astype(np.float32) / HEAD_DIM, BF16)


def _dft_constants():
    d = np.arange(FOURIER_DIM)
    ang = 2.0 * np.pi * ((d[:, None] * d[None, :]) % FOURIER_DIM) / FOURIER_DIM
    feat = np.concatenate([np.cos(ang), np.sin(ang)], axis=1) / np.sqrt(FOURIER_DIM)
    k1 = np.arange(FFT_N1)
    n1 = np.arange(FFT_N1)
    n2 = np.arange(FFT_N2)
    n = FFT_N2 * n1[None, None, :] + n2[:, None, None]
    ang1 = 2.0 * np.pi * ((k1[None, :, None] * n) % SEQ) / SEQ
    c1, s1 = np.cos(ang1) / np.sqrt(FFT_N1), np.sin(ang1) / np.sqrt(FFT_N1)
    seq1 = np.concatenate([np.concatenate([c1, -s1], axis=2), np.concatenate([s1, c1], axis=2)], axis=1)
    k2 = np.arange(FFT_N2)
    ang2 = 2.0 * np.pi * ((k2[:, None] * n2[None, :]) % FFT_N2) / FFT_N2
    seq2 = np.concatenate([np.cos(ang2), -np.sin(ang2)], axis=1) / np.sqrt(FFT_N2)
    return (jnp.asarray(feat, BF16), jnp.asarray(seq1, BF16), jnp.asarray(seq2, BF16))


def _constants():
    feat, seq1, seq2 = _dft_constants()
    return dict(e_blk=_head_mean_matrix(), dft_feat=feat, dft_seq1=seq1, dft_seq2=seq2)


def _na_bias_table(rpb):
    c = np.arange(GRID_W)
    dc = np.clip(c[None, :] - c[:, None], -(KERNEL_COLS - 1), KERNEL_COLS - 1) + (KERNEL_COLS - 1)
    cs = np.clip(c - KERNEL_COLS // 2, 0, GRID_W - KERNEL_COLS)
    inside = (c[None, :] >= cs[:, None]) & (c[None, :] < cs[:, None] + KERNEL_COLS)
    t = jnp.take(rpb.astype(F32) * LOG2_E, jnp.asarray(dc), axis=2)
    t = jnp.where(jnp.asarray(inside)[None, None], t, NEG_INF)
    t = jnp.stack([t[:, s:s + KERNEL_ROWS] for s in range(KERNEL_ROWS)], axis=1)
    t = t.transpose(0, 1, 3, 2, 4)
    t = t.reshape(NA_HEADS // NA_HEADS_PER_STEP, NA_HEADS_PER_STEP, KERNEL_ROWS, GRID_W, NA_KEYS)
    t = t.transpose(0, 2, 1, 3, 4)
    return t.reshape(NA_HEADS // NA_HEADS_PER_STEP, KERNEL_ROWS, LANES, NA_KEYS)


def _row(v):
    return v.reshape(1, -1).astype(F32)


def _tile_heads(g, heads):
    return jnp.tile(g.astype(F32), heads).reshape(1, -1)


def _trunk(x, mem, p, consts):
    b = x.shape[0]
    x2 = x.reshape(b * SEQ, D_MODEL)
    for i in range(DEPTH):
        x2 = _ffn(x2, p["norm_ffn1"][i], *p["ffn1"][i], p["norm_out"][i], False)
        km, vm = _memkv(mem, p["norm_mem"][i], p["w_mem_kv"][i], p["mem_k_norm"][i], consts)
        if i % 2 == 0:
            q, k, v, qm = _mixin_a(x2, p["norm_mix"][i], p["w_in_a"], p["na_q_norm"], p["na_k_norm"],
                                   p["mem_q_norm"][i], consts)
            seq3 = lambda t: t.reshape(b, SEQ, t.shape[-1])
            mixed = _na(seq3(q), seq3(k), seq3(v), p["na_bias"])
            w_out = p["w_out_a"]
        else:
            a, bb, qm = _mixin_b(x2, p["norm_mix"][i], p["w_in_b"], p["mem_q_norm"][i], consts)
            mixed = _seq_dft_real(a.reshape(b, SEQ, FOURIER_DIM), bb.reshape(b, SEQ, FOURIER_DIM), consts)
            w_out = p["w_out_b"]
        x3 = _mixout(x2.reshape(b, SEQ, D_MODEL), mixed, qm.reshape(b, SEQ, MEM_DIM), km, vm,
                     w_out[0], w_out[1])
        x2 = x3.reshape(b * SEQ, D_MODEL)
        x2 = _ffn(x2, p["norm_ffn2"][i], *p["ffn2"][i], p["norm_out"][i], True)
    return x2.reshape(b, SEQ, D_MODEL)


def kernel(x_prompt, x_sample, mem_prompt, mem_sample, norm_ffn1, w_ffn1_in, w_ffn1_out, norm_mix, norm_mem,
           w_mem_kv, mem_q_norm, mem_k_norm, w_in_a, na_q_norm, na_k_norm, na_rpb, w_out_a, w_in_b, w_out_b,
           norm_ffn2, w_ffn2_in, w_ffn2_out, norm_out):
    assert x_prompt.shape[1:] == (SEQ, D_MODEL) and x_sample.shape[1:] == (SEQ, D_MODEL)
    consts = _constants()
    split_out = lambda w: (w[:NA_DIM].astype(BF16), w[NA_DIM:].astype(BF16))
    p = dict(
        norm_ffn1=[_row(norm_ffn1[i]) for i in range(DEPTH)],
        norm_ffn2=[_row(norm_ffn2[i]) for i in range(DEPTH)],
        norm_mix=[_row(norm_mix[i]) for i in range(DEPTH)],
        norm_mem=[_row(norm_mem[i]) for i in range(DEPTH)],
        norm_out=[_row(norm_out[i]) for i in range(DEPTH)],
        ffn1=[(w_ffn1_in[i].astype(BF16), w_ffn1_out[i].astype(BF16)) for i in range(DEPTH)],
        ffn2=[(w_ffn2_in[i].astype(BF16), w_ffn2_out[i].astype(BF16)) for i in range(DEPTH)],
        w_mem_kv=[w_mem_kv[i].astype(BF16) for i in range(DEPTH)],
        mem_q_norm=[_tile_heads(mem_q_norm[i], MEM_HEADS) for i in range(DEPTH)],
        mem_k_norm=[_tile_heads(mem_k_norm[i], MEM_HEADS) for i in range(DEPTH)],
        w_in_a=w_in_a[0].astype(BF16),
        na_q_norm=_tile_heads(na_q_norm[0], NA_HEADS),
        na_k_norm=_tile_heads(na_k_norm[0], NA_HEADS),
        na_bias=_na_bias_table(na_rpb[0]),
        w_out_a=split_out(w_out_a[0]),
        w_in_b=w_in_b[0].astype(BF16),
        w_out_b=split_out(w_out_b[0]),
    )
    return (_trunk(x_prompt, mem_prompt, p, consts), _trunk(x_sample, mem_sample, p, consts))
```

```python
import functools

import numpy as np
import jax
import jax.numpy as jnp
from jax import lax
from jax.experimental import pallas as pl
from jax.experimental.pallas import tpu as pltpu

D_MODEL = 1024
DEPTH = 2
HEAD_DIM = 64
MEM_HEADS = 4
MEM_DIM = MEM_HEADS * HEAD_DIM
N_MEM = 256
NA_HEADS = 12
NA_DIM = NA_HEADS * HEAD_DIM
FOURIER_DIM = 768
GRID_W = 64
KERNEL_ROWS = 8
KERNEL_COLS = 16
D_FF = 2816
EPS = 1e-6
NEG_INF = -1e30
LOG2_E = 1.4426950408889634
SEQ = 8192
ROWS = SEQ // GRID_W
FFT_N1 = 64
FFT_N2 = 128

F32 = jnp.float32
BF16 = jnp.bfloat16

LANES = 128
MXU_WIDTH = 256
BF16_ROWS = 16
FF_CHUNK = MXU_WIDTH
N_FF_CHUNKS = D_FF // FF_CHUNK
SUBTILE_ROWS = 512
SUBTILES = 2
TOKEN_TILE = SUBTILES * SUBTILE_ROWS
LIGHT_SUBTILES = 4
LIGHT_TILE = LIGHT_SUBTILES * SUBTILE_ROWS
VMEM_LIMIT = 56 * 1024 * 1024


def _params(n_axes):
    return pltpu.CompilerParams(dimension_semantics=("arbitrary",) * n_axes,
                                vmem_limit_bytes=VMEM_LIMIT)


def _const_spec(shape):
    n = len(shape)
    return pl.BlockSpec(shape, lambda *_: (0,) * n, pipeline_mode=pl.Buffered(1))


def _rms(x, g):
    ms = jnp.mean(x * x, axis=-1, keepdims=True)
    return x * lax.rsqrt(ms + EPS) * g


def _head_rmsnorm(t, gain, e_blk):
    sq = (t * t).astype(BF16)
    slabs = [jnp.dot(sq[:, c:c + MXU_WIDTH], e_blk, preferred_element_type=F32)
             for c in range(0, t.shape[-1], MXU_WIDTH)]
    ms = slabs[0] if len(slabs) == 1 else jnp.concatenate(slabs, axis=-1)
    return t * lax.rsqrt(ms + EPS) * gain


def _head_lane_mask(shape, head, lane_axis):
    lane = lax.broadcasted_iota(jnp.int32, shape, lane_axis)
    return (lane >= head * HEAD_DIM) & (lane < (head + 1) * HEAD_DIM)


def _ffn_kernel(x_ref, g_ref, win_ref, wout_ref, gout_ref, o_ref, xn_ref, acc_ref, *, final_norm):
    for s in range(SUBTILES):
        rows = slice(s * SUBTILE_ROWS, (s + 1) * SUBTILE_ROWS)
        xn_ref[rows] = _rms(x_ref[rows], g_ref[...]).astype(BF16)
        for c in range(N_FF_CHUNKS):
            cols = slice(c * FF_CHUNK, (c + 1) * FF_CHUNK)
            up_cols = slice(D_FF + c * FF_CHUNK, D_FF + (c + 1) * FF_CHUNK)
            gate = jnp.dot(xn_ref[rows], win_ref[:, cols], preferred_element_type=F32)
            up = jnp.dot(xn_ref[rows], win_ref[:, up_cols], preferred_element_type=F32)
            act = (gate * jax.nn.sigmoid(gate) * up).astype(BF16)
            part = jnp.dot(act, wout_ref[cols, :], preferred_element_type=F32)
            if c == 0:
                acc_ref[rows] = part
            elif c < N_FF_CHUNKS - 1:
                acc_ref[rows] += part
            else:
                y = x_ref[rows] + 0.5 * (acc_ref[rows] + part)
        if final_norm:
            y = _rms(y, gout_ref[...])
        o_ref[rows] = y


def _ffn(x2d, g, win_c, wout_c, gout, final_norm):
    t = x2d.shape[0]
    tm = TOKEN_TILE
    return pl.pallas_call(
        functools.partial(_ffn_kernel, final_norm=final_norm),
        out_shape=jax.ShapeDtypeStruct((t, D_MODEL), F32),
        grid=(t // tm,),
        in_specs=[
            pl.BlockSpec((tm, D_MODEL), lambda i: (i, 0)),
            _const_spec((1, D_MODEL)),
            _const_spec((D_MODEL, 2 * D_FF)),
            _const_spec((D_FF, D_MODEL)),
            _const_spec((1, D_MODEL)),
        ],
        out_specs=pl.BlockSpec((tm, D_MODEL), lambda i: (i, 0)),
        scratch_shapes=[pltpu.VMEM((tm, D_MODEL), BF16), pltpu.VMEM((tm, D_MODEL), F32)],
        compiler_params=_params(1),
        name="ffn_final" if final_norm else "ffn",
    )(x2d, g, win_c, wout_c, gout)


def _mixin_a_kernel(x_ref, g_ref, w_ref, gq_ref, gk_ref, gqm_ref, e_ref, q_ref, k_ref, v_ref, qm_ref):
    scale = HEAD_DIM ** -0.5
    e_blk = e_ref[...]
    for s in range(LIGHT_SUBTILES):
        rows = slice(s * SUBTILE_ROWS, (s + 1) * SUBTILE_ROWS)
        h = _rms(x_ref[rows], g_ref[...]).astype(BF16)
        proj = jnp.dot(h, w_ref[...], preferred_element_type=F32)
        q = proj[:, :NA_DIM]
        k = proj[:, NA_DIM:2 * NA_DIM]
        v = proj[:, 2 * NA_DIM:3 * NA_DIM]
        qm = proj[:, 3 * NA_DIM:]
        q_ref[rows] = (_head_rmsnorm(q, gq_ref[...], e_blk) * (scale * LOG2_E)).astype(BF16)
        k_ref[rows] = _head_rmsnorm(k, gk_ref[...], e_blk).astype(BF16)
        v_ref[rows] = v.astype(BF16)
        qm_ref[rows] = (_head_rmsnorm(qm, gqm_ref[...], e_blk) * scale).astype(BF16)


def _mixin_a(x2d, g, w, gq, gk, gqm, consts):
    t = x2d.shape[0]
    tm = LIGHT_TILE
    width = 3 * NA_DIM + MEM_DIM
    tok = lambda n: pl.BlockSpec((tm, n), lambda i: (i, 0))
    return pl.pallas_call(
        _mixin_a_kernel,
        out_shape=(jax.ShapeDtypeStruct((t, NA_DIM), BF16),) * 3 + (jax.ShapeDtypeStruct((t, MEM_DIM), BF16),),
        grid=(t // tm,),
        in_specs=[
            tok(D_MODEL),
            _const_spec((1, D_MODEL)),
            _const_spec((D_MODEL, width)),
            _const_spec((1, NA_DIM)), _const_spec((1, NA_DIM)), _const_spec((1, MEM_DIM)),
            _const_spec((MXU_WIDTH, MXU_WIDTH)),
        ],
        out_specs=(tok(NA_DIM), tok(NA_DIM), tok(NA_DIM), tok(MEM_DIM)),
        compiler_params=_params(1),
        name="mixin_na",
    )(x2d, g, w, gq, gk, gqm, consts["e_blk"])


def _mixin_b_kernel(x_ref, g_ref, w_ref, gqm_ref, e_ref, dft_ref, a_ref, b_ref, qm_ref):
    scale = HEAD_DIM ** -0.5
    for s in range(LIGHT_SUBTILES):
        rows = slice(s * SUBTILE_ROWS, (s + 1) * SUBTILE_ROWS)
        h = _rms(x_ref[rows], g_ref[...]).astype(BF16)
        proj = jnp.dot(h, w_ref[...], preferred_element_type=F32)
        z = proj[:, :FOURIER_DIM].astype(BF16)
        qm = proj[:, FOURIER_DIM:]
        ab = jnp.dot(z, dft_ref[...], preferred_element_type=F32)
        a_ref[rows] = ab[:, :FOURIER_DIM].astype(BF16)
        b_ref[rows] = ab[:, FOURIER_DIM:].astype(BF16)
        qm_ref[rows] = (_head_rmsnorm(qm, gqm_ref[...], e_ref[...]) * scale).astype(BF16)


def _mixin_b(x2d, g, w, gqm, consts):
    t = x2d.shape[0]
    tm = LIGHT_TILE
    tok = lambda n: pl.BlockSpec((tm, n), lambda i: (i, 0))
    return pl.pallas_call(
        _mixin_b_kernel,
        out_shape=(jax.ShapeDtypeStruct((t, FOURIER_DIM), BF16),) * 2 + (jax.ShapeDtypeStruct((t, MEM_DIM), BF16),),
        grid=(t // tm,),
        in_specs=[
            tok(D_MODEL),
            _const_spec((1, D_MODEL)),
            _const_spec((D_MODEL, FOURIER_DIM + MEM_DIM)),
            _const_spec((1, MEM_DIM)),
            _const_spec((MXU_WIDTH, MXU_WIDTH)),
            _const_spec((FOURIER_DIM, 2 * FOURIER_DIM)),
        ],
        out_specs=(tok(FOURIER_DIM), tok(FOURIER_DIM), tok(MEM_DIM)),
        compiler_params=_params(1),
        name="mixin_fnet",
    )(x2d, g, w, gqm, consts["e_blk"], consts["dft_feat"])


NA_HEADS_PER_STEP = LANES // HEAD_DIM
NA_KEYS = KERNEL_ROWS * GRID_W
NA_BIAS_ROWS = 2 * KERNEL_ROWS - 2
NA_TICK_ROWS = 2
NA_TICKS = ROWS // NA_TICK_ROWS
NA_LOOP_TICKS = 62


def _na_window(r):
    if isinstance(r, int):
        rs = min(max(r - KERNEL_ROWS // 2, 0), ROWS - KERNEL_ROWS)
        return rs * GRID_W, rs - r + (KERNEL_ROWS - 1)
    rs = jnp.clip(r - KERNEL_ROWS // 2, 0, ROWS - KERNEL_ROWS)
    return pl.multiple_of(rs * GRID_W, GRID_W), rs - r + (KERNEL_ROWS - 1)


def _na_row_start(r):
    return r * GRID_W if isinstance(r, int) else pl.multiple_of(r * GRID_W, GRID_W)


def _na_kernel(q_ref, k_ref, v_ref, bias_ref, o_ref, s_ref, e_ref, l_ref):
    first_head = _head_lane_mask((GRID_W, LANES), 0, 1)

    def scores(t, slot):
        for j in range(NA_TICK_ROWS):
            r = t * NA_TICK_ROWS + j
            key_start, shift = _na_window(r)
            q = q_ref[pl.ds(_na_row_start(r), GRID_W), :]
            zero = jnp.zeros_like(q)
            q2 = jnp.concatenate([jnp.where(first_head, q, zero), jnp.where(first_head, zero, q)], axis=0)
            kb = k_ref[pl.ds(key_start, NA_KEYS), :]
            s = lax.dot_general(q2, kb, (((1,), (1,)), ((), ())), preferred_element_type=F32)
            bias = jnp.concatenate([bias_ref[shift + 2 * i] for i in range(KERNEL_ROWS // 2)], axis=1)
            s_ref[slot, j] = s + bias

    def softmax(slot):
        for j in range(NA_TICK_ROWS):
            s = s_ref[slot, j]
            m = jnp.max(s, axis=-1, keepdims=True)
            e = jnp.exp2(s - m)
            l = jnp.sum(e, axis=-1, keepdims=True)
            e_ref[slot, j] = e.astype(BF16)
            l_ref[slot, j] = jnp.broadcast_to(1.0 / l, (LANES, LANES))

    def values(t, slot):
        for j in range(NA_TICK_ROWS):
            r = t * NA_TICK_ROWS + j
            key_start, _ = _na_window(r)
            vb = v_ref[pl.ds(key_start, NA_KEYS), :]
            o2 = jnp.dot(e_ref[slot, j], vb, preferred_element_type=F32) * l_ref[slot, j]
            o = jnp.where(first_head, o2[:GRID_W], o2[GRID_W:])
            o_ref[pl.ds(_na_row_start(r), GRID_W), :] = o.astype(BF16)

    def tick(t, parity, do_scores=True, do_values=True, do_softmax=True):
        if do_values:
            values(t - 2, parity)
        if do_scores:
            scores(t, parity)
        if do_softmax:
            softmax(1 - parity)

    tick(0, 0, do_values=False, do_softmax=False)
    tick(1, 1, do_values=False)

    def body(u, carry):
        for i in range(NA_LOOP_TICKS):
            tick(2 + NA_LOOP_TICKS * u + i, i % 2)
        return carry

    n_iter = (NA_TICKS - 2) // NA_LOOP_TICKS
    lax.fori_loop(0, n_iter, body, 0)
    for t in range(2 + n_iter * NA_LOOP_TICKS, NA_TICKS):
        tick(t, t % 2)
    tick(NA_TICKS, 0, do_scores=False)
    tick(NA_TICKS + 1, 1, do_scores=False, do_softmax=False)


def _na(q3, k3, v3, bias):
    b = q3.shape[0]
    n_pairs = NA_HEADS // NA_HEADS_PER_STEP
    seq_spec = pl.BlockSpec((None, SEQ, LANES), lambda i, j: (i, 0, j))
    return pl.pallas_call(
        _na_kernel,
        out_shape=jax.ShapeDtypeStruct((b, SEQ, NA_DIM), BF16),
        grid=(b, n_pairs),
        in_specs=[seq_spec, seq_spec, seq_spec,
                  pl.BlockSpec((None, NA_BIAS_ROWS, LANES, LANES), lambda i, j: (j, 0, 0, 0))],
        out_specs=seq_spec,
        scratch_shapes=[pltpu.VMEM((2, NA_TICK_ROWS, LANES, NA_KEYS), F32),
                        pltpu.VMEM((2, NA_TICK_ROWS, LANES, NA_KEYS), BF16),
                        pltpu.VMEM((2, NA_TICK_ROWS, LANES, LANES), F32)],
        compiler_params=_params(2),
        name="na_attention",
    )(q3, k3, v3, bias)


FFT_G = BF16_ROWS


def _swap_leading(x):
    return pltpu.einshape("abc->bac", x)


def _fft_stage1_kernel(a_ref, b_ref, g_ref, yr_ref, yi_ref, sr_ref, si_ref):
    a_t = _swap_leading(a_ref[...])
    b_t = _swap_leading(b_ref[...])
    for j in range(FFT_G):
        x = jnp.concatenate([a_t[j], b_t[j]], axis=0)
        y = jnp.dot(g_ref[j], x, preferred_element_type=F32)
        sr_ref[j] = y[:FFT_N1].astype(BF16)
        si_ref[j] = y[FFT_N1:].astype(BF16)
    yr_ref[...] = _swap_leading(sr_ref[...])
    yi_ref[...] = _swap_leading(si_ref[...])


def _fft_stage2_kernel(yr_ref, yi_ref, h_ref, f_ref, s_ref):
    for j in range(FFT_G):
        x = jnp.concatenate([yr_ref[j], yi_ref[j]], axis=0)
        s_ref[j] = jnp.dot(h_ref[...], x, preferred_element_type=F32).astype(BF16)
    f_ref[...] = _swap_leading(s_ref[...])


def _seq_dft_real(a3, b3, consts):
    b = a3.shape[0]
    n2_groups = FFT_N2 // FFT_G
    k1_groups = FFT_N1 // FFT_G
    split1 = (b, FFT_N1, n2_groups, FFT_G, FOURIER_DIM)
    blk1 = pl.BlockSpec((None, FFT_N1, None, FFT_G, FOURIER_DIM), lambda i, j: (i, 0, j, 0, 0))
    stage = pltpu.VMEM((FFT_G, FFT_N1, FOURIER_DIM), BF16)
    yr, yi = pl.pallas_call(
        _fft_stage1_kernel,
        out_shape=(jax.ShapeDtypeStruct(split1, BF16),) * 2,
        grid=(b, n2_groups),
        in_specs=[blk1, blk1,
                  pl.BlockSpec((FFT_G, 2 * FFT_N1, 2 * FFT_N1), lambda i, j: (j, 0, 0))],
        out_specs=(blk1, blk1),
        scratch_shapes=[stage, stage],
        compiler_params=_params(2),
        name="fft_stage1",
    )(a3.reshape(split1), b3.reshape(split1), consts["dft_seq1"])
    yr = yr.reshape(b, FFT_N1, FFT_N2, FOURIER_DIM)
    yi = yi.reshape(b, FFT_N1, FFT_N2, FOURIER_DIM)
    blk2 = pl.BlockSpec((None, FFT_G, FFT_N2, FOURIER_DIM), lambda i, j: (i, j, 0, 0))
    f = pl.pallas_call(
        _fft_stage2_kernel,
        out_shape=jax.ShapeDtypeStruct((b, FFT_N2, k1_groups, FFT_G, FOURIER_DIM), BF16),
        grid=(b, k1_groups),
        in_specs=[blk2, blk2, _const_spec((FFT_N2, 2 * FFT_N2))],
        out_specs=pl.BlockSpec((None, FFT_N2, None, FFT_G, FOURIER_DIM), lambda i, j: (i, 0, j, 0, 0)),
        scratch_shapes=[pltpu.VMEM((FFT_G, FFT_N2, FOURIER_DIM), BF16)],
        compiler_params=_params(2),
        name="fft_stage2",
    )(yr, yi, consts["dft_seq2"])
    return f.reshape(b, SEQ, FOURIER_DIM)


def _memkv_kernel(mem_ref, g_ref, w_ref, gk_ref, e_ref, k_ref, v_ref):
    h = _rms(mem_ref[...], g_ref[...]).astype(BF16)
    kv = jnp.dot(h, w_ref[...], preferred_element_type=F32)
    k = _head_rmsnorm(kv[:, :MEM_DIM], gk_ref[...], e_ref[...])
    k_ref[...] = k.T.astype(BF16)
    v_ref[...] = kv[:, MEM_DIM:].astype(BF16)


def _memkv(mem, g, w, gk, consts):
    b = mem.shape[0]
    out_blk = pl.BlockSpec((None, N_MEM, MEM_DIM), lambda i: (i, 0, 0))
    return pl.pallas_call(
        _memkv_kernel,
        out_shape=(jax.ShapeDtypeStruct((b, N_MEM, MEM_DIM), BF16),) * 2,
        grid=(b,),
        in_specs=[
            pl.BlockSpec((None, N_MEM, D_MODEL), lambda i: (i, 0, 0)),
            _const_spec((1, D_MODEL)),
            _const_spec((D_MODEL, 2 * MEM_DIM)),
            _const_spec((1, MEM_DIM)),
            _const_spec((MXU_WIDTH, MXU_WIDTH)),
        ],
        out_specs=(out_blk, out_blk),
        compiler_params=_params(1),
        name="memory_kv",
    )(mem, g, w, gk, consts["e_blk"])


def _mixout_kernel(x_ref, a_ref, qm_ref, km_ref, vm_ref, w_ref, o_ref):
    km = km_ref[...]
    vm = vm_ref[...]
    for s in range(LIGHT_SUBTILES):
        rows = slice(s * SUBTILE_ROWS, (s + 1) * SUBTILE_ROWS)
        qm = qm_ref[rows]
        zero = jnp.zeros_like(qm)
        mo = jnp.zeros(qm.shape, F32)
        for head in range(MEM_HEADS):
            in_head = _head_lane_mask(qm.shape, head, 1)
            sc = jnp.dot(jnp.where(in_head, qm, zero), km, preferred_element_type=F32)
            e = jnp.exp(sc - jnp.max(sc, axis=-1, keepdims=True))
            l = jnp.sum(e, axis=-1, keepdims=True)
            oh = jnp.dot(e.astype(BF16), vm, preferred_element_type=F32) * (1.0 / l)
            mo = jnp.where(in_head, oh, mo)
        mix = a_ref.shape[-1]
        y = jnp.dot(a_ref[rows], w_ref[:mix, :], preferred_element_type=F32)
        y = y + jnp.dot(mo.astype(BF16), w_ref[mix:, :], preferred_element_type=F32)
        o_ref[rows] = x_ref[rows] + y


def _mixout(x3, a3, qm3, km, vm, w_out):
    b = x3.shape[0]
    tm = LIGHT_TILE
    tok = lambda n: pl.BlockSpec((None, tm, n), lambda i, j: (i, j, 0))
    mem_blk = pl.BlockSpec((None, N_MEM, MEM_DIM), lambda i, j: (i, 0, 0))
    mix = a3.shape[-1]
    return pl.pallas_call(
        _mixout_kernel,
        out_shape=jax.ShapeDtypeStruct(x3.shape, F32),
        grid=(b, SEQ // tm),
        in_specs=[tok(D_MODEL), tok(mix), tok(MEM_DIM), mem_blk, mem_blk,
                  _const_spec((D_MODEL, D_MODEL))],
        out_specs=tok(D_MODEL),
        compiler_params=_params(2),
        name="mixer_out",
    )(x3, a3, qm3, km, vm, w_out)


def _head_mean_matrix():
    head = np.arange(MXU_WIDTH) // HEAD_DIM
    return jnp.asarray((head[:, None] == head[None, :]).astype(np.float32) / HEAD_DIM, BF16)


def _dft_constants():
    d = np.arange(FOURIER_DIM)
    ang = 2.0 * np.pi * ((d[:, None] * d[None, :]) % FOURIER_DIM) / FOURIER_DIM
    feat = np.concatenate([np.cos(ang), np.sin(ang)], axis=1) / np.sqrt(FOURIER_DIM)
    k1 = np.arange(FFT_N1)
    n1 = np.arange(FFT_N1)
    n2 = np.arange(FFT_N2)
    n = FFT_N2 * n1[None, None, :] + n2[:, None, None]
    ang1 = 2.0 * np.pi * ((k1[None, :, None] * n) % SEQ) / SEQ
    c1, s1 = np.cos(ang1) / np.sqrt(FFT_N1), np.sin(ang1) / np.sqrt(FFT_N1)
    seq1 = np.concatenate([np.concatenate([c1, -s1], axis=2), np.concatenate([s1, c1], axis=2)], axis=1)
    k2 = np.arange(FFT_N2)
    ang2 = 2.0 * np.pi * ((k2[:, None] * n2[None, :]) % FFT_N2) / FFT_N2
    seq2 = np.concatenate([np.cos(ang2), -np.sin(ang2)], axis=1) / np.sqrt(FFT_N2)
    return (jnp.asarray(feat, BF16), jnp.asarray(seq1, BF16), jnp.asarray(seq2, BF16))


def _constants():
    feat, seq1, seq2 = _dft_constants()
    return dict(e_blk=_head_mean_matrix(), dft_feat=feat, dft_seq1=seq1, dft_seq2=seq2)


def _na_bias_table(rpb):
    c = np.arange(GRID_W)
    dc = np.clip(c[None, :] - c[:, None], -(KERNEL_COLS - 1), KERNEL_COLS - 1) + (KERNEL_COLS - 1)
    cs = np.clip(c - KERNEL_COLS // 2, 0, GRID_W - KERNEL_COLS)
    inside = (c[None, :] >= cs[:, None]) & (c[None, :] < cs[:, None] + KERNEL_COLS)
    t = jnp.take(rpb.astype(F32) * LOG2_E, jnp.asarray(dc), axis=2)
    t = jnp.where(jnp.asarray(inside)[None, None], t, NEG_INF)
    t = jnp.concatenate([t[:, :-1], t[:, 1:]], axis=-1)
    n_pairs = NA_HEADS // NA_HEADS_PER_STEP
    t = t.reshape(n_pairs, NA_HEADS_PER_STEP, NA_BIAS_ROWS, GRID_W, LANES).transpose(0, 2, 1, 3, 4)
    return t.reshape(n_pairs, NA_BIAS_ROWS, LANES, LANES)


def _row(v):
    return v.reshape(1, -1).astype(F32)


def _tile_heads(g, heads):
    return jnp.tile(g.astype(F32), heads).reshape(1, -1)


def _trunk(x, mem, p, consts):
    b = x.shape[0]
    x2 = x.reshape(b * SEQ, D_MODEL)
    for i in range(DEPTH):
        x2 = _ffn(x2, p["norm_ffn1"][i], *p["ffn1"][i], p["norm_out"][i], False)
        km, vm = _memkv(mem, p["norm_mem"][i], p["w_mem_kv"][i], p["mem_k_norm"][i], consts)
        if i % 2 == 0:
            q, k, v, qm = _mixin_a(x2, p["norm_mix"][i], p["w_in_a"], p["na_q_norm"], p["na_k_norm"],
                                   p["mem_q_norm"][i], consts)
            seq3 = lambda t: t.reshape(b, SEQ, t.shape[-1])
            mixed = _na(seq3(q), seq3(k), seq3(v), p["na_bias"])
            w_out = p["w_out_a"]
        else:
            a, bb, qm = _mixin_b(x2, p["norm_mix"][i], p["w_in_b"], p["mem_q_norm"][i], consts)
            mixed = _seq_dft_real(a.reshape(b, SEQ, FOURIER_DIM), bb.reshape(b, SEQ, FOURIER_DIM), consts)
            w_out = p["w_out_b"]
        x3 = _mixout(x2.reshape(b, SEQ, D_MODEL), mixed, qm.reshape(b, SEQ, MEM_DIM), km, vm, w_out)
        x2 = x3.reshape(b * SEQ, D_MODEL)
        x2 = _ffn(x2, p["norm_ffn2"][i], *p["ffn2"][i], p["norm_out"][i], True)
    return x2.reshape(b, SEQ, D_MODEL)


def kernel(x_prompt, x_sample, mem_prompt, mem_sample, norm_ffn1, w_ffn1_in, w_ffn1_out, norm_mix, norm_mem,
           w_mem_kv, mem_q_norm, mem_k_norm, w_in_a, na_q_norm, na_k_norm, na_rpb, w_out_a, w_in_b, w_out_b,
           norm_ffn2, w_ffn2_in, w_ffn2_out, norm_out):
    assert x_prompt.shape[1:] == (SEQ, D_MODEL) and x_sample.shape[1:] == (SEQ, D_MODEL)
    consts = _constants()
    p = dict(
        norm_ffn1=[_row(norm_ffn1[i]) for i in range(DEPTH)],
        norm_ffn2=[_row(norm_ffn2[i]) for i in range(DEPTH)],
        norm_mix=[_row(norm_mix[i]) for i in range(DEPTH)],
        norm_mem=[_row(norm_mem[i]) for i in range(DEPTH)],
        norm_out=[_row(norm_out[i]) for i in range(DEPTH)],
        ffn1=[(w_ffn1_in[i].astype(BF16), w_ffn1_out[i].astype(BF16)) for i in range(DEPTH)],
        ffn2=[(w_ffn2_in[i].astype(BF16), w_ffn2_out[i].astype(BF16)) for i in range(DEPTH)],
        w_mem_kv=[w_mem_kv[i].astype(BF16) for i in range(DEPTH)],
        mem_q_norm=[_tile_heads(mem_q_norm[i], MEM_HEADS) for i in range(DEPTH)],
        mem_k_norm=[_tile_heads(mem_k_norm[i], MEM_HEADS) for i in range(DEPTH)],
        w_in_a=w_in_a[0].astype(BF16),
        na_q_norm=_tile_heads(na_q_norm[0], NA_HEADS),
        na_k_norm=_tile_heads(na_k_norm[0], NA_HEADS),
        na_bias=_na_bias_table(na_rpb[0]),
        w_out_a=w_out_a[0].astype(BF16),
        w_in_b=w_in_b[0].astype(BF16),
        w_out_b=w_out_b[0].astype(BF16),
    )
    return (_trunk(x_prompt, mem_prompt, p, consts), _trunk(x_sample, mem_sample, p, consts))
```

```python
import functools

import numpy as np
import jax
import jax.numpy as jnp
from jax import lax
from jax.experimental import pallas as pl
from jax.experimental.pallas import tpu as pltpu

D_MODEL = 1024
DEPTH = 2
HEAD_DIM = 64
MEM_HEADS = 4
MEM_DIM = MEM_HEADS * HEAD_DIM
N_MEM = 256
NA_HEADS = 12
NA_DIM = NA_HEADS * HEAD_DIM
FOURIER_DIM = 768
GRID_W = 64
KERNEL_ROWS = 8
KERNEL_COLS = 16
D_FF = 2816
EPS = 1e-6
NEG_INF = -1e30
LOG2_E = 1.4426950408889634
SEQ = 8192
ROWS = SEQ // GRID_W
FFT_N1 = 64
FFT_N2 = 128

F32 = jnp.float32
BF16 = jnp.bfloat16

LANES = 128
MXU_WIDTH = 256
BF16_ROWS = 16
FF_CHUNK = MXU_WIDTH
N_FF_CHUNKS = D_FF // FF_CHUNK
SUBTILE_ROWS = 512
SUBTILES = 2
TOKEN_TILE = SUBTILES * SUBTILE_ROWS
LIGHT_SUBTILES = 4
LIGHT_TILE = LIGHT_SUBTILES * SUBTILE_ROWS
VMEM_LIMIT = 56 * 1024 * 1024


def _params(n_axes):
    return pltpu.CompilerParams(dimension_semantics=("arbitrary",) * n_axes,
                                vmem_limit_bytes=VMEM_LIMIT)


def _const_spec(shape):
    n = len(shape)
    return pl.BlockSpec(shape, lambda *_: (0,) * n, pipeline_mode=pl.Buffered(1))


def _layer_spec(shape, layer):
    n = len(shape)
    return pl.BlockSpec((None,) + tuple(shape), lambda *_: (layer,) + (0,) * n, pipeline_mode=pl.Buffered(1))


def _rms(x, g):
    ms = jnp.mean(x * x, axis=-1, keepdims=True)
    return x * lax.rsqrt(ms + EPS) * g


def _head_rmsnorm(t, gain, e_blk):
    sq = (t * t).astype(BF16)
    slabs = [jnp.dot(sq[:, c:c + MXU_WIDTH], e_blk, preferred_element_type=F32)
             for c in range(0, t.shape[-1], MXU_WIDTH)]
    ms = slabs[0] if len(slabs) == 1 else jnp.concatenate(slabs, axis=-1)
    return t * lax.rsqrt(ms + EPS) * gain


def _head_lane_mask(shape, head, lane_axis):
    lane = lax.broadcasted_iota(jnp.int32, shape, lane_axis)
    return (lane >= head * HEAD_DIM) & (lane < (head + 1) * HEAD_DIM)


def _ffn_kernel(x_ref, g_ref, win_ref, wout_ref, gout_ref, o_ref, xn_ref, acc_ref, *, final_norm):
    for s in range(SUBTILES):
        rows = slice(s * SUBTILE_ROWS, (s + 1) * SUBTILE_ROWS)
        xn_ref[rows] = _rms(x_ref[rows], g_ref[...]).astype(BF16)
        for c in range(N_FF_CHUNKS):
            cols = slice(c * FF_CHUNK, (c + 1) * FF_CHUNK)
            up_cols = slice(D_FF + c * FF_CHUNK, D_FF + (c + 1) * FF_CHUNK)
            gate = jnp.dot(xn_ref[rows], win_ref[:, cols], preferred_element_type=F32)
            up = jnp.dot(xn_ref[rows], win_ref[:, up_cols], preferred_element_type=F32)
            act = (gate * jax.nn.sigmoid(gate) * up).astype(BF16)
            part = jnp.dot(act, wout_ref[cols, :], preferred_element_type=F32)
            if c == 0:
                acc_ref[rows] = part
            elif c < N_FF_CHUNKS - 1:
                acc_ref[rows] += part
            else:
                y = x_ref[rows] + 0.5 * (acc_ref[rows] + part)
        if final_norm:
            y = _rms(y, gout_ref[...])
        o_ref[rows] = y


def _ffn(x2d, g, w_in, w_out, layer, gout, final_norm):
    t = x2d.shape[0]
    tm = TOKEN_TILE
    return pl.pallas_call(
        functools.partial(_ffn_kernel, final_norm=final_norm),
        out_shape=jax.ShapeDtypeStruct((t, D_MODEL), F32),
        grid=(t // tm,),
        in_specs=[
            pl.BlockSpec((tm, D_MODEL), lambda i: (i, 0)),
            _const_spec((1, D_MODEL)),
            _layer_spec((D_MODEL, 2 * D_FF), layer),
            _layer_spec((D_FF, D_MODEL), layer),
            _const_spec((1, D_MODEL)),
        ],
        out_specs=pl.BlockSpec((tm, D_MODEL), lambda i: (i, 0)),
        scratch_shapes=[pltpu.VMEM((tm, D_MODEL), BF16), pltpu.VMEM((tm, D_MODEL), F32)],
        compiler_params=_params(1),
        name="ffn_final" if final_norm else "ffn",
    )(x2d, g, w_in, w_out, gout)


def _mixin_a_kernel(x_ref, g_ref, w_ref, gq_ref, gk_ref, gqm_ref, e_ref, q_ref, k_ref, v_ref, qm_ref):
    scale = HEAD_DIM ** -0.5
    e_blk = e_ref[...]
    for s in range(LIGHT_SUBTILES):
        rows = slice(s * SUBTILE_ROWS, (s + 1) * SUBTILE_ROWS)
        h = _rms(x_ref[rows], g_ref[...]).astype(BF16)
        proj = jnp.dot(h, w_ref[...], preferred_element_type=F32)
        q = proj[:, :NA_DIM]
        k = proj[:, NA_DIM:2 * NA_DIM]
        v = proj[:, 2 * NA_DIM:3 * NA_DIM]
        qm = proj[:, 3 * NA_DIM:]
        q_ref[rows] = (_head_rmsnorm(q, gq_ref[...], e_blk) * (scale * LOG2_E)).astype(BF16)
        k_ref[rows] = _head_rmsnorm(k, gk_ref[...], e_blk).astype(BF16)
        v_ref[rows] = v.astype(BF16)
        qm_ref[rows] = (_head_rmsnorm(qm, gqm_ref[...], e_blk) * scale).astype(BF16)


def _mixin_a(x2d, g, w, gq, gk, gqm, consts):
    t = x2d.shape[0]
    tm = LIGHT_TILE
    width = 3 * NA_DIM + MEM_DIM
    tok = lambda n: pl.BlockSpec((tm, n), lambda i: (i, 0))
    return pl.pallas_call(
        _mixin_a_kernel,
        out_shape=(jax.ShapeDtypeStruct((t, NA_DIM), BF16),) * 3 + (jax.ShapeDtypeStruct((t, MEM_DIM), BF16),),
        grid=(t // tm,),
        in_specs=[
            tok(D_MODEL),
            _const_spec((1, D_MODEL)),
            _const_spec((D_MODEL, width)),
            _const_spec((1, NA_DIM)), _const_spec((1, NA_DIM)), _const_spec((1, MEM_DIM)),
            _const_spec((MXU_WIDTH, MXU_WIDTH)),
        ],
        out_specs=(tok(NA_DIM), tok(NA_DIM), tok(NA_DIM), tok(MEM_DIM)),
        compiler_params=_params(1),
        name="mixin_na",
    )(x2d, g, w, gq, gk, gqm, consts["e_blk"])


def _mixin_b_kernel(x_ref, g_ref, w_ref, gqm_ref, e_ref, dft_ref, a_ref, b_ref, qm_ref):
    scale = HEAD_DIM ** -0.5
    for s in range(LIGHT_SUBTILES):
        rows = slice(s * SUBTILE_ROWS, (s + 1) * SUBTILE_ROWS)
        h = _rms(x_ref[rows], g_ref[...]).astype(BF16)
        proj = jnp.dot(h, w_ref[...], preferred_element_type=F32)
        z = proj[:, :FOURIER_DIM].astype(BF16)
        qm = proj[:, FOURIER_DIM:]
        ab = jnp.dot(z, dft_ref[...], preferred_element_type=F32)
        a_ref[rows] = ab[:, :FOURIER_DIM].astype(BF16)
        b_ref[rows] = ab[:, FOURIER_DIM:].astype(BF16)
        qm_ref[rows] = (_head_rmsnorm(qm, gqm_ref[...], e_ref[...]) * scale).astype(BF16)


def _mixin_b(x2d, g, w, gqm, consts):
    t = x2d.shape[0]
    tm = LIGHT_TILE
    tok = lambda n: pl.BlockSpec((tm, n), lambda i: (i, 0))
    return pl.pallas_call(
        _mixin_b_kernel,
        out_shape=(jax.ShapeDtypeStruct((t, FOURIER_DIM), BF16),) * 2 + (jax.ShapeDtypeStruct((t, MEM_DIM), BF16),),
        grid=(t // tm,),
        in_specs=[
            tok(D_MODEL),
            _const_spec((1, D_MODEL)),
            _const_spec((D_MODEL, FOURIER_DIM + MEM_DIM)),
            _const_spec((1, MEM_DIM)),
            _const_spec((MXU_WIDTH, MXU_WIDTH)),
            _const_spec((FOURIER_DIM, 2 * FOURIER_DIM)),
        ],
        out_specs=(tok(FOURIER_DIM), tok(FOURIER_DIM), tok(MEM_DIM)),
        compiler_params=_params(1),
        name="mixin_fnet",
    )(x2d, g, w, gqm, consts["e_blk"], consts["dft_feat"])


NA_HEADS_PER_STEP = LANES // HEAD_DIM
NA_KEYS = KERNEL_ROWS * GRID_W
NA_BIAS_ROWS = 2 * KERNEL_ROWS - 2
NA_TICK_ROWS = 2
NA_TICKS = ROWS // NA_TICK_ROWS
NA_LOOP_TICKS = 62


def _na_window(r):
    if isinstance(r, int):
        rs = min(max(r - KERNEL_ROWS // 2, 0), ROWS - KERNEL_ROWS)
        return rs * GRID_W, rs - r + (KERNEL_ROWS - 1)
    rs = jnp.clip(r - KERNEL_ROWS // 2, 0, ROWS - KERNEL_ROWS)
    return pl.multiple_of(rs * GRID_W, GRID_W), rs - r + (KERNEL_ROWS - 1)


def _na_row_start(r):
    return r * GRID_W if isinstance(r, int) else pl.multiple_of(r * GRID_W, GRID_W)


def _na_kernel(q_ref, k_ref, v_ref, bias_ref, o_ref, s_ref, e_ref, l_ref):
    first_head = _head_lane_mask((GRID_W, LANES), 0, 1)

    def scores(t, slot):
        for j in range(NA_TICK_ROWS):
            r = t * NA_TICK_ROWS + j
            key_start, shift = _na_window(r)
            q = q_ref[pl.ds(_na_row_start(r), GRID_W), :]
            zero = jnp.zeros_like(q)
            q2 = jnp.concatenate([jnp.where(first_head, q, zero), jnp.where(first_head, zero, q)], axis=0)
            kb = k_ref[pl.ds(key_start, NA_KEYS), :]
            s = lax.dot_general(q2, kb, (((1,), (1,)), ((), ())), preferred_element_type=F32)
            bias = jnp.concatenate([bias_ref[shift + 2 * i] for i in range(KERNEL_ROWS // 2)], axis=1)
            s_ref[slot, j] = s + bias

    def softmax(slot):
        for j in range(NA_TICK_ROWS):
            s = s_ref[slot, j]
            m = jnp.max(s, axis=-1, keepdims=True)
            e = jnp.exp2(s - m)
            l = jnp.sum(e, axis=-1, keepdims=True)
            e_ref[slot, j] = e.astype(BF16)
            l_ref[slot, j] = jnp.broadcast_to(1.0 / l, (LANES, LANES))

    def values(t, slot):
        for j in range(NA_TICK_ROWS):
            r = t * NA_TICK_ROWS + j
            key_start, _ = _na_window(r)
            vb = v_ref[pl.ds(key_start, NA_KEYS), :]
            o2 = jnp.dot(e_ref[slot, j], vb, preferred_element_type=F32) * l_ref[slot, j]
            o = jnp.where(first_head, o2[:GRID_W], o2[GRID_W:])
            o_ref[pl.ds(_na_row_start(r), GRID_W), :] = o.astype(BF16)

    def tick(t, parity, do_scores=True, do_values=True, do_softmax=True):
        if do_values:
            values(t - 2, parity)
        if do_scores:
            scores(t, parity)
        if do_softmax:
            softmax(1 - parity)

    tick(0, 0, do_values=False, do_softmax=False)
    tick(1, 1, do_values=False)

    def body(u, carry):
        for i in range(NA_LOOP_TICKS):
            tick(2 + NA_LOOP_TICKS * u + i, i % 2)
        return carry

    n_iter = (NA_TICKS - 2) // NA_LOOP_TICKS
    lax.fori_loop(0, n_iter, body, 0)
    for t in range(2 + n_iter * NA_LOOP_TICKS, NA_TICKS):
        tick(t, t % 2)
    tick(NA_TICKS, 0, do_scores=False)
    tick(NA_TICKS + 1, 1, do_scores=False, do_softmax=False)


def _na(q3, k3, v3, bias):
    b = q3.shape[0]
    n_pairs = NA_HEADS // NA_HEADS_PER_STEP
    seq_spec = pl.BlockSpec((None, SEQ, LANES), lambda i, j: (i, 0, j))
    return pl.pallas_call(
        _na_kernel,
        out_shape=jax.ShapeDtypeStruct((b, SEQ, NA_DIM), BF16),
        grid=(b, n_pairs),
        in_specs=[seq_spec, seq_spec, seq_spec,
                  pl.BlockSpec((None, NA_BIAS_ROWS, LANES, LANES), lambda i, j: (j, 0, 0, 0))],
        out_specs=seq_spec,
        scratch_shapes=[pltpu.VMEM((2, NA_TICK_ROWS, LANES, NA_KEYS), F32),
                        pltpu.VMEM((2, NA_TICK_ROWS, LANES, NA_KEYS), BF16),
                        pltpu.VMEM((2, NA_TICK_ROWS, LANES, LANES), F32)],
        compiler_params=_params(2),
        name="na_attention",
    )(q3, k3, v3, bias)


FFT_COLS = MXU_WIDTH


def _swap_leading(x):
    return pltpu.einshape("abc->bac", x)


def _seq_dft_kernel(a_ref, b_ref, g_ref, h_ref, f_ref, t1_ref, t2_ref):
    cols = a_ref.shape[-1]
    t1_ref[:, :FFT_N1, :] = _swap_leading(a_ref[...].reshape(FFT_N1, FFT_N2, cols))
    t1_ref[:, FFT_N1:, :] = _swap_leading(b_ref[...].reshape(FFT_N1, FFT_N2, cols))

    for j in range(FFT_N2):
        t2_ref[j] = jnp.dot(g_ref[j], t1_ref[j], preferred_element_type=F32).astype(BF16)
    t1_ref[...] = _swap_leading(t2_ref[...])

    for k1 in range(FFT_N1):
        x = jnp.concatenate([t1_ref[k1], t1_ref[FFT_N1 + k1]], axis=0)
        t2_ref[k1] = jnp.dot(h_ref[...], x, preferred_element_type=F32).astype(BF16)
    f_ref[...] = _swap_leading(t2_ref[:FFT_N1]).reshape(SEQ, cols)


def _seq_dft_real(a3, b3, consts):
    b = a3.shape[0]
    slab = pl.BlockSpec((None, SEQ, FFT_COLS), lambda i, j: (i, 0, j))
    work = pltpu.VMEM((FFT_N2, 2 * FFT_N1, FFT_COLS), BF16)
    return pl.pallas_call(
        _seq_dft_kernel,
        out_shape=jax.ShapeDtypeStruct((b, SEQ, FOURIER_DIM), BF16),
        grid=(b, FOURIER_DIM // FFT_COLS),
        in_specs=[slab, slab, _const_spec((FFT_N2, 2 * FFT_N1, 2 * FFT_N1)), _const_spec((FFT_N2, 2 * FFT_N2))],
        out_specs=slab,
        scratch_shapes=[work, work],
        compiler_params=_params(2),
        name="seq_dft",
    )(a3, b3, consts["dft_seq1"], consts["dft_seq2"])


def _memkv_kernel(mem_ref, g_ref, w_ref, gk_ref, e_ref, k_ref, v_ref):
    h = _rms(mem_ref[...], g_ref[...]).astype(BF16)
    kv = jnp.dot(h, w_ref[...], preferred_element_type=F32)
    k = _head_rmsnorm(kv[:, :MEM_DIM], gk_ref[...], e_ref[...])
    k_ref[...] = k.T.astype(BF16)
    v_ref[...] = kv[:, MEM_DIM:].astype(BF16)


def _memkv(mem, g, w, gk, consts):
    b = mem.shape[0]
    out_blk = pl.BlockSpec((None, N_MEM, MEM_DIM), lambda i: (i, 0, 0))
    return pl.pallas_call(
        _memkv_kernel,
        out_shape=(jax.ShapeDtypeStruct((b, N_MEM, MEM_DIM), BF16),) * 2,
        grid=(b,),
        in_specs=[
            pl.BlockSpec((None, N_MEM, D_MODEL), lambda i: (i, 0, 0)),
            _const_spec((1, D_MODEL)),
            _const_spec((D_MODEL, 2 * MEM_DIM)),
            _const_spec((1, MEM_DIM)),
            _const_spec((MXU_WIDTH, MXU_WIDTH)),
        ],
        out_specs=(out_blk, out_blk),
        compiler_params=_params(1),
        name="memory_kv",
    )(mem, g, w, gk, consts["e_blk"])


def _mixout_kernel(x_ref, a_ref, qm_ref, km_ref, vm_ref, w_ref, o_ref):
    km = km_ref[...]
    vm = vm_ref[...]
    for s in range(LIGHT_SUBTILES):
        rows = slice(s * SUBTILE_ROWS, (s + 1) * SUBTILE_ROWS)
        qm = qm_ref[rows]
        zero = jnp.zeros_like(qm)
        mo = jnp.zeros(qm.shape, F32)
        for head in range(MEM_HEADS):
            in_head = _head_lane_mask(qm.shape, head, 1)
            sc = jnp.dot(jnp.where(in_head, qm, zero), km, preferred_element_type=F32)
            e = jnp.exp(sc - jnp.max(sc, axis=-1, keepdims=True))
            l = jnp.sum(e, axis=-1, keepdims=True)
            oh = jnp.dot(e.astype(BF16), vm, preferred_element_type=F32) * (1.0 / l)
            mo = jnp.where(in_head, oh, mo)
        mix = a_ref.shape[-1]
        y = jnp.dot(a_ref[rows], w_ref[:mix, :], preferred_element_type=F32)
        y = y + jnp.dot(mo.astype(BF16), w_ref[mix:, :], preferred_element_type=F32)
        o_ref[rows] = x_ref[rows] + y


def _mixout(x3, a3, qm3, km, vm, w_out):
    b = x3.shape[0]
    tm = LIGHT_TILE
    tok = lambda n: pl.BlockSpec((None, tm, n), lambda i, j: (i, j, 0))
    mem_blk = pl.BlockSpec((None, N_MEM, MEM_DIM), lambda i, j: (i, 0, 0))
    mix = a3.shape[-1]
    return pl.pallas_call(
        _mixout_kernel,
        out_shape=jax.ShapeDtypeStruct(x3.shape, F32),
        grid=(b, SEQ // tm),
        in_specs=[tok(D_MODEL), tok(mix), tok(MEM_DIM), mem_blk, mem_blk,
                  _const_spec((D_MODEL, D_MODEL))],
        out_specs=tok(D_MODEL),
        compiler_params=_params(2),
        name="mixer_out",
    )(x3, a3, qm3, km, vm, w_out)


def _head_mean_matrix():
    head = np.arange(MXU_WIDTH) // HEAD_DIM
    return jnp.asarray((head[:, None] == head[None, :]).astype(np.float32) / HEAD_DIM, BF16)


def _dft_constants():
    d = np.arange(FOURIER_DIM)
    ang = 2.0 * np.pi * ((d[:, None] * d[None, :]) % FOURIER_DIM) / FOURIER_DIM
    feat = np.concatenate([np.cos(ang), np.sin(ang)], axis=1) / np.sqrt(FOURIER_DIM)
    k1 = np.arange(FFT_N1)
    n1 = np.arange(FFT_N1)
    n2 = np.arange(FFT_N2)
    n = FFT_N2 * n1[None, None, :] + n2[:, None, None]
    ang1 = 2.0 * np.pi * ((k1[None, :, None] * n) % SEQ) / SEQ
    c1, s1 = np.cos(ang1) / np.sqrt(FFT_N1), np.sin(ang1) / np.sqrt(FFT_N1)
    seq1 = np.concatenate([np.concatenate([c1, -s1], axis=2), np.concatenate([s1, c1], axis=2)], axis=1)
    k2 = np.arange(FFT_N2)
    ang2 = 2.0 * np.pi * ((k2[:, None] * n2[None, :]) % FFT_N2) / FFT_N2
    seq2 = np.concatenate([np.cos(ang2), -np.sin(ang2)], axis=1) / np.sqrt(FFT_N2)
    return (jnp.asarray(feat, BF16), jnp.asarray(seq1, BF16), jnp.asarray(seq2, BF16))


def _constants():
    feat, seq1, seq2 = _dft_constants()
    return dict(e_blk=_head_mean_matrix(), dft_feat=feat, dft_seq1=seq1, dft_seq2=seq2)


def _na_bias_table(rpb):
    c = np.arange(GRID_W)
    cs = np.clip(c - KERNEL_COLS // 2, 0, GRID_W - KERNEL_COLS)
    inside = (c[None, :] >= cs[:, None]) & (c[None, :] < cs[:, None] + KERNEL_COLS)
    r = rpb.astype(F32) * LOG2_E
    pad = GRID_W - KERNEL_COLS
    r = jnp.concatenate([jnp.repeat(r[..., :1], pad, axis=-1), r, jnp.repeat(r[..., -1:], pad, axis=-1)], axis=-1)
    t = jnp.stack([r[..., GRID_W - 1 - q:2 * GRID_W - 1 - q] for q in range(GRID_W)], axis=-2)
    t = jnp.where(jnp.asarray(inside)[None, None], t, NEG_INF)
    t = jnp.concatenate([t[:, :-1], t[:, 1:]], axis=-1)
    n_pairs = NA_HEADS // NA_HEADS_PER_STEP
    t = t.reshape(n_pairs, NA_HEADS_PER_STEP, NA_BIAS_ROWS, GRID_W, LANES).transpose(0, 2, 1, 3, 4)
    return t.reshape(n_pairs, NA_BIAS_ROWS, LANES, LANES)


def _row(v):
    return v.reshape(1, -1).astype(F32)


def _tile_heads(g, heads):
    return jnp.tile(g.astype(F32), heads).reshape(1, -1)


def _trunk(x, mem, p, consts):
    b = x.shape[0]
    x2 = x.reshape(b * SEQ, D_MODEL)
    for i in range(DEPTH):
        x2 = _ffn(x2, p["norm_ffn1"][i], *p["ffn1"], i, p["norm_out"][i], False)
        km, vm = _memkv(mem, p["norm_mem"][i], p["w_mem_kv"][i], p["mem_k_norm"][i], consts)
        if i % 2 == 0:
            q, k, v, qm = _mixin_a(x2, p["norm_mix"][i], p["w_in_a"], p["na_q_norm"], p["na_k_norm"],
                                   p["mem_q_norm"][i], consts)
            seq3 = lambda t: t.reshape(b, SEQ, t.shape[-1])
            mixed = _na(seq3(q), seq3(k), seq3(v), p["na_bias"])
            w_out = p["w_out_a"]
        else:
            a, bb, qm = _mixin_b(x2, p["norm_mix"][i], p["w_in_b"], p["mem_q_norm"][i], consts)
            mixed = _seq_dft_real(a.reshape(b, SEQ, FOURIER_DIM), bb.reshape(b, SEQ, FOURIER_DIM), consts)
            w_out = p["w_out_b"]
        x3 = _mixout(x2.reshape(b, SEQ, D_MODEL), mixed, qm.reshape(b, SEQ, MEM_DIM), km, vm, w_out)
        x2 = x3.reshape(b * SEQ, D_MODEL)
        x2 = _ffn(x2, p["norm_ffn2"][i], *p["ffn2"], i, p["norm_out"][i], True)
    return x2.reshape(b, SEQ, D_MODEL)


def kernel(x_prompt, x_sample, mem_prompt, mem_sample, norm_ffn1, w_ffn1_in, w_ffn1_out, norm_mix, norm_mem,
           w_mem_kv, mem_q_norm, mem_k_norm, w_in_a, na_q_norm, na_k_norm, na_rpb, w_out_a, w_in_b, w_out_b,
           norm_ffn2, w_ffn2_in, w_ffn2_out, norm_out):
    assert x_prompt.shape[1:] == (SEQ, D_MODEL) and x_sample.shape[1:] == (SEQ, D_MODEL)
    consts = _constants()
    p = dict(
        norm_ffn1=[_row(norm_ffn1[i]) for i in range(DEPTH)],
        norm_ffn2=[_row(norm_ffn2[i]) for i in range(DEPTH)],
        norm_mix=[_row(norm_mix[i]) for i in range(DEPTH)],
        norm_mem=[_row(norm_mem[i]) for i in range(DEPTH)],
        norm_out=[_row(norm_out[i]) for i in range(DEPTH)],
        ffn1=(w_ffn1_in.astype(BF16), w_ffn1_out.astype(BF16)),
        ffn2=(w_ffn2_in.astype(BF16), w_ffn2_out.astype(BF16)),
        w_mem_kv=[w_mem_kv[i].astype(BF16) for i in range(DEPTH)],
        mem_q_norm=[_tile_heads(mem_q_norm[i], MEM_HEADS) for i in range(DEPTH)],
        mem_k_norm=[_tile_heads(mem_k_norm[i], MEM_HEADS) for i in range(DEPTH)],
        w_in_a=w_in_a[0].astype(BF16),
        na_q_norm=_tile_heads(na_q_norm[0], NA_HEADS),
        na_k_norm=_tile_heads(na_k_norm[0], NA_HEADS),
        na_bias=_na_bias_table(na_rpb[0]),
        w_out_a=w_out_a[0].astype(BF16),
        w_in_b=w_in_b[0].astype(BF16),
        w_out_b=w_out_b[0].astype(BF16),
    )
    return (_trunk(x_prompt, mem_prompt, p, consts), _trunk(x_sample, mem_sample, p, consts))
```

```python
import functools

import numpy as np
import jax
import jax.numpy as jnp
from jax import lax
from jax.experimental import pallas as pl
from jax.experimental.pallas import tpu as pltpu

D_MODEL = 1024
DEPTH = 2
HEAD_DIM = 64
MEM_HEADS = 4
MEM_DIM = MEM_HEADS * HEAD_DIM
N_MEM = 256
NA_HEADS = 12
NA_DIM = NA_HEADS * HEAD_DIM
FOURIER_DIM = 768
GRID_W = 64
KERNEL_ROWS = 8
KERNEL_COLS = 16
D_FF = 2816
EPS = 1e-6
NEG_INF = -1e30
LOG2_E = 1.4426950408889634
SEQ = 8192
ROWS = SEQ // GRID_W
FFT_N1 = 64
FFT_N2 = 128

F32 = jnp.float32
BF16 = jnp.bfloat16

LANES = 128
MXU_WIDTH = 256
FFT_COLS = MXU_WIDTH
FFT_SLABS = FOURIER_DIM // FFT_COLS
FF_CHUNK = MXU_WIDTH
N_FF_CHUNKS = D_FF // FF_CHUNK
SUBTILE_ROWS = 512
SUBTILES = 2
TOKEN_TILE = SUBTILES * SUBTILE_ROWS
LIGHT_SUBTILES = 4
LIGHT_TILE = LIGHT_SUBTILES * SUBTILE_ROWS
VMEM_LIMIT = 56 * 1024 * 1024


def _params(n_axes):
    return pltpu.CompilerParams(dimension_semantics=("arbitrary",) * n_axes,
                                vmem_limit_bytes=VMEM_LIMIT)


def _const_spec(shape):
    n = len(shape)
    return pl.BlockSpec(shape, lambda *_: (0,) * n, pipeline_mode=pl.Buffered(1))


def _layer_spec(shape, layer):
    n = len(shape)
    return pl.BlockSpec((None,) + tuple(shape), lambda *_: (layer,) + (0,) * n, pipeline_mode=pl.Buffered(1))


def _rms(x, g):
    ms = jnp.mean(x * x, axis=-1, keepdims=True)
    return x * lax.rsqrt(ms + EPS) * g


def _head_rmsnorm(t, gain, e_blk):
    sq = (t * t).astype(BF16)
    slabs = [jnp.dot(sq[:, c:c + MXU_WIDTH], e_blk, preferred_element_type=F32)
             for c in range(0, t.shape[-1], MXU_WIDTH)]
    ms = slabs[0] if len(slabs) == 1 else jnp.concatenate(slabs, axis=-1)
    return t * lax.rsqrt(ms + EPS) * gain


def _head_lane_mask(shape, head, lane_axis):
    lane = lax.broadcasted_iota(jnp.int32, shape, lane_axis)
    return (lane >= head * HEAD_DIM) & (lane < (head + 1) * HEAD_DIM)


def _ffn_kernel(x_ref, g_ref, win_ref, wout_ref, gout_ref, o_ref, xn_ref, acc_ref, *, final_norm):
    for s in range(SUBTILES):
        rows = slice(s * SUBTILE_ROWS, (s + 1) * SUBTILE_ROWS)
        xn_ref[rows] = _rms(x_ref[rows], g_ref[...]).astype(BF16)
        for c in range(N_FF_CHUNKS):
            cols = slice(c * FF_CHUNK, (c + 1) * FF_CHUNK)
            up_cols = slice(D_FF + c * FF_CHUNK, D_FF + (c + 1) * FF_CHUNK)
            gate = jnp.dot(xn_ref[rows], win_ref[:, cols], preferred_element_type=F32)
            up = jnp.dot(xn_ref[rows], win_ref[:, up_cols], preferred_element_type=F32)
            act = (gate * jax.nn.sigmoid(gate) * up).astype(BF16)
            part = jnp.dot(act, wout_ref[cols, :], preferred_element_type=F32)
            if c == 0:
                acc_ref[rows] = part
            elif c < N_FF_CHUNKS - 1:
                acc_ref[rows] += part
            else:
                y = x_ref[rows] + 0.5 * (acc_ref[rows] + part)
        if final_norm:
            y = _rms(y, gout_ref[...])
        o_ref[rows] = y


def _ffn(x2d, g, w_in, w_out, layer, gout, final_norm):
    t = x2d.shape[0]
    tm = TOKEN_TILE
    return pl.pallas_call(
        functools.partial(_ffn_kernel, final_norm=final_norm),
        out_shape=jax.ShapeDtypeStruct((t, D_MODEL), F32),
        grid=(t // tm,),
        in_specs=[
            pl.BlockSpec((tm, D_MODEL), lambda i: (i, 0)),
            _const_spec((1, D_MODEL)),
            _layer_spec((D_MODEL, 2 * D_FF), layer),
            _layer_spec((D_FF, D_MODEL), layer),
            _const_spec((1, D_MODEL)),
        ],
        out_specs=pl.BlockSpec((tm, D_MODEL), lambda i: (i, 0)),
        scratch_shapes=[pltpu.VMEM((tm, D_MODEL), BF16), pltpu.VMEM((tm, D_MODEL), F32)],
        compiler_params=_params(1),
        name="ffn_final" if final_norm else "ffn",
    )(x2d, g, w_in, w_out, gout)


def _mixin_a_kernel(x_ref, g_ref, w_ref, gq_ref, gk_ref, gqm_ref, e_ref, q_ref, k_ref, v_ref, qm_ref):
    scale = HEAD_DIM ** -0.5
    e_blk = e_ref[...]
    for s in range(LIGHT_SUBTILES):
        rows = slice(s * SUBTILE_ROWS, (s + 1) * SUBTILE_ROWS)
        h = _rms(x_ref[rows], g_ref[...]).astype(BF16)
        proj = jnp.dot(h, w_ref[...], preferred_element_type=F32)
        q = proj[:, :NA_DIM]
        k = proj[:, NA_DIM:2 * NA_DIM]
        v = proj[:, 2 * NA_DIM:3 * NA_DIM]
        qm = proj[:, 3 * NA_DIM:]
        q_ref[rows] = (_head_rmsnorm(q, gq_ref[...], e_blk) * (scale * LOG2_E)).astype(BF16)
        k_ref[rows] = _head_rmsnorm(k, gk_ref[...], e_blk).astype(BF16)
        v_ref[rows] = v.astype(BF16)
        qm_ref[rows] = (_head_rmsnorm(qm, gqm_ref[...], e_blk) * scale).astype(BF16)


def _mixin_a(x2d, g, w, gq, gk, gqm, consts):
    t = x2d.shape[0]
    tm = LIGHT_TILE
    width = 3 * NA_DIM + MEM_DIM
    tok = lambda n: pl.BlockSpec((tm, n), lambda i: (i, 0))
    return pl.pallas_call(
        _mixin_a_kernel,
        out_shape=(jax.ShapeDtypeStruct((t, NA_DIM), BF16),) * 3 + (jax.ShapeDtypeStruct((t, MEM_DIM), BF16),),
        grid=(t // tm,),
        in_specs=[
            tok(D_MODEL),
            _const_spec((1, D_MODEL)),
            _const_spec((D_MODEL, width)),
            _const_spec((1, NA_DIM)), _const_spec((1, NA_DIM)), _const_spec((1, MEM_DIM)),
            _const_spec((MXU_WIDTH, MXU_WIDTH)),
        ],
        out_specs=(tok(NA_DIM), tok(NA_DIM), tok(NA_DIM), tok(MEM_DIM)),
        compiler_params=_params(1),
        name="mixin_na",
    )(x2d, g, w, gq, gk, gqm, consts["e_blk"])


def _mixin_b_kernel(x_ref, g_ref, w_ref, gqm_ref, e_ref, dft_ref, a_ref, b_ref, qm_ref):
    scale = HEAD_DIM ** -0.5
    for s in range(LIGHT_SUBTILES):
        rows = slice(s * SUBTILE_ROWS, (s + 1) * SUBTILE_ROWS)
        h = _rms(x_ref[rows], g_ref[...]).astype(BF16)
        proj = jnp.dot(h, w_ref[...], preferred_element_type=F32)
        z = proj[:, :FOURIER_DIM].astype(BF16)
        qm = proj[:, FOURIER_DIM:]
        ab = jnp.dot(z, dft_ref[...], preferred_element_type=F32)
        for c in range(FFT_SLABS):
            a_ref[c, rows] = ab[:, c * FFT_COLS:(c + 1) * FFT_COLS].astype(BF16)
            b_ref[c, rows] = ab[:, FOURIER_DIM + c * FFT_COLS:FOURIER_DIM + (c + 1) * FFT_COLS].astype(BF16)
        qm_ref[rows] = (_head_rmsnorm(qm, gqm_ref[...], e_ref[...]) * scale).astype(BF16)


def _mixin_b(x2d, g, w, gqm, consts):
    t = x2d.shape[0]
    tm = LIGHT_TILE
    tok = lambda n: pl.BlockSpec((tm, n), lambda i: (i, 0))
    slab = pl.BlockSpec((FFT_SLABS, tm, FFT_COLS), lambda i: (0, i, 0))
    return pl.pallas_call(
        _mixin_b_kernel,
        out_shape=(jax.ShapeDtypeStruct((FFT_SLABS, t, FFT_COLS), BF16),) * 2
        + (jax.ShapeDtypeStruct((t, MEM_DIM), BF16),),
        grid=(t // tm,),
        in_specs=[
            tok(D_MODEL),
            _const_spec((1, D_MODEL)),
            _const_spec((D_MODEL, FOURIER_DIM + MEM_DIM)),
            _const_spec((1, MEM_DIM)),
            _const_spec((MXU_WIDTH, MXU_WIDTH)),
            _const_spec((FOURIER_DIM, 2 * FOURIER_DIM)),
        ],
        out_specs=(slab, slab, tok(MEM_DIM)),
        compiler_params=_params(1),
        name="mixin_fnet",
    )(x2d, g, w, gqm, consts["e_blk"], consts["dft_feat"])


NA_HEADS_PER_STEP = LANES // HEAD_DIM
NA_KEYS = KERNEL_ROWS * GRID_W
NA_BIAS_ROWS = 2 * KERNEL_ROWS - 2
NA_TICK_ROWS = 2
NA_TICKS = ROWS // NA_TICK_ROWS
NA_LOOP_TICKS = 62


def _na_window(r):
    if isinstance(r, int):
        rs = min(max(r - KERNEL_ROWS // 2, 0), ROWS - KERNEL_ROWS)
        return rs * GRID_W, rs - r + (KERNEL_ROWS - 1)
    rs = jnp.clip(r - KERNEL_ROWS // 2, 0, ROWS - KERNEL_ROWS)
    return pl.multiple_of(rs * GRID_W, GRID_W), rs - r + (KERNEL_ROWS - 1)


def _na_row_start(r):
    return r * GRID_W if isinstance(r, int) else pl.multiple_of(r * GRID_W, GRID_W)


def _na_kernel(q_ref, k_ref, v_ref, bias_ref, o_ref, s_ref, e_ref, l_ref):
    first_head = _head_lane_mask((GRID_W, LANES), 0, 1)

    def scores(t, slot):
        for j in range(NA_TICK_ROWS):
            r = t * NA_TICK_ROWS + j
            key_start, shift = _na_window(r)
            q = q_ref[pl.ds(_na_row_start(r), GRID_W), :]
            zero = jnp.zeros_like(q)
            q2 = jnp.concatenate([jnp.where(first_head, q, zero), jnp.where(first_head, zero, q)], axis=0)
            kb = k_ref[pl.ds(key_start, NA_KEYS), :]
            s = lax.dot_general(q2, kb, (((1,), (1,)), ((), ())), preferred_element_type=F32)
            bias = jnp.concatenate([bias_ref[shift + 2 * i] for i in range(KERNEL_ROWS // 2)], axis=1)
            s_ref[slot, j] = s + bias

    def softmax(slot):
        for j in range(NA_TICK_ROWS):
            s = s_ref[slot, j]
            m = jnp.max(s, axis=-1, keepdims=True)
            e = jnp.exp2(s - m)
            l = jnp.sum(e, axis=-1, keepdims=True)
            e_ref[slot, j] = e.astype(BF16)
            l_ref[slot, j] = jnp.broadcast_to(1.0 / l, (LANES, LANES))

    def values(t, slot):
        for j in range(NA_TICK_ROWS):
            r = t * NA_TICK_ROWS + j
            key_start, _ = _na_window(r)
            vb = v_ref[pl.ds(key_start, NA_KEYS), :]
            o2 = jnp.dot(e_ref[slot, j], vb, preferred_element_type=F32) * l_ref[slot, j]
            o = jnp.where(first_head, o2[:GRID_W], o2[GRID_W:])
            o_ref[pl.ds(_na_row_start(r), GRID_W), :] = o.astype(BF16)

    def tick(t, parity, do_scores=True, do_values=True, do_softmax=True):
        if do_values:
            values(t - 2, parity)
        if do_scores:
            scores(t, parity)
        if do_softmax:
            softmax(1 - parity)

    tick(0, 0, do_values=False, do_softmax=False)
    tick(1, 1, do_values=False)

    def body(u, carry):
        for i in range(NA_LOOP_TICKS):
            tick(2 + NA_LOOP_TICKS * u + i, i % 2)
        return carry

    n_iter = (NA_TICKS - 2) // NA_LOOP_TICKS
    lax.fori_loop(0, n_iter, body, 0)
    for t in range(2 + n_iter * NA_LOOP_TICKS, NA_TICKS):
        tick(t, t % 2)
    tick(NA_TICKS, 0, do_scores=False)
    tick(NA_TICKS + 1, 1, do_scores=False, do_softmax=False)


def _na(q3, k3, v3, bias):
    b = q3.shape[0]
    n_pairs = NA_HEADS // NA_HEADS_PER_STEP
    seq_spec = pl.BlockSpec((None, SEQ, LANES), lambda i, j: (i, 0, j))
    return pl.pallas_call(
        _na_kernel,
        out_shape=jax.ShapeDtypeStruct((b, SEQ, NA_DIM), BF16),
        grid=(b, n_pairs),
        in_specs=[seq_spec, seq_spec, seq_spec,
                  pl.BlockSpec((None, NA_BIAS_ROWS, LANES, LANES), lambda i, j: (j, 0, 0, 0))],
        out_specs=seq_spec,
        scratch_shapes=[pltpu.VMEM((2, NA_TICK_ROWS, LANES, NA_KEYS), F32),
                        pltpu.VMEM((2, NA_TICK_ROWS, LANES, NA_KEYS), BF16),
                        pltpu.VMEM((2, NA_TICK_ROWS, LANES, LANES), F32)],
        compiler_params=_params(2),
        name="na_attention",
    )(q3, k3, v3, bias)


def _swap_leading(x):
    return pltpu.einshape("abc->bac", x)


def _seq_dft_kernel(a_ref, b_ref, g_ref, h_ref, f_ref, t1_ref, t2_ref):
    cols = a_ref.shape[-1]
    t1_ref[:, :FFT_N1, :] = _swap_leading(a_ref[...].reshape(FFT_N1, FFT_N2, cols))
    t1_ref[:, FFT_N1:, :] = _swap_leading(b_ref[...].reshape(FFT_N1, FFT_N2, cols))

    for j in range(FFT_N2):
        t2_ref[j] = jnp.dot(g_ref[j], t1_ref[j], preferred_element_type=F32).astype(BF16)
    t1_ref[...] = _swap_leading(t2_ref[...])

    for k1 in range(FFT_N1):
        x = jnp.concatenate([t1_ref[k1], t1_ref[FFT_N1 + k1]], axis=0)
        t2_ref[k1] = jnp.dot(h_ref[...], x, preferred_element_type=F32).astype(BF16)
    f_ref[...] = _swap_leading(t2_ref[:FFT_N1]).reshape(SEQ, cols)


def _seq_dft_real(a3, b3, consts):
    b = a3.shape[1]
    slab = pl.BlockSpec((None, None, SEQ, FFT_COLS), lambda i, j: (j, i, 0, 0))
    work = pltpu.VMEM((FFT_N2, 2 * FFT_N1, FFT_COLS), BF16)
    return pl.pallas_call(
        _seq_dft_kernel,
        out_shape=jax.ShapeDtypeStruct((FFT_SLABS, b, SEQ, FFT_COLS), BF16),
        grid=(b, FFT_SLABS),
        in_specs=[slab, slab, _const_spec((FFT_N2, 2 * FFT_N1, 2 * FFT_N1)), _const_spec((FFT_N2, 2 * FFT_N2))],
        out_specs=slab,
        scratch_shapes=[work, work],
        compiler_params=_params(2),
        name="seq_dft",
    )(a3, b3, consts["dft_seq1"], consts["dft_seq2"])


def _memkv_kernel(mem_ref, g_ref, w_ref, gk_ref, e_ref, k_ref, v_ref):
    h = _rms(mem_ref[...], g_ref[...]).astype(BF16)
    kv = jnp.dot(h, w_ref[...], preferred_element_type=F32)
    k = _head_rmsnorm(kv[:, :MEM_DIM], gk_ref[...], e_ref[...])
    k_ref[...] = k.T.astype(BF16)
    v_ref[...] = kv[:, MEM_DIM:].astype(BF16)


def _memkv(mem, g, w, gk, consts):
    b = mem.shape[0]
    out_blk = pl.BlockSpec((None, N_MEM, MEM_DIM), lambda i: (i, 0, 0))
    return pl.pallas_call(
        _memkv_kernel,
        out_shape=(jax.ShapeDtypeStruct((b, N_MEM, MEM_DIM), BF16),) * 2,
        grid=(b,),
        in_specs=[
            pl.BlockSpec((None, N_MEM, D_MODEL), lambda i: (i, 0, 0)),
            _const_spec((1, D_MODEL)),
            _const_spec((D_MODEL, 2 * MEM_DIM)),
            _const_spec((1, MEM_DIM)),
            _const_spec((MXU_WIDTH, MXU_WIDTH)),
        ],
        out_specs=(out_blk, out_blk),
        compiler_params=_params(1),
        name="memory_kv",
    )(mem, g, w, gk, consts["e_blk"])


def _mixout_kernel(x_ref, a_ref, qm_ref, km_ref, vm_ref, w_ref, o_ref, *, slab_major):
    km = km_ref[...]
    vm = vm_ref[...]
    for s in range(LIGHT_SUBTILES):
        rows = slice(s * SUBTILE_ROWS, (s + 1) * SUBTILE_ROWS)
        qm = qm_ref[rows]
        zero = jnp.zeros_like(qm)
        mo = jnp.zeros(qm.shape, F32)
        for head in range(MEM_HEADS):
            in_head = _head_lane_mask(qm.shape, head, 1)
            sc = jnp.dot(jnp.where(in_head, qm, zero), km, preferred_element_type=F32)
            e = jnp.exp(sc - jnp.max(sc, axis=-1, keepdims=True))
            l = jnp.sum(e, axis=-1, keepdims=True)
            oh = jnp.dot(e.astype(BF16), vm, preferred_element_type=F32) * (1.0 / l)
            mo = jnp.where(in_head, oh, mo)
        y = jnp.dot(mo.astype(BF16), w_ref[D_MODEL - MEM_DIM:, :], preferred_element_type=F32)
        if slab_major:
            for c in range(FFT_SLABS):
                y = y + jnp.dot(a_ref[c, rows], w_ref[c * FFT_COLS:(c + 1) * FFT_COLS, :],
                                preferred_element_type=F32)
        else:
            y = y + jnp.dot(a_ref[rows], w_ref[:D_MODEL - MEM_DIM, :], preferred_element_type=F32)
        o_ref[rows] = x_ref[rows] + y


def _mixout(x3, a3, qm3, km, vm, w_out):
    b = x3.shape[0]
    tm = LIGHT_TILE
    tok = lambda n: pl.BlockSpec((None, tm, n), lambda i, j: (i, j, 0))
    slab_major = a3.ndim == 4
    a_spec = (pl.BlockSpec((FFT_SLABS, None, tm, FFT_COLS), lambda i, j: (0, i, j, 0)) if slab_major
              else tok(a3.shape[-1]))
    mem_blk = pl.BlockSpec((None, N_MEM, MEM_DIM), lambda i, j: (i, 0, 0))
    return pl.pallas_call(
        functools.partial(_mixout_kernel, slab_major=slab_major),
        out_shape=jax.ShapeDtypeStruct(x3.shape, F32),
        grid=(b, SEQ // tm),
        in_specs=[tok(D_MODEL), a_spec, tok(MEM_DIM), mem_blk, mem_blk,
                  _const_spec((D_MODEL, D_MODEL))],
        out_specs=tok(D_MODEL),
        compiler_params=_params(2),
        name="mixer_out",
    )(x3, a3, qm3, km, vm, w_out)


def _head_mean_matrix():
    head = np.arange(MXU_WIDTH) // HEAD_DIM
    return jnp.asarray((head[:, None] == head[None, :]).astype(np.float32) / HEAD_DIM, BF16)


def _dft_constants():
    d = np.arange(FOURIER_DIM)
    ang = 2.0 * np.pi * ((d[:, None] * d[None, :]) % FOURIER_DIM) / FOURIER_DIM
    feat = np.concatenate([np.cos(ang), np.sin(ang)], axis=1) / np.sqrt(FOURIER_DIM)
    k1 = np.arange(FFT_N1)
    n1 = np.arange(FFT_N1)
    n2 = np.arange(FFT_N2)
    n = FFT_N2 * n1[None, None, :] + n2[:, None, None]
    ang1 = 2.0 * np.pi * ((k1[None, :, None] * n) % SEQ) / SEQ
    c1, s1 = np.cos(ang1) / np.sqrt(FFT_N1), np.sin(ang1) / np.sqrt(FFT_N1)
    seq1 = np.concatenate([np.concatenate([c1, -s1], axis=2), np.concatenate([s1, c1], axis=2)], axis=1)
    k2 = np.arange(FFT_N2)
    ang2 = 2.0 * np.pi * ((k2[:, None] * n2[None, :]) % FFT_N2) / FFT_N2
    seq2 = np.concatenate([np.cos(ang2), -np.sin(ang2)], axis=1) / np.sqrt(FFT_N2)
    return (jnp.asarray(feat, BF16), jnp.asarray(seq1, BF16), jnp.asarray(seq2, BF16))


def _constants():
    feat, seq1, seq2 = _dft_constants()
    return dict(e_blk=_head_mean_matrix(), dft_feat=feat, dft_seq1=seq1, dft_seq2=seq2)


def _na_bias_table(rpb):
    c = np.arange(GRID_W)
    cs = np.clip(c - KERNEL_COLS // 2, 0, GRID_W - KERNEL_COLS)
    inside = (c[None, :] >= cs[:, None]) & (c[None, :] < cs[:, None] + KERNEL_COLS)
    r = rpb.astype(F32) * LOG2_E
    pad = GRID_W - KERNEL_COLS
    r = jnp.concatenate([jnp.repeat(r[..., :1], pad, axis=-1), r, jnp.repeat(r[..., -1:], pad, axis=-1)], axis=-1)
    t = jnp.stack([r[..., GRID_W - 1 - q:2 * GRID_W - 1 - q] for q in range(GRID_W)], axis=-2)
    t = jnp.where(jnp.asarray(inside)[None, None], t, NEG_INF)
    t = jnp.concatenate([t[:, :-1], t[:, 1:]], axis=-1)
    n_pairs = NA_HEADS // NA_HEADS_PER_STEP
    t = t.reshape(n_pairs, NA_HEADS_PER_STEP, NA_BIAS_ROWS, GRID_W, LANES).transpose(0, 2, 1, 3, 4)
    return t.reshape(n_pairs, NA_BIAS_ROWS, LANES, LANES)


def _row(v):
    return v.reshape(1, -1).astype(F32)


def _tile_heads(g, heads):
    return jnp.tile(g.astype(F32), heads).reshape(1, -1)


def _trunk(x, mem, p, consts):
    b = x.shape[0]
    x2 = x.reshape(b * SEQ, D_MODEL)
    for i in range(DEPTH):
        x2 = _ffn(x2, p["norm_ffn1"][i], *p["ffn1"], i, p["norm_out"][i], False)
        km, vm = _memkv(mem, p["norm_mem"][i], p["w_mem_kv"][i], p["mem_k_norm"][i], consts)
        if i % 2 == 0:
            q, k, v, qm = _mixin_a(x2, p["norm_mix"][i], p["w_in_a"], p["na_q_norm"], p["na_k_norm"],
                                   p["mem_q_norm"][i], consts)
            seq3 = lambda t: t.reshape(b, SEQ, t.shape[-1])
            mixed = _na(seq3(q), seq3(k), seq3(v), p["na_bias"])
            w_out = p["w_out_a"]
        else:
            a, bb, qm = _mixin_b(x2, p["norm_mix"][i], p["w_in_b"], p["mem_q_norm"][i], consts)
            slabs = lambda t: t.reshape(FFT_SLABS, b, SEQ, FFT_COLS)
            mixed = _seq_dft_real(slabs(a), slabs(bb), consts)
            w_out = p["w_out_b"]
        x3 = _mixout(x2.reshape(b, SEQ, D_MODEL), mixed, qm.reshape(b, SEQ, MEM_DIM), km, vm, w_out)
        x2 = x3.reshape(b * SEQ, D_MODEL)
        x2 = _ffn(x2, p["norm_ffn2"][i], *p["ffn2"], i, p["norm_out"][i], True)
    return x2.reshape(b, SEQ, D_MODEL)


def kernel(x_prompt, x_sample, mem_prompt, mem_sample, norm_ffn1, w_ffn1_in, w_ffn1_out, norm_mix, norm_mem,
           w_mem_kv, mem_q_norm, mem_k_norm, w_in_a, na_q_norm, na_k_norm, na_rpb, w_out_a, w_in_b, w_out_b,
           norm_ffn2, w_ffn2_in, w_ffn2_out, norm_out):
    assert x_prompt.shape[1:] == (SEQ, D_MODEL) and x_sample.shape[1:] == (SEQ, D_MODEL)
    consts = _constants()
    p = dict(
        norm_ffn1=[_row(norm_ffn1[i]) for i in range(DEPTH)],
        norm_ffn2=[_row(norm_ffn2[i]) for i in range(DEPTH)],
        norm_mix=[_row(norm_mix[i]) for i in range(DEPTH)],
        norm_mem=[_row(norm_mem[i]) for i in range(DEPTH)],
        norm_out=[_row(norm_out[i]) for i in range(DEPTH)],
        ffn1=(w_ffn1_in.astype(BF16), w_ffn1_out.astype(BF16)),
        ffn2=(w_ffn2_in.astype(BF16), w_ffn2_out.astype(BF16)),
        w_mem_kv=[w_mem_kv[i].astype(BF16) for i in range(DEPTH)],
        mem_q_norm=[_tile_heads(mem_q_norm[i], MEM_HEADS) for i in range(DEPTH)],
        mem_k_norm=[_tile_heads(mem_k_norm[i], MEM_HEADS) for i in range(DEPTH)],
        w_in_a=w_in_a[0].astype(BF16),
        na_q_norm=_tile_heads(na_q_norm[0], NA_HEADS),
        na_k_norm=_tile_heads(na_k_norm[0], NA_HEADS),
        na_bias=_na_bias_table(na_rpb[0]),
        w_out_a=w_out_a[0].astype(BF16),
        w_in_b=w_in_b[0].astype(BF16),
        w_out_b=w_out_b[0].astype(BF16),
    )
    return (_trunk(x_prompt, mem_prompt, p, consts), _trunk(x_sample, mem_sample, p, consts))
```

```python
import functools

import numpy as np
import jax
import jax.numpy as jnp
from jax import lax
from jax.experimental import pallas as pl
from jax.experimental.pallas import tpu as pltpu

D_MODEL = 1024
DEPTH = 2
HEAD_DIM = 64
MEM_HEADS = 4
MEM_DIM = MEM_HEADS * HEAD_DIM
N_MEM = 256
NA_HEADS = 12
NA_DIM = NA_HEADS * HEAD_DIM
FOURIER_DIM = 768
GRID_W = 64
KERNEL_ROWS = 8
KERNEL_COLS = 16
D_FF = 2816
EPS = 1e-6
NEG_INF = -1e30
LOG2_E = 1.4426950408889634
SEQ = 8192
ROWS = SEQ // GRID_W
FFT_N1 = 64
FFT_N2 = 128

F32 = jnp.float32
BF16 = jnp.bfloat16

LANES = 128
MXU_WIDTH = 256
FFT_COLS = MXU_WIDTH
FFT_SLABS = FOURIER_DIM // FFT_COLS
FF_CHUNK = MXU_WIDTH
N_FF_CHUNKS = D_FF // FF_CHUNK
SUBTILE_ROWS = 512
SUBTILES = 2
TOKEN_TILE = SUBTILES * SUBTILE_ROWS
LIGHT_SUBTILES = 4
LIGHT_TILE = LIGHT_SUBTILES * SUBTILE_ROWS
VMEM_LIMIT = 56 * 1024 * 1024


def _params(n_axes):
    return pltpu.CompilerParams(dimension_semantics=("arbitrary",) * n_axes,
                                vmem_limit_bytes=VMEM_LIMIT)


def _const_spec(shape):
    n = len(shape)
    return pl.BlockSpec(shape, lambda *_: (0,) * n, pipeline_mode=pl.Buffered(1))


def _layer_spec(shape, layer):
    n = len(shape)
    return pl.BlockSpec((None,) + tuple(shape), lambda *_: (layer,) + (0,) * n, pipeline_mode=pl.Buffered(1))


def _rms(x, g):
    ms = jnp.mean(x * x, axis=-1, keepdims=True)
    return x * lax.rsqrt(ms + EPS) * g


def _head_rmsnorm(t, gain, e_blk):
    sq = (t * t).astype(BF16)
    slabs = [jnp.dot(sq[:, c:c + MXU_WIDTH], e_blk, preferred_element_type=F32)
             for c in range(0, t.shape[-1], MXU_WIDTH)]
    ms = slabs[0] if len(slabs) == 1 else jnp.concatenate(slabs, axis=-1)
    return t * lax.rsqrt(ms + EPS) * gain


def _head_lane_mask(shape, head, lane_axis):
    lane = lax.broadcasted_iota(jnp.int32, shape, lane_axis)
    return (lane >= head * HEAD_DIM) & (lane < (head + 1) * HEAD_DIM)


def _ffn_kernel(x_ref, g_ref, win_ref, wout_ref, gout_ref, o_ref, xn_ref, acc_ref, *, final_norm):
    for s in range(SUBTILES):
        rows = slice(s * SUBTILE_ROWS, (s + 1) * SUBTILE_ROWS)
        xn_ref[rows] = _rms(x_ref[rows], g_ref[...]).astype(BF16)
        for c in range(N_FF_CHUNKS):
            cols = slice(c * FF_CHUNK, (c + 1) * FF_CHUNK)
            up_cols = slice(D_FF + c * FF_CHUNK, D_FF + (c + 1) * FF_CHUNK)
            gate = jnp.dot(xn_ref[rows], win_ref[:, cols], preferred_element_type=F32)
            up = jnp.dot(xn_ref[rows], win_ref[:, up_cols], preferred_element_type=F32)
            act = (gate * jax.nn.sigmoid(gate) * up).astype(BF16)
            part = jnp.dot(act, wout_ref[cols, :], preferred_element_type=F32)
            if c == 0:
                acc_ref[rows] = part
            elif c < N_FF_CHUNKS - 1:
                acc_ref[rows] += part
            else:
                y = x_ref[rows] + 0.5 * (acc_ref[rows] + part)
        if final_norm:
            y = _rms(y, gout_ref[...])
        o_ref[rows] = y


def _ffn(x2d, g, w_in, w_out, layer, gout, final_norm):
    t = x2d.shape[0]
    tm = TOKEN_TILE
    return pl.pallas_call(
        functools.partial(_ffn_kernel, final_norm=final_norm),
        out_shape=jax.ShapeDtypeStruct((t, D_MODEL), F32),
        grid=(t // tm,),
        in_specs=[
            pl.BlockSpec((tm, D_MODEL), lambda i: (i, 0)),
            _const_spec((1, D_MODEL)),
            _layer_spec((D_MODEL, 2 * D_FF), layer),
            _layer_spec((D_FF, D_MODEL), layer),
            _const_spec((1, D_MODEL)),
        ],
        out_specs=pl.BlockSpec((tm, D_MODEL), lambda i: (i, 0)),
        scratch_shapes=[pltpu.VMEM((tm, D_MODEL), BF16), pltpu.VMEM((tm, D_MODEL), F32)],
        compiler_params=_params(1),
        name="ffn_final" if final_norm else "ffn",
    )(x2d, g, w_in, w_out, gout)


def _mixin_a_kernel(x_ref, g_ref, w_ref, gq_ref, gk_ref, gqm_ref, e_ref, q_ref, k_ref, v_ref, qm_ref):
    scale = HEAD_DIM ** -0.5
    e_blk = e_ref[...]
    for s in range(LIGHT_SUBTILES):
        rows = slice(s * SUBTILE_ROWS, (s + 1) * SUBTILE_ROWS)
        h = _rms(x_ref[rows], g_ref[...]).astype(BF16)
        proj = lambda lo, hi: jnp.dot(h, w_ref[:, lo:hi], preferred_element_type=F32)
        q = proj(0, NA_DIM)
        k = proj(NA_DIM, 2 * NA_DIM)
        q_ref[rows] = (_head_rmsnorm(q, gq_ref[...], e_blk) * (scale * LOG2_E)).astype(BF16)
        qm = proj(3 * NA_DIM, 3 * NA_DIM + MEM_DIM)
        k_ref[rows] = _head_rmsnorm(k, gk_ref[...], e_blk).astype(BF16)
        v = proj(2 * NA_DIM, 3 * NA_DIM)
        qm_ref[rows] = (_head_rmsnorm(qm, gqm_ref[...], e_blk) * scale).astype(BF16)
        v_ref[rows] = v.astype(BF16)


def _mixin_a(x2d, g, w, gq, gk, gqm, consts):
    t = x2d.shape[0]
    tm = LIGHT_TILE
    width = 3 * NA_DIM + MEM_DIM
    tok = lambda n: pl.BlockSpec((tm, n), lambda i: (i, 0))
    return pl.pallas_call(
        _mixin_a_kernel,
        out_shape=(jax.ShapeDtypeStruct((t, NA_DIM), BF16),) * 3 + (jax.ShapeDtypeStruct((t, MEM_DIM), BF16),),
        grid=(t // tm,),
        in_specs=[
            tok(D_MODEL),
            _const_spec((1, D_MODEL)),
            _const_spec((D_MODEL, width)),
            _const_spec((1, NA_DIM)), _const_spec((1, NA_DIM)), _const_spec((1, MEM_DIM)),
            _const_spec((MXU_WIDTH, MXU_WIDTH)),
        ],
        out_specs=(tok(NA_DIM), tok(NA_DIM), tok(NA_DIM), tok(MEM_DIM)),
        compiler_params=_params(1),
        name="mixin_na",
    )(x2d, g, w, gq, gk, gqm, consts["e_blk"])


def _mixin_b_kernel(x_ref, g_ref, w_ref, gqm_ref, e_ref, dft_ref, a_ref, b_ref, qm_ref):
    scale = HEAD_DIM ** -0.5
    for s in range(LIGHT_SUBTILES):
        rows = slice(s * SUBTILE_ROWS, (s + 1) * SUBTILE_ROWS)
        h = _rms(x_ref[rows], g_ref[...]).astype(BF16)
        proj = jnp.dot(h, w_ref[...], preferred_element_type=F32)
        z = proj[:, :FOURIER_DIM].astype(BF16)
        qm = proj[:, FOURIER_DIM:]
        ab = jnp.dot(z, dft_ref[...], preferred_element_type=F32)
        for c in range(FFT_SLABS):
            a_ref[c, rows] = ab[:, c * FFT_COLS:(c + 1) * FFT_COLS].astype(BF16)
            b_ref[c, rows] = ab[:, FOURIER_DIM + c * FFT_COLS:FOURIER_DIM + (c + 1) * FFT_COLS].astype(BF16)
        qm_ref[rows] = (_head_rmsnorm(qm, gqm_ref[...], e_ref[...]) * scale).astype(BF16)


def _mixin_b(x2d, g, w, gqm, consts):
    t = x2d.shape[0]
    tm = LIGHT_TILE
    tok = lambda n: pl.BlockSpec((tm, n), lambda i: (i, 0))
    slab = pl.BlockSpec((FFT_SLABS, tm, FFT_COLS), lambda i: (0, i, 0))
    return pl.pallas_call(
        _mixin_b_kernel,
        out_shape=(jax.ShapeDtypeStruct((FFT_SLABS, t, FFT_COLS), BF16),) * 2
        + (jax.ShapeDtypeStruct((t, MEM_DIM), BF16),),
        grid=(t // tm,),
        in_specs=[
            tok(D_MODEL),
            _const_spec((1, D_MODEL)),
            _const_spec((D_MODEL, FOURIER_DIM + MEM_DIM)),
            _const_spec((1, MEM_DIM)),
            _const_spec((MXU_WIDTH, MXU_WIDTH)),
            _const_spec((FOURIER_DIM, 2 * FOURIER_DIM)),
        ],
        out_specs=(slab, slab, tok(MEM_DIM)),
        compiler_params=_params(1),
        name="mixin_fnet",
    )(x2d, g, w, gqm, consts["e_blk"], consts["dft_feat"])


NA_HEADS_PER_STEP = LANES // HEAD_DIM
NA_KEYS = KERNEL_ROWS * GRID_W
NA_BIAS_ROWS = 2 * KERNEL_ROWS - 2
NA_TICK_ROWS = 2
NA_TICKS = ROWS // NA_TICK_ROWS
NA_LOOP_TICKS = 62


def _na_window(r):
    if isinstance(r, int):
        rs = min(max(r - KERNEL_ROWS // 2, 0), ROWS - KERNEL_ROWS)
        return rs * GRID_W, rs - r + (KERNEL_ROWS - 1)
    rs = jnp.clip(r - KERNEL_ROWS // 2, 0, ROWS - KERNEL_ROWS)
    return pl.multiple_of(rs * GRID_W, GRID_W), rs - r + (KERNEL_ROWS - 1)


def _na_row_start(r):
    return r * GRID_W if isinstance(r, int) else pl.multiple_of(r * GRID_W, GRID_W)


def _na_kernel(q_ref, k_ref, v_ref, bias_ref, o_ref, s_ref, e_ref, l_ref):
    first_head = _head_lane_mask((GRID_W, LANES), 0, 1)

    def scores(t, slot):
        for j in range(NA_TICK_ROWS):
            r = t * NA_TICK_ROWS + j
            key_start, shift = _na_window(r)
            q = q_ref[pl.ds(_na_row_start(r), GRID_W), :]
            zero = jnp.zeros_like(q)
            q2 = jnp.concatenate([jnp.where(first_head, q, zero), jnp.where(first_head, zero, q)], axis=0)
            kb = k_ref[pl.ds(key_start, NA_KEYS), :]
            s = lax.dot_general(q2, kb, (((1,), (1,)), ((), ())), preferred_element_type=F32)
            bias = jnp.concatenate([bias_ref[shift + 2 * i] for i in range(KERNEL_ROWS // 2)], axis=1)
            s_ref[slot, j] = s + bias

    def softmax(slot):
        for j in range(NA_TICK_ROWS):
            s = s_ref[slot, j]
            m = jnp.max(s, axis=-1, keepdims=True)
            e = jnp.exp2(s - m)
            l = jnp.sum(e, axis=-1, keepdims=True)
            e_ref[slot, j] = e.astype(BF16)
            l_ref[slot, j] = jnp.broadcast_to(1.0 / l, (LANES, LANES))

    def values(t, slot):
        for j in range(NA_TICK_ROWS):
            r = t * NA_TICK_ROWS + j
            key_start, _ = _na_window(r)
            vb = v_ref[pl.ds(key_start, NA_KEYS), :]
            o2 = jnp.dot(e_ref[slot, j], vb, preferred_element_type=F32) * l_ref[slot, j]
            o = jnp.where(first_head, o2[:GRID_W], o2[GRID_W:])
            o_ref[pl.ds(_na_row_start(r), GRID_W), :] = o.astype(BF16)

    def tick(t, parity, do_scores=True, do_values=True, do_softmax=True):
        if do_values:
            values(t - 2, parity)
        if do_scores:
            scores(t, parity)
        if do_softmax:
            softmax(1 - parity)

    tick(0, 0, do_values=False, do_softmax=False)
    tick(1, 1, do_values=False)

    def body(u, carry):
        for i in range(NA_LOOP_TICKS):
            tick(2 + NA_LOOP_TICKS * u + i, i % 2)
        return carry

    n_iter = (NA_TICKS - 2) // NA_LOOP_TICKS
    lax.fori_loop(0, n_iter, body, 0)
    for t in range(2 + n_iter * NA_LOOP_TICKS, NA_TICKS):
        tick(t, t % 2)
    tick(NA_TICKS, 0, do_scores=False)
    tick(NA_TICKS + 1, 1, do_scores=False, do_softmax=False)


def _na(q3, k3, v3, bias):
    b = q3.shape[0]
    n_pairs = NA_HEADS // NA_HEADS_PER_STEP
    seq_spec = pl.BlockSpec((None, SEQ, LANES), lambda i, j: (i, 0, j))
    return pl.pallas_call(
        _na_kernel,
        out_shape=jax.ShapeDtypeStruct((b, SEQ, NA_DIM), BF16),
        grid=(b, n_pairs),
        in_specs=[seq_spec, seq_spec, seq_spec,
                  pl.BlockSpec((None, NA_BIAS_ROWS, LANES, LANES), lambda i, j: (j, 0, 0, 0))],
        out_specs=seq_spec,
        scratch_shapes=[pltpu.VMEM((2, NA_TICK_ROWS, LANES, NA_KEYS), F32),
                        pltpu.VMEM((2, NA_TICK_ROWS, LANES, NA_KEYS), BF16),
                        pltpu.VMEM((2, NA_TICK_ROWS, LANES, LANES), F32)],
        compiler_params=_params(2),
        name="na_attention",
    )(q3, k3, v3, bias)


def _swap_leading(x):
    return pltpu.einshape("abc->bac", x)


def _seq_dft_kernel(a_ref, b_ref, g_ref, h_ref, f_ref, t1_ref, t2_ref):
    cols = a_ref.shape[-1]
    t1_ref[:, :FFT_N1, :] = _swap_leading(a_ref[...].reshape(FFT_N1, FFT_N2, cols))
    t1_ref[:, FFT_N1:, :] = _swap_leading(b_ref[...].reshape(FFT_N1, FFT_N2, cols))

    for j in range(FFT_N2):
        t2_ref[j] = jnp.dot(g_ref[j], t1_ref[j], preferred_element_type=F32).astype(BF16)
    t1_ref[...] = _swap_leading(t2_ref[...])

    for k1 in range(FFT_N1):
        x = jnp.concatenate([t1_ref[k1], t1_ref[FFT_N1 + k1]], axis=0)
        t2_ref[k1] = jnp.dot(h_ref[...], x, preferred_element_type=F32).astype(BF16)
    f_ref[...] = _swap_leading(t2_ref[:FFT_N1]).reshape(SEQ, cols)


def _seq_dft_real(a3, b3, consts):
    b = a3.shape[1]
    slab = pl.BlockSpec((None, None, SEQ, FFT_COLS), lambda i, j: (j, i, 0, 0))
    work = pltpu.VMEM((FFT_N2, 2 * FFT_N1, FFT_COLS), BF16)
    return pl.pallas_call(
        _seq_dft_kernel,
        out_shape=jax.ShapeDtypeStruct((FFT_SLABS, b, SEQ, FFT_COLS), BF16),
        grid=(b, FFT_SLABS),
        in_specs=[slab, slab, _const_spec((FFT_N2, 2 * FFT_N1, 2 * FFT_N1)), _const_spec((FFT_N2, 2 * FFT_N2))],
        out_specs=slab,
        scratch_shapes=[work, work],
        compiler_params=_params(2),
        name="seq_dft",
    )(a3, b3, consts["dft_seq1"], consts["dft_seq2"])


def _memkv_kernel(mem_ref, g_ref, w_ref, gk_ref, e_ref, k_ref, v_ref):
    h = _rms(mem_ref[...], g_ref[...]).astype(BF16)
    kv = jnp.dot(h, w_ref[...], preferred_element_type=F32)
    k = _head_rmsnorm(kv[:, :MEM_DIM], gk_ref[...], e_ref[...])
    k_ref[...] = k.T.astype(BF16)
    v_ref[...] = kv[:, MEM_DIM:].astype(BF16)


def _memkv(mem, g, w, gk, consts):
    b = mem.shape[0]
    out_blk = pl.BlockSpec((None, N_MEM, MEM_DIM), lambda i: (i, 0, 0))
    return pl.pallas_call(
        _memkv_kernel,
        out_shape=(jax.ShapeDtypeStruct((b, N_MEM, MEM_DIM), BF16),) * 2,
        grid=(b,),
        in_specs=[
            pl.BlockSpec((None, N_MEM, D_MODEL), lambda i: (i, 0, 0)),
            _const_spec((1, D_MODEL)),
            _const_spec((D_MODEL, 2 * MEM_DIM)),
            _const_spec((1, MEM_DIM)),
            _const_spec((MXU_WIDTH, MXU_WIDTH)),
        ],
        out_specs=(out_blk, out_blk),
        compiler_params=_params(1),
        name="memory_kv",
    )(mem, g, w, gk, consts["e_blk"])


def _mixout_kernel(x_ref, a_ref, qm_ref, km_ref, vm_ref, w_ref, o_ref, *, slab_major):
    km = km_ref[...]
    vm = vm_ref[...]
    for s in range(LIGHT_SUBTILES):
        rows = slice(s * SUBTILE_ROWS, (s + 1) * SUBTILE_ROWS)
        qm = qm_ref[rows]
        zero = jnp.zeros_like(qm)
        mo = jnp.zeros(qm.shape, F32)
        for head in range(MEM_HEADS):
            in_head = _head_lane_mask(qm.shape, head, 1)
            sc = jnp.dot(jnp.where(in_head, qm, zero), km, preferred_element_type=F32)
            e = jnp.exp(sc - jnp.max(sc, axis=-1, keepdims=True))
            l = jnp.sum(e, axis=-1, keepdims=True)
            oh = jnp.dot(e.astype(BF16), vm, preferred_element_type=F32) * (1.0 / l)
            mo = jnp.where(in_head, oh, mo)
        y = jnp.dot(mo.astype(BF16), w_ref[D_MODEL - MEM_DIM:, :], preferred_element_type=F32)
        if slab_major:
            for c in range(FFT_SLABS):
                y = y + jnp.dot(a_ref[c, rows], w_ref[c * FFT_COLS:(c + 1) * FFT_COLS, :],
                                preferred_element_type=F32)
        else:
            y = y + jnp.dot(a_ref[rows], w_ref[:D_MODEL - MEM_DIM, :], preferred_element_type=F32)
        o_ref[rows] = x_ref[rows] + y


def _mixout(x3, a3, qm3, km, vm, w_out):
    b = x3.shape[0]
    tm = LIGHT_TILE
    tok = lambda n: pl.BlockSpec((None, tm, n), lambda i, j: (i, j, 0))
    slab_major = a3.ndim == 4
    a_spec = (pl.BlockSpec((FFT_SLABS, None, tm, FFT_COLS), lambda i, j: (0, i, j, 0)) if slab_major
              else tok(a3.shape[-1]))
    mem_blk = pl.BlockSpec((None, N_MEM, MEM_DIM), lambda i, j: (i, 0, 0))
    return pl.pallas_call(
        functools.partial(_mixout_kernel, slab_major=slab_major),
        out_shape=jax.ShapeDtypeStruct(x3.shape, F32),
        grid=(b, SEQ // tm),
        in_specs=[tok(D_MODEL), a_spec, tok(MEM_DIM), mem_blk, mem_blk,
                  _const_spec((D_MODEL, D_MODEL))],
        out_specs=tok(D_MODEL),
        compiler_params=_params(2),
        name="mixer_out",
    )(x3, a3, qm3, km, vm, w_out)


def _head_mean_matrix():
    head = np.arange(MXU_WIDTH) // HEAD_DIM
    return jnp.asarray((head[:, None] == head[None, :]).astype(np.float32) / HEAD_DIM, BF16)


def _dft_constants():
    d = np.arange(FOURIER_DIM)
    ang = 2.0 * np.pi * ((d[:, None] * d[None, :]) % FOURIER_DIM) / FOURIER_DIM
    feat = np.concatenate([np.cos(ang), np.sin(ang)], axis=1) / np.sqrt(FOURIER_DIM)
    k1 = np.arange(FFT_N1)
    n1 = np.arange(FFT_N1)
    n2 = np.arange(FFT_N2)
    n = FFT_N2 * n1[None, None, :] + n2[:, None, None]
    ang1 = 2.0 * np.pi * ((k1[None, :, None] * n) % SEQ) / SEQ
    c1, s1 = np.cos(ang1) / np.sqrt(FFT_N1), np.sin(ang1) / np.sqrt(FFT_N1)
    seq1 = np.concatenate([np.concatenate([c1, -s1], axis=2), np.concatenate([s1, c1], axis=2)], axis=1)
    k2 = np.arange(FFT_N2)
    ang2 = 2.0 * np.pi * ((k2[:, None] * n2[None, :]) % FFT_N2) / FFT_N2
    seq2 = np.concatenate([np.cos(ang2), -np.sin(ang2)], axis=1) / np.sqrt(FFT_N2)
    return (jnp.asarray(feat, BF16), jnp.asarray(seq1, BF16), jnp.asarray(seq2, BF16))


def _constants():
    feat, seq1, seq2 = _dft_constants()
    return dict(e_blk=_head_mean_matrix(), dft_feat=feat, dft_seq1=seq1, dft_seq2=seq2)


def _na_bias_table(rpb):
    c = np.arange(GRID_W)
    cs = np.clip(c - KERNEL_COLS // 2, 0, GRID_W - KERNEL_COLS)
    inside = (c[None, :] >= cs[:, None]) & (c[None, :] < cs[:, None] + KERNEL_COLS)
    r = rpb.astype(F32) * LOG2_E
    pad = GRID_W - KERNEL_COLS
    r = jnp.concatenate([jnp.repeat(r[..., :1], pad, axis=-1), r, jnp.repeat(r[..., -1:], pad, axis=-1)], axis=-1)
    t = jnp.stack([r[..., GRID_W - 1 - q:2 * GRID_W - 1 - q] for q in range(GRID_W)], axis=-2)
    t = jnp.where(jnp.asarray(inside)[None, None], t, NEG_INF)
    t = jnp.concatenate([t[:, :-1], t[:, 1:]], axis=-1)
    n_pairs = NA_HEADS // NA_HEADS_PER_STEP
    t = t.reshape(n_pairs, NA_HEADS_PER_STEP, NA_BIAS_ROWS, GRID_W, LANES).transpose(0, 2, 1, 3, 4)
    return t.reshape(n_pairs, NA_BIAS_ROWS, LANES, LANES)


def _row(v):
    return v.reshape(1, -1).astype(F32)


def _tile_heads(g, heads):
    return jnp.tile(g.astype(F32), heads).reshape(1, -1)


def _trunk(x, mem, p, consts):
    b = x.shape[0]
    x2 = x.reshape(b * SEQ, D_MODEL)
    for i in range(DEPTH):
        x2 = _ffn(x2, p["norm_ffn1"][i], *p["ffn1"], i, p["norm_out"][i], False)
        km, vm = _memkv(mem, p["norm_mem"][i], p["w_mem_kv"][i], p["mem_k_norm"][i], consts)
        if i % 2 == 0:
            q, k, v, qm = _mixin_a(x2, p["norm_mix"][i], p["w_in_a"], p["na_q_norm"], p["na_k_norm"],
                                   p["mem_q_norm"][i], consts)
            seq3 = lambda t: t.reshape(b, SEQ, t.shape[-1])
            mixed = _na(seq3(q), seq3(k), seq3(v), p["na_bias"])
            w_out = p["w_out_a"]
        else:
            a, bb, qm = _mixin_b(x2, p["norm_mix"][i], p["w_in_b"], p["mem_q_norm"][i], consts)
            slabs = lambda t: t.reshape(FFT_SLABS, b, SEQ, FFT_COLS)
            mixed = _seq_dft_real(slabs(a), slabs(bb), consts)
            w_out = p["w_out_b"]
        x3 = _mixout(x2.reshape(b, SEQ, D_MODEL), mixed, qm.reshape(b, SEQ, MEM_DIM), km, vm, w_out)
        x2 = x3.reshape(b * SEQ, D_MODEL)
        x2 = _ffn(x2, p["norm_ffn2"][i], *p["ffn2"], i, p["norm_out"][i], True)
    return x2.reshape(b, SEQ, D_MODEL)


def kernel(x_prompt, x_sample, mem_prompt, mem_sample, norm_ffn1, w_ffn1_in, w_ffn1_out, norm_mix, norm_mem,
           w_mem_kv, mem_q_norm, mem_k_norm, w_in_a, na_q_norm, na_k_norm, na_rpb, w_out_a, w_in_b, w_out_b,
           norm_ffn2, w_ffn2_in, w_ffn2_out, norm_out):
    assert x_prompt.shape[1:] == (SEQ, D_MODEL) and x_sample.shape[1:] == (SEQ, D_MODEL)
    consts = _constants()
    p = dict(
        norm_ffn1=[_row(norm_ffn1[i]) for i in range(DEPTH)],
        norm_ffn2=[_row(norm_ffn2[i]) for i in range(DEPTH)],
        norm_mix=[_row(norm_mix[i]) for i in range(DEPTH)],
        norm_mem=[_row(norm_mem[i]) for i in range(DEPTH)],
        norm_out=[_row(norm_out[i]) for i in range(DEPTH)],
        ffn1=(w_ffn1_in.astype(BF16), w_ffn1_out.astype(BF16)),
        ffn2=(w_ffn2_in.astype(BF16), w_ffn2_out.astype(BF16)),
        w_mem_kv=[w_mem_kv[i].astype(BF16) for i in range(DEPTH)],
        mem_q_norm=[_tile_heads(mem_q_norm[i], MEM_HEADS) for i in range(DEPTH)],
        mem_k_norm=[_tile_heads(mem_k_norm[i], MEM_HEADS) for i in range(DEPTH)],
        w_in_a=w_in_a[0].astype(BF16),
        na_q_norm=_tile_heads(na_q_norm[0], NA_HEADS),
        na_k_norm=_tile_heads(na_k_norm[0], NA_HEADS),
        na_bias=_na_bias_table(na_rpb[0]),
        w_out_a=w_out_a[0].astype(BF16),
        w_in_b=w_in_b[0].astype(BF16),
        w_out_b=w_out_b[0].astype(BF16),
    )
    return (_trunk(x_prompt, mem_prompt, p, consts), _trunk(x_sample, mem_sample, p, consts))
```

```python
import functools

import numpy as np
import jax
import jax.numpy as jnp
from jax import lax
from jax.experimental import pallas as pl
from jax.experimental.pallas import tpu as pltpu

D_MODEL = 1024
DEPTH = 2
HEAD_DIM = 64
MEM_HEADS = 4
MEM_DIM = MEM_HEADS * HEAD_DIM
N_MEM = 256
NA_HEADS = 12
NA_DIM = NA_HEADS * HEAD_DIM
FOURIER_DIM = 768
GRID_W = 64
KERNEL_ROWS = 8
KERNEL_COLS = 16
D_FF = 2816
EPS = 1e-6
NEG_INF = -1e30
LOG2_E = 1.4426950408889634
SEQ = 8192
ROWS = SEQ // GRID_W
FFT_N1 = 64
FFT_N2 = 128

F32 = jnp.float32
BF16 = jnp.bfloat16

LANES = 128
MXU_WIDTH = 256
FFT_COLS = MXU_WIDTH
FFT_SLABS = FOURIER_DIM // FFT_COLS
FF_CHUNK = MXU_WIDTH
N_FF_CHUNKS = D_FF // FF_CHUNK
SUBTILE_ROWS = 512
SUBTILES = 2
TOKEN_TILE = SUBTILES * SUBTILE_ROWS
LIGHT_SUBTILES = 4
LIGHT_TILE = LIGHT_SUBTILES * SUBTILE_ROWS
VMEM_LIMIT = 56 * 1024 * 1024


def _params(n_axes):
    return pltpu.CompilerParams(dimension_semantics=("arbitrary",) * n_axes,
                                vmem_limit_bytes=VMEM_LIMIT)


def _const_spec(shape):
    n = len(shape)
    return pl.BlockSpec(shape, lambda *_: (0,) * n, pipeline_mode=pl.Buffered(1))


def _layer_spec(shape, layer):
    n = len(shape)
    return pl.BlockSpec((None,) + tuple(shape), lambda *_: (layer,) + (0,) * n, pipeline_mode=pl.Buffered(1))


def _rms(x, g):
    ms = jnp.mean(x * x, axis=-1, keepdims=True)
    return x * lax.rsqrt(ms + EPS) * g


def _head_rmsnorm(t, gain, e_blk):
    sq = (t * t).astype(BF16)
    slabs = [jnp.dot(sq[:, c:c + MXU_WIDTH], e_blk, preferred_element_type=F32)
             for c in range(0, t.shape[-1], MXU_WIDTH)]
    ms = slabs[0] if len(slabs) == 1 else jnp.concatenate(slabs, axis=-1)
    return t * lax.rsqrt(ms + EPS) * gain


def _head_lane_mask(shape, head, lane_axis):
    lane = lax.broadcasted_iota(jnp.int32, shape, lane_axis)
    return (lane >= head * HEAD_DIM) & (lane < (head + 1) * HEAD_DIM)


def _ffn_kernel(x_ref, g_ref, win_ref, wout_ref, gout_ref, o_ref, xn_ref, acc_ref, *, final_norm):
    for s in range(SUBTILES):
        rows = slice(s * SUBTILE_ROWS, (s + 1) * SUBTILE_ROWS)
        xn_ref[rows] = _rms(x_ref[rows], g_ref[...]).astype(BF16)
        for c in range(N_FF_CHUNKS):
            cols = slice(c * FF_CHUNK, (c + 1) * FF_CHUNK)
            up_cols = slice(D_FF + c * FF_CHUNK, D_FF + (c + 1) * FF_CHUNK)
            gate = jnp.dot(xn_ref[rows], win_ref[:, cols], preferred_element_type=F32)
            up = jnp.dot(xn_ref[rows], win_ref[:, up_cols], preferred_element_type=F32)
            act = (gate * jax.nn.sigmoid(gate) * up).astype(BF16)
            part = jnp.dot(act, wout_ref[cols, :], preferred_element_type=F32)
            if c == 0:
                acc_ref[rows] = part
            elif c < N_FF_CHUNKS - 1:
                acc_ref[rows] += part
            else:
                y = x_ref[rows] + 0.5 * (acc_ref[rows] + part)
        if final_norm:
            y = _rms(y, gout_ref[...])
        o_ref[rows] = y


def _ffn(x2d, g, w_in, w_out, layer, gout, final_norm):
    t = x2d.shape[0]
    tm = TOKEN_TILE
    return pl.pallas_call(
        functools.partial(_ffn_kernel, final_norm=final_norm),
        out_shape=jax.ShapeDtypeStruct((t, D_MODEL), F32),
        grid=(t // tm,),
        in_specs=[
            pl.BlockSpec((tm, D_MODEL), lambda i: (i, 0)),
            _const_spec((1, D_MODEL)),
            _layer_spec((D_MODEL, 2 * D_FF), layer),
            _layer_spec((D_FF, D_MODEL), layer),
            _const_spec((1, D_MODEL)),
        ],
        out_specs=pl.BlockSpec((tm, D_MODEL), lambda i: (i, 0)),
        scratch_shapes=[pltpu.VMEM((tm, D_MODEL), BF16), pltpu.VMEM((tm, D_MODEL), F32)],
        compiler_params=_params(1),
        name="ffn_final" if final_norm else "ffn",
    )(x2d, g, w_in, w_out, gout)


def _mixin_a_kernel(x_ref, g_ref, w_ref, gq_ref, gk_ref, gqm_ref, e_ref, q_ref, k_ref, v_ref, qm_ref):
    scale = HEAD_DIM ** -0.5
    e_blk = e_ref[...]
    for s in range(LIGHT_SUBTILES):
        rows = slice(s * SUBTILE_ROWS, (s + 1) * SUBTILE_ROWS)
        h = _rms(x_ref[rows], g_ref[...]).astype(BF16)
        proj = lambda lo, hi: jnp.dot(h, w_ref[:, lo:hi], preferred_element_type=F32)
        q = proj(0, NA_DIM)
        k = proj(NA_DIM, 2 * NA_DIM)
        q_ref[rows] = (_head_rmsnorm(q, gq_ref[...], e_blk) * (scale * LOG2_E)).astype(BF16)
        qm = proj(3 * NA_DIM, 3 * NA_DIM + MEM_DIM)
        k_ref[rows] = _head_rmsnorm(k, gk_ref[...], e_blk).astype(BF16)
        v = proj(2 * NA_DIM, 3 * NA_DIM)
        qm_ref[rows] = (_head_rmsnorm(qm, gqm_ref[...], e_blk) * scale).astype(BF16)
        v_ref[rows] = v.astype(BF16)


def _mixin_a(x2d, g, w, gq, gk, gqm, consts):
    t = x2d.shape[0]
    tm = LIGHT_TILE
    width = 3 * NA_DIM + MEM_DIM
    tok = lambda n: pl.BlockSpec((tm, n), lambda i: (i, 0))
    return pl.pallas_call(
        _mixin_a_kernel,
        out_shape=(jax.ShapeDtypeStruct((t, NA_DIM), BF16),) * 3 + (jax.ShapeDtypeStruct((t, MEM_DIM), BF16),),
        grid=(t // tm,),
        in_specs=[
            tok(D_MODEL),
            _const_spec((1, D_MODEL)),
            _const_spec((D_MODEL, width)),
            _const_spec((1, NA_DIM)), _const_spec((1, NA_DIM)), _const_spec((1, MEM_DIM)),
            _const_spec((MXU_WIDTH, MXU_WIDTH)),
        ],
        out_specs=(tok(NA_DIM), tok(NA_DIM), tok(NA_DIM), tok(MEM_DIM)),
        compiler_params=_params(1),
        name="mixin_na",
    )(x2d, g, w, gq, gk, gqm, consts["e_blk"])


def _mixin_b_kernel(x_ref, g_ref, w_ref, gqm_ref, e_ref, dft_ref, a_ref, b_ref, qm_ref):
    scale = HEAD_DIM ** -0.5
    for s in range(LIGHT_SUBTILES):
        rows = slice(s * SUBTILE_ROWS, (s + 1) * SUBTILE_ROWS)
        h = _rms(x_ref[rows], g_ref[...]).astype(BF16)
        proj = jnp.dot(h, w_ref[...], preferred_element_type=F32)
        z = proj[:, :FOURIER_DIM].astype(BF16)
        qm = proj[:, FOURIER_DIM:]
        ab = jnp.dot(z, dft_ref[...], preferred_element_type=F32)
        for c in range(FFT_SLABS):
            a_ref[c, rows] = ab[:, c * FFT_COLS:(c + 1) * FFT_COLS].astype(BF16)
            b_ref[c, rows] = ab[:, FOURIER_DIM + c * FFT_COLS:FOURIER_DIM + (c + 1) * FFT_COLS].astype(BF16)
        qm_ref[rows] = (_head_rmsnorm(qm, gqm_ref[...], e_ref[...]) * scale).astype(BF16)


def _mixin_b(x2d, g, w, gqm, consts):
    t = x2d.shape[0]
    tm = LIGHT_TILE
    tok = lambda n: pl.BlockSpec((tm, n), lambda i: (i, 0))
    slab = pl.BlockSpec((FFT_SLABS, tm, FFT_COLS), lambda i: (0, i, 0))
    return pl.pallas_call(
        _mixin_b_kernel,
        out_shape=(jax.ShapeDtypeStruct((FFT_SLABS, t, FFT_COLS), BF16),) * 2
        + (jax.ShapeDtypeStruct((t, MEM_DIM), BF16),),
        grid=(t // tm,),
        in_specs=[
            tok(D_MODEL),
            _const_spec((1, D_MODEL)),
            _const_spec((D_MODEL, FOURIER_DIM + MEM_DIM)),
            _const_spec((1, MEM_DIM)),
            _const_spec((MXU_WIDTH, MXU_WIDTH)),
            _const_spec((FOURIER_DIM, 2 * FOURIER_DIM)),
        ],
        out_specs=(slab, slab, tok(MEM_DIM)),
        compiler_params=_params(1),
        name="mixin_fnet",
    )(x2d, g, w, gqm, consts["e_blk"], consts["dft_feat"])


NA_HEADS_PER_STEP = LANES // HEAD_DIM
NA_KEYS = KERNEL_ROWS * GRID_W
NA_BIAS_ROWS = 2 * KERNEL_ROWS - 2
NA_TICK_ROWS = 2
NA_TICKS = ROWS // NA_TICK_ROWS
NA_LOOP_TICKS = 62


def _na_window(r):
    if isinstance(r, int):
        rs = min(max(r - KERNEL_ROWS // 2, 0), ROWS - KERNEL_ROWS)
        return rs * GRID_W, rs - r + (KERNEL_ROWS - 1)
    rs = jnp.clip(r - KERNEL_ROWS // 2, 0, ROWS - KERNEL_ROWS)
    return pl.multiple_of(rs * GRID_W, GRID_W), rs - r + (KERNEL_ROWS - 1)


def _na_row_start(r):
    return r * GRID_W if isinstance(r, int) else pl.multiple_of(r * GRID_W, GRID_W)


def _na_kernel(q_ref, k_ref, v_ref, bias_ref, o_ref, s_ref, e_ref, l_ref):
    first_head = _head_lane_mask((GRID_W, LANES), 0, 1)

    def scores(t, slot, js):
        for j in js:
            r = t * NA_TICK_ROWS + j
            key_start, shift = _na_window(r)
            q = q_ref[pl.ds(_na_row_start(r), GRID_W), :]
            zero = jnp.zeros_like(q)
            q2 = jnp.concatenate([jnp.where(first_head, q, zero), jnp.where(first_head, zero, q)], axis=0)
            kb = k_ref[pl.ds(key_start, NA_KEYS), :]
            s = lax.dot_general(q2, kb, (((1,), (1,)), ((), ())), preferred_element_type=F32)
            bias = jnp.concatenate([bias_ref[shift + 2 * i] for i in range(KERNEL_ROWS // 2)], axis=1)
            s_ref[slot, j] = s + bias

    def softmax(slot, js):
        for j in js:
            s = s_ref[slot, j]
            m = jnp.max(s, axis=-1, keepdims=True)
            e = jnp.exp2(s - m)
            l = jnp.sum(e, axis=-1, keepdims=True)
            e_ref[slot, j] = e.astype(BF16)
            l_ref[slot, j] = jnp.broadcast_to(1.0 / l, (LANES, LANES))

    def values(t, slot, js):
        for j in js:
            r = t * NA_TICK_ROWS + j
            key_start, _ = _na_window(r)
            vb = v_ref[pl.ds(key_start, NA_KEYS), :]
            o2 = jnp.dot(e_ref[slot, j], vb, preferred_element_type=F32) * l_ref[slot, j]
            o = jnp.where(first_head, o2[:GRID_W], o2[GRID_W:])
            o_ref[pl.ds(_na_row_start(r), GRID_W), :] = o.astype(BF16)

    def tick(t, parity, do_scores=True, do_values=True, do_softmax=True):
        for j in range(NA_TICK_ROWS):
            if do_values:
                values(t - 2, parity, (j,))
            if do_scores:
                scores(t, parity, (j,))
            if do_softmax:
                softmax(1 - parity, (j,))

    tick(0, 0, do_values=False, do_softmax=False)
    tick(1, 1, do_values=False)

    def body(u, carry):
        for i in range(NA_LOOP_TICKS):
            tick(2 + NA_LOOP_TICKS * u + i, i % 2)
        return carry

    n_iter = (NA_TICKS - 2) // NA_LOOP_TICKS
    lax.fori_loop(0, n_iter, body, 0)
    for t in range(2 + n_iter * NA_LOOP_TICKS, NA_TICKS):
        tick(t, t % 2)
    tick(NA_TICKS, 0, do_scores=False)
    tick(NA_TICKS + 1, 1, do_scores=False, do_softmax=False)


def _na(q3, k3, v3, bias):
    b = q3.shape[0]
    n_pairs = NA_HEADS // NA_HEADS_PER_STEP
    seq_spec = pl.BlockSpec((None, SEQ, LANES), lambda i, j: (i, 0, j))
    return pl.pallas_call(
        _na_kernel,
        out_shape=jax.ShapeDtypeStruct((b, SEQ, NA_DIM), BF16),
        grid=(b, n_pairs),
        in_specs=[seq_spec, seq_spec, seq_spec,
                  pl.BlockSpec((None, NA_BIAS_ROWS, LANES, LANES), lambda i, j: (j, 0, 0, 0))],
        out_specs=seq_spec,
        scratch_shapes=[pltpu.VMEM((2, NA_TICK_ROWS, LANES, NA_KEYS), F32),
                        pltpu.VMEM((2, NA_TICK_ROWS, LANES, NA_KEYS), BF16),
                        pltpu.VMEM((2, NA_TICK_ROWS, LANES, LANES), F32)],
        compiler_params=_params(2),
        name="na_attention",
    )(q3, k3, v3, bias)


def _swap_leading(x):
    return pltpu.einshape("abc->bac", x)


def _seq_dft_kernel(a_ref, b_ref, g_ref, h_ref, f_ref, t1_ref, t2_ref):
    cols = a_ref.shape[-1]
    t1_ref[:, :FFT_N1, :] = _swap_leading(a_ref[...].reshape(FFT_N1, FFT_N2, cols))
    t1_ref[:, FFT_N1:, :] = _swap_leading(b_ref[...].reshape(FFT_N1, FFT_N2, cols))

    for j in range(FFT_N2):
        t2_ref[j] = jnp.dot(g_ref[j], t1_ref[j], preferred_element_type=F32).astype(BF16)
    t1_ref[...] = _swap_leading(t2_ref[...])

    for k1 in range(FFT_N1):
        x = jnp.concatenate([t1_ref[k1], t1_ref[FFT_N1 + k1]], axis=0)
        t2_ref[k1] = jnp.dot(h_ref[...], x, preferred_element_type=F32).astype(BF16)
    f_ref[...] = _swap_leading(t2_ref[:FFT_N1]).reshape(SEQ, cols)


def _seq_dft_real(a3, b3, consts):
    b = a3.shape[1]
    slab = pl.BlockSpec((None, None, SEQ, FFT_COLS), lambda i, j: (j, i, 0, 0))
    work = pltpu.VMEM((FFT_N2, 2 * FFT_N1, FFT_COLS), BF16)
    return pl.pallas_call(
        _seq_dft_kernel,
        out_shape=jax.ShapeDtypeStruct((FFT_SLABS, b, SEQ, FFT_COLS), BF16),
        grid=(b, FFT_SLABS),
        in_specs=[slab, slab, _const_spec((FFT_N2, 2 * FFT_N1, 2 * FFT_N1)), _const_spec((FFT_N2, 2 * FFT_N2))],
        out_specs=slab,
        scratch_shapes=[work, work],
        compiler_params=_params(2),
        name="seq_dft",
    )(a3, b3, consts["dft_seq1"], consts["dft_seq2"])


def _memkv_kernel(mem_ref, g_ref, w_ref, gk_ref, e_ref, k_ref, v_ref):
    h = _rms(mem_ref[...], g_ref[...]).astype(BF16)
    kv = jnp.dot(h, w_ref[...], preferred_element_type=F32)
    k = _head_rmsnorm(kv[:, :MEM_DIM], gk_ref[...], e_ref[...])
    k_ref[...] = k.T.astype(BF16)
    v_ref[...] = kv[:, MEM_DIM:].astype(BF16)


def _memkv(mem, g, w, gk, consts):
    b = mem.shape[0]
    out_blk = pl.BlockSpec((None, N_MEM, MEM_DIM), lambda i: (i, 0, 0))
    return pl.pallas_call(
        _memkv_kernel,
        out_shape=(jax.ShapeDtypeStruct((b, N_MEM, MEM_DIM), BF16),) * 2,
        grid=(b,),
        in_specs=[
            pl.BlockSpec((None, N_MEM, D_MODEL), lambda i: (i, 0, 0)),
            _const_spec((1, D_MODEL)),
            _const_spec((D_MODEL, 2 * MEM_DIM)),
            _const_spec((1, MEM_DIM)),
            _const_spec((MXU_WIDTH, MXU_WIDTH)),
        ],
        out_specs=(out_blk, out_blk),
        compiler_params=_params(1),
        name="memory_kv",
    )(mem, g, w, gk, consts["e_blk"])


def _mixout_kernel(x_ref, a_ref, qm_ref, km_ref, vm_ref, w_ref, o_ref, *, slab_major):
    km = km_ref[...]
    vm = vm_ref[...]
    for s in range(LIGHT_SUBTILES):
        rows = slice(s * SUBTILE_ROWS, (s + 1) * SUBTILE_ROWS)
        qm = qm_ref[rows]
        zero = jnp.zeros_like(qm)
        mo = jnp.zeros(qm.shape, F32)
        for head in range(MEM_HEADS):
            in_head = _head_lane_mask(qm.shape, head, 1)
            sc = jnp.dot(jnp.where(in_head, qm, zero), km, preferred_element_type=F32)
            e = jnp.exp(sc - jnp.max(sc, axis=-1, keepdims=True))
            l = jnp.sum(e, axis=-1, keepdims=True)
            oh = jnp.dot(e.astype(BF16), vm, preferred_element_type=F32) * (1.0 / l)
            mo = jnp.where(in_head, oh, mo)
        y = jnp.dot(mo.astype(BF16), w_ref[D_MODEL - MEM_DIM:, :], preferred_element_type=F32)
        if slab_major:
            for c in range(FFT_SLABS):
                y = y + jnp.dot(a_ref[c, rows], w_ref[c * FFT_COLS:(c + 1) * FFT_COLS, :],
                                preferred_element_type=F32)
        else:
            y = y + jnp.dot(a_ref[rows], w_ref[:D_MODEL - MEM_DIM, :], preferred_element_type=F32)
        o_ref[rows] = x_ref[rows] + y


def _mixout(x3, a3, qm3, km, vm, w_out):
    b = x3.shape[0]
    tm = LIGHT_TILE
    tok = lambda n: pl.BlockSpec((None, tm, n), lambda i, j: (i, j, 0))
    slab_major = a3.ndim == 4
    a_spec = (pl.BlockSpec((FFT_SLABS, None, tm, FFT_COLS), lambda i, j: (0, i, j, 0)) if slab_major
              else tok(a3.shape[-1]))
    mem_blk = pl.BlockSpec((None, N_MEM, MEM_DIM), lambda i, j: (i, 0, 0))
    return pl.pallas_call(
        functools.partial(_mixout_kernel, slab_major=slab_major),
        out_shape=jax.ShapeDtypeStruct(x3.shape, F32),
        grid=(b, SEQ // tm),
        in_specs=[tok(D_MODEL), a_spec, tok(MEM_DIM), mem_blk, mem_blk,
                  _const_spec((D_MODEL, D_MODEL))],
        out_specs=tok(D_MODEL),
        compiler_params=_params(2),
        name="mixer_out",
    )(x3, a3, qm3, km, vm, w_out)


def _head_mean_matrix():
    head = np.arange(MXU_WIDTH) // HEAD_DIM
    return jnp.asarray((head[:, None] == head[None, :]).astype(np.float32) / HEAD_DIM, BF16)


def _dft_constants():
    d = np.arange(FOURIER_DIM)
    ang = 2.0 * np.pi * ((d[:, None] * d[None, :]) % FOURIER_DIM) / FOURIER_DIM
    feat = np.concatenate([np.cos(ang), np.sin(ang)], axis=1) / np.sqrt(FOURIER_DIM)
    k1 = np.arange(FFT_N1)
    n1 = np.arange(FFT_N1)
    n2 = np.arange(FFT_N2)
    n = FFT_N2 * n1[None, None, :] + n2[:, None, None]
    ang1 = 2.0 * np.pi * ((k1[None, :, None] * n) % SEQ) / SEQ
    c1, s1 = np.cos(ang1) / np.sqrt(FFT_N1), np.sin(ang1) / np.sqrt(FFT_N1)
    seq1 = np.concatenate([np.concatenate([c1, -s1], axis=2), np.concatenate([s1, c1], axis=2)], axis=1)
    k2 = np.arange(FFT_N2)
    ang2 = 2.0 * np.pi * ((k2[:, None] * n2[None, :]) % FFT_N2) / FFT_N2
    seq2 = np.concatenate([np.cos(ang2), -np.sin(ang2)], axis=1) / np.sqrt(FFT_N2)
    return (jnp.asarray(feat, BF16), jnp.asarray(seq1, BF16), jnp.asarray(seq2, BF16))


def _constants():
    feat, seq1, seq2 = _dft_constants()
    return dict(e_blk=_head_mean_matrix(), dft_feat=feat, dft_seq1=seq1, dft_seq2=seq2)


def _na_bias_table(rpb):
    c = np.arange(GRID_W)
    cs = np.clip(c - KERNEL_COLS // 2, 0, GRID_W - KERNEL_COLS)
    inside = (c[None, :] >= cs[:, None]) & (c[None, :] < cs[:, None] + KERNEL_COLS)
    r = rpb.astype(F32) * LOG2_E
    pad = GRID_W - KERNEL_COLS
    r = jnp.concatenate([jnp.repeat(r[..., :1], pad, axis=-1), r, jnp.repeat(r[..., -1:], pad, axis=-1)], axis=-1)
    t = jnp.stack([r[..., GRID_W - 1 - q:2 * GRID_W - 1 - q] for q in range(GRID_W)], axis=-2)
    t = jnp.where(jnp.asarray(inside)[None, None], t, NEG_INF)
    t = jnp.concatenate([t[:, :-1], t[:, 1:]], axis=-1)
    n_pairs = NA_HEADS // NA_HEADS_PER_STEP
    t = t.reshape(n_pairs, NA_HEADS_PER_STEP, NA_BIAS_ROWS, GRID_W, LANES).transpose(0, 2, 1, 3, 4)
    return t.reshape(n_pairs, NA_BIAS_ROWS, LANES, LANES)


def _row(v):
    return v.reshape(1, -1).astype(F32)


def _tile_heads(g, heads):
    return jnp.tile(g.astype(F32), heads).reshape(1, -1)


def _trunk(x, mem, p, consts):
    b = x.shape[0]
    x2 = x.reshape(b * SEQ, D_MODEL)
    for i in range(DEPTH):
        x2 = _ffn(x2, p["norm_ffn1"][i], *p["ffn1"], i, p["norm_out"][i], False)
        km, vm = _memkv(mem, p["norm_mem"][i], p["w_mem_kv"][i], p["mem_k_norm"][i], consts)
        if i % 2 == 0:
            q, k, v, qm = _mixin_a(x2, p["norm_mix"][i], p["w_in_a"], p["na_q_norm"], p["na_k_norm"],
                                   p["mem_q_norm"][i], consts)
            seq3 = lambda t: t.reshape(b, SEQ, t.shape[-1])
            mixed = _na(seq3(q), seq3(k), seq3(v), p["na_bias"])
            w_out = p["w_out_a"]
        else:
            a, bb, qm = _mixin_b(x2, p["norm_mix"][i], p["w_in_b"], p["mem_q_norm"][i], consts)
            slabs = lambda t: t.reshape(FFT_SLABS, b, SEQ, FFT_COLS)
            mixed = _seq_dft_real(slabs(a), slabs(bb), consts)
            w_out = p["w_out_b"]
        x3 = _mixout(x2.reshape(b, SEQ, D_MODEL), mixed, qm.reshape(b, SEQ, MEM_DIM), km, vm, w_out)
        x2 = x3.reshape(b * SEQ, D_MODEL)
        x2 = _ffn(x2, p["norm_ffn2"][i], *p["ffn2"], i, p["norm_out"][i], True)
    return x2.reshape(b, SEQ, D_MODEL)


def kernel(x_prompt, x_sample, mem_prompt, mem_sample, norm_ffn1, w_ffn1_in, w_ffn1_out, norm_mix, norm_mem,
           w_mem_kv, mem_q_norm, mem_k_norm, w_in_a, na_q_norm, na_k_norm, na_rpb, w_out_a, w_in_b, w_out_b,
           norm_ffn2, w_ffn2_in, w_ffn2_out, norm_out):
    assert x_prompt.shape[1:] == (SEQ, D_MODEL) and x_sample.shape[1:] == (SEQ, D_MODEL)
    consts = _constants()
    p = dict(
        norm_ffn1=[_row(norm_ffn1[i]) for i in range(DEPTH)],
        norm_ffn2=[_row(norm_ffn2[i]) for i in range(DEPTH)],
        norm_mix=[_row(norm_mix[i]) for i in range(DEPTH)],
        norm_mem=[_row(norm_mem[i]) for i in range(DEPTH)],
        norm_out=[_row(norm_out[i]) for i in range(DEPTH)],
        ffn1=(w_ffn1_in.astype(BF16), w_ffn1_out.astype(BF16)),
        ffn2=(w_ffn2_in.astype(BF16), w_ffn2_out.astype(BF16)),
        w_mem_kv=[w_mem_kv[i].astype(BF16) for i in range(DEPTH)],
        mem_q_norm=[_tile_heads(mem_q_norm[i], MEM_HEADS) for i in range(DEPTH)],
        mem_k_norm=[_tile_heads(mem_k_norm[i], MEM_HEADS) for i in range(DEPTH)],
        w_in_a=w_in_a[0].astype(BF16),
        na_q_norm=_tile_heads(na_q_norm[0], NA_HEADS),
        na_k_norm=_tile_heads(na_k_norm[0], NA_HEADS),
        na_bias=_na_bias_table(na_rpb[0]),
        w_out_a=w_out_a[0].astype(BF16),
        w_in_b=w_in_b[0].astype(BF16),
        w_out_b=w_out_b[0].astype(BF16),
    )
    return (_trunk(x_prompt, mem_prompt, p, consts), _trunk(x_sample, mem_sample, p, consts))
```

```python
import functools

import numpy as np
import jax
import jax.numpy as jnp
from jax import lax
from jax.experimental import pallas as pl
from jax.experimental.pallas import tpu as pltpu

D_MODEL = 1024
DEPTH = 2
HEAD_DIM = 64
MEM_HEADS = 4
MEM_DIM = MEM_HEADS * HEAD_DIM
N_MEM = 256
NA_HEADS = 12
NA_DIM = NA_HEADS * HEAD_DIM
FOURIER_DIM = 768
GRID_W = 64
KERNEL_ROWS = 8
KERNEL_COLS = 16
D_FF = 2816
EPS = 1e-6
NEG_INF = -1e30
LOG2_E = 1.4426950408889634
SEQ = 8192
ROWS = SEQ // GRID_W
FFT_N1 = 64
FFT_N2 = 128

F32 = jnp.float32
BF16 = jnp.bfloat16

LANES = 128
MXU_WIDTH = 256
FFT_COLS = MXU_WIDTH
FFT_SLABS = FOURIER_DIM // FFT_COLS
FF_CHUNK = MXU_WIDTH
N_FF_CHUNKS = D_FF // FF_CHUNK
SUBTILE_ROWS = 512
SUBTILES = 2
TOKEN_TILE = SUBTILES * SUBTILE_ROWS
LIGHT_SUBTILES = 4
LIGHT_TILE = LIGHT_SUBTILES * SUBTILE_ROWS
VMEM_LIMIT = 56 * 1024 * 1024


def _params(n_axes):
    return pltpu.CompilerParams(dimension_semantics=("arbitrary",) * n_axes,
                                vmem_limit_bytes=VMEM_LIMIT)


def _const_spec(shape):
    n = len(shape)
    return pl.BlockSpec(shape, lambda *_: (0,) * n, pipeline_mode=pl.Buffered(1))


def _layer_spec(shape, layer):
    n = len(shape)
    return pl.BlockSpec((None,) + tuple(shape), lambda *_: (layer,) + (0,) * n, pipeline_mode=pl.Buffered(1))


def _rms(x, g):
    ms = jnp.mean(x * x, axis=-1, keepdims=True)
    return x * lax.rsqrt(ms + EPS) * g


def _head_rmsnorm(t, gain, e_blk):
    sq = (t * t).astype(BF16)
    slabs = [jnp.dot(sq[:, c:c + MXU_WIDTH], e_blk, preferred_element_type=F32)
             for c in range(0, t.shape[-1], MXU_WIDTH)]
    ms = slabs[0] if len(slabs) == 1 else jnp.concatenate(slabs, axis=-1)
    return t * lax.rsqrt(ms + EPS) * gain


def _head_lane_mask(shape, head, lane_axis):
    lane = lax.broadcasted_iota(jnp.int32, shape, lane_axis)
    return (lane >= head * HEAD_DIM) & (lane < (head + 1) * HEAD_DIM)


def _ffn_body(load_x, g_ref, win_ref, wout_ref, gout_ref, o_ref, xn_ref, acc_ref, final_norm):
    for s in range(SUBTILES):
        rows = slice(s * SUBTILE_ROWS, (s + 1) * SUBTILE_ROWS)
        xn_ref[rows] = _rms(load_x(rows), g_ref[...]).astype(BF16)
        for c in range(N_FF_CHUNKS):
            cols = slice(c * FF_CHUNK, (c + 1) * FF_CHUNK)
            up_cols = slice(D_FF + c * FF_CHUNK, D_FF + (c + 1) * FF_CHUNK)
            gate = jnp.dot(xn_ref[rows], win_ref[:, cols], preferred_element_type=F32)
            up = jnp.dot(xn_ref[rows], win_ref[:, up_cols], preferred_element_type=F32)
            act = (gate * jax.nn.sigmoid(gate) * up).astype(BF16)
            part = jnp.dot(act, wout_ref[cols, :], preferred_element_type=F32)
            if c == 0:
                acc_ref[rows] = part
            elif c < N_FF_CHUNKS - 1:
                acc_ref[rows] += part
            else:
                y = load_x(rows) + 0.5 * (acc_ref[rows] + part)
        if final_norm:
            y = _rms(y, gout_ref[...])
        o_ref[rows] = y


def _ffn_kernel(x_ref, g_ref, win_ref, wout_ref, gout_ref, o_ref, xn_ref, acc_ref, *, final_norm):
    _ffn_body(lambda rows: x_ref[rows], g_ref, win_ref, wout_ref, gout_ref, o_ref, xn_ref, acc_ref, final_norm)


def _ffn_two_inputs_kernel(xa_ref, xb_ref, g_ref, win_ref, wout_ref, gout_ref, o_ref, xn_ref, acc_ref, *,
                           final_norm, a_tiles):
    from_a = pl.program_id(0) < a_tiles
    _ffn_body(lambda rows: jnp.where(from_a, xa_ref[rows], xb_ref[rows]),
              g_ref, win_ref, wout_ref, gout_ref, o_ref, xn_ref, acc_ref, final_norm)


def _ffn(x2d, g, w_in, w_out, layer, gout, final_norm, *, second=None, tiles=None):
    tm = TOKEN_TILE
    first_tile, n_tiles = tiles if tiles is not None else (0, x2d.shape[0] // tm)
    tok = pl.BlockSpec((tm, D_MODEL), lambda i: (i, 0))
    weights = [_const_spec((1, D_MODEL)), _layer_spec((D_MODEL, 2 * D_FF), layer),
               _layer_spec((D_FF, D_MODEL), layer), _const_spec((1, D_MODEL))]
    if second is None:
        body = functools.partial(_ffn_kernel, final_norm=final_norm)
        x_specs = [pl.BlockSpec((tm, D_MODEL), lambda i: (i + first_tile, 0))]
        xs = (x2d,)
    else:
        a_tiles = n_tiles
        n_tiles = a_tiles + second.shape[0] // tm
        body = functools.partial(_ffn_two_inputs_kernel, final_norm=final_norm, a_tiles=a_tiles)
        x_specs = [pl.BlockSpec((tm, D_MODEL), lambda i: (jnp.minimum(i, a_tiles - 1), 0)),
                   pl.BlockSpec((tm, D_MODEL), lambda i: (jnp.maximum(i - a_tiles, 0), 0))]
        xs = (x2d, second)
    return pl.pallas_call(
        body,
        out_shape=jax.ShapeDtypeStruct((n_tiles * tm, D_MODEL), F32),
        grid=(n_tiles,),
        in_specs=x_specs + weights,
        out_specs=tok,
        scratch_shapes=[pltpu.VMEM((tm, D_MODEL), BF16), pltpu.VMEM((tm, D_MODEL), F32)],
        compiler_params=_params(1),
        name="ffn_final" if final_norm else "ffn",
    )(*xs, g, w_in, w_out, gout)


def _mixin_a_kernel(x_ref, g_ref, w_ref, gq_ref, gk_ref, gqm_ref, e_ref, q_ref, k_ref, v_ref, qm_ref):
    scale = HEAD_DIM ** -0.5
    e_blk = e_ref[...]
    for s in range(LIGHT_SUBTILES):
        rows = slice(s * SUBTILE_ROWS, (s + 1) * SUBTILE_ROWS)
        h = _rms(x_ref[rows], g_ref[...]).astype(BF16)
        proj = lambda lo, hi: jnp.dot(h, w_ref[:, lo:hi], preferred_element_type=F32)
        q = proj(0, NA_DIM)
        k = proj(NA_DIM, 2 * NA_DIM)
        q_ref[rows] = (_head_rmsnorm(q, gq_ref[...], e_blk) * (scale * LOG2_E)).astype(BF16)
        qm = proj(3 * NA_DIM, 3 * NA_DIM + MEM_DIM)
        k_ref[rows] = _head_rmsnorm(k, gk_ref[...], e_blk).astype(BF16)
        v = proj(2 * NA_DIM, 3 * NA_DIM)
        qm_ref[rows] = (_head_rmsnorm(qm, gqm_ref[...], e_blk) * scale).astype(BF16)
        v_ref[rows] = v.astype(BF16)


def _mixin_a(x2d, g, w, gq, gk, gqm, consts):
    t = x2d.shape[0]
    tm = LIGHT_TILE
    width = 3 * NA_DIM + MEM_DIM
    tok = lambda n: pl.BlockSpec((tm, n), lambda i: (i, 0))
    return pl.pallas_call(
        _mixin_a_kernel,
        out_shape=(jax.ShapeDtypeStruct((t, NA_DIM), BF16),) * 3 + (jax.ShapeDtypeStruct((t, MEM_DIM), BF16),),
        grid=(t // tm,),
        in_specs=[
            tok(D_MODEL),
            _const_spec((1, D_MODEL)),
            _const_spec((D_MODEL, width)),
            _const_spec((1, NA_DIM)), _const_spec((1, NA_DIM)), _const_spec((1, MEM_DIM)),
            _const_spec((MXU_WIDTH, MXU_WIDTH)),
        ],
        out_specs=(tok(NA_DIM), tok(NA_DIM), tok(NA_DIM), tok(MEM_DIM)),
        compiler_params=_params(1),
        name="mixin_na",
    )(x2d, g, w, gq, gk, gqm, consts["e_blk"])


def _mixin_b_kernel(x_ref, g_ref, w_ref, gqm_ref, e_ref, dft_ref, a_ref, b_ref, qm_ref):
    scale = HEAD_DIM ** -0.5
    for s in range(LIGHT_SUBTILES):
        rows = slice(s * SUBTILE_ROWS, (s + 1) * SUBTILE_ROWS)
        h = _rms(x_ref[rows], g_ref[...]).astype(BF16)
        proj = jnp.dot(h, w_ref[...], preferred_element_type=F32)
        z = proj[:, :FOURIER_DIM].astype(BF16)
        qm = proj[:, FOURIER_DIM:]
        ab = jnp.dot(z, dft_ref[...], preferred_element_type=F32)
        for c in range(FFT_SLABS):
            a_ref[c, rows] = ab[:, c * FFT_COLS:(c + 1) * FFT_COLS].astype(BF16)
            b_ref[c, rows] = ab[:, FOURIER_DIM + c * FFT_COLS:FOURIER_DIM + (c + 1) * FFT_COLS].astype(BF16)
        qm_ref[rows] = (_head_rmsnorm(qm, gqm_ref[...], e_ref[...]) * scale).astype(BF16)


def _mixin_b(x2d, g, w, gqm, consts):
    t = x2d.shape[0]
    tm = LIGHT_TILE
    tok = lambda n: pl.BlockSpec((tm, n), lambda i: (i, 0))
    slab = pl.BlockSpec((FFT_SLABS, tm, FFT_COLS), lambda i: (0, i, 0))
    return pl.pallas_call(
        _mixin_b_kernel,
        out_shape=(jax.ShapeDtypeStruct((FFT_SLABS, t, FFT_COLS), BF16),) * 2
        + (jax.ShapeDtypeStruct((t, MEM_DIM), BF16),),
        grid=(t // tm,),
        in_specs=[
            tok(D_MODEL),
            _const_spec((1, D_MODEL)),
            _const_spec((D_MODEL, FOURIER_DIM + MEM_DIM)),
            _const_spec((1, MEM_DIM)),
            _const_spec((MXU_WIDTH, MXU_WIDTH)),
            _const_spec((FOURIER_DIM, 2 * FOURIER_DIM)),
        ],
        out_specs=(slab, slab, tok(MEM_DIM)),
        compiler_params=_params(1),
        name="mixin_fnet",
    )(x2d, g, w, gqm, consts["e_blk"], consts["dft_feat"])


NA_HEADS_PER_STEP = LANES // HEAD_DIM
NA_KEYS = KERNEL_ROWS * GRID_W
NA_BIAS_ROWS = 2 * KERNEL_ROWS - 2
NA_TICK_ROWS = 2
NA_TICKS = ROWS // NA_TICK_ROWS
NA_LOOP_TICKS = 62


def _na_window(r):
    if isinstance(r, int):
        rs = min(max(r - KERNEL_ROWS // 2, 0), ROWS - KERNEL_ROWS)
        return rs * GRID_W, rs - r + (KERNEL_ROWS - 1)
    rs = jnp.clip(r - KERNEL_ROWS // 2, 0, ROWS - KERNEL_ROWS)
    return pl.multiple_of(rs * GRID_W, GRID_W), rs - r + (KERNEL_ROWS - 1)


def _na_row_start(r):
    return r * GRID_W if isinstance(r, int) else pl.multiple_of(r * GRID_W, GRID_W)


def _na_kernel(q_ref, k_ref, v_ref, bias_ref, o_ref, s_ref, e_ref, l_ref):
    first_head = _head_lane_mask((GRID_W, LANES), 0, 1)

    def scores(t, slot, js):
        for j in js:
            r = t * NA_TICK_ROWS + j
            key_start, shift = _na_window(r)
            q = q_ref[pl.ds(_na_row_start(r), GRID_W), :]
            zero = jnp.zeros_like(q)
            q2 = jnp.concatenate([jnp.where(first_head, q, zero), jnp.where(first_head, zero, q)], axis=0)
            kb = k_ref[pl.ds(key_start, NA_KEYS), :]
            s = lax.dot_general(q2, kb, (((1,), (1,)), ((), ())), preferred_element_type=F32)
            bias = jnp.concatenate([bias_ref[shift + 2 * i] for i in range(KERNEL_ROWS // 2)], axis=1)
            s_ref[slot, j] = s + bias

    def softmax(slot, js):
        for j in js:
            s = s_ref[slot, j]
            m = jnp.max(s, axis=-1, keepdims=True)
            e = jnp.exp2(s - m)
            l = jnp.sum(e, axis=-1, keepdims=True)
            e_ref[slot, j] = e.astype(BF16)
            l_ref[slot, j] = jnp.broadcast_to(1.0 / l, (LANES, LANES))

    def values(t, slot, js):
        for j in js:
            r = t * NA_TICK_ROWS + j
            key_start, _ = _na_window(r)
            vb = v_ref[pl.ds(key_start, NA_KEYS), :]
            o2 = jnp.dot(e_ref[slot, j], vb, preferred_element_type=F32) * l_ref[slot, j]
            o = jnp.where(first_head, o2[:GRID_W], o2[GRID_W:])
            o_ref[pl.ds(_na_row_start(r), GRID_W), :] = o.astype(BF16)

    def tick(t, parity, do_scores=True, do_values=True, do_softmax=True):
        for j in range(NA_TICK_ROWS):
            if do_values:
                values(t - 2, parity, (j,))
            if do_scores:
                scores(t, parity, (j,))
            if do_softmax:
                softmax(1 - parity, (j,))

    tick(0, 0, do_values=False, do_softmax=False)
    tick(1, 1, do_values=False)

    def body(u, carry):
        for i in range(NA_LOOP_TICKS):
            tick(2 + NA_LOOP_TICKS * u + i, i % 2)
        return carry

    n_iter = (NA_TICKS - 2) // NA_LOOP_TICKS
    lax.fori_loop(0, n_iter, body, 0)
    for t in range(2 + n_iter * NA_LOOP_TICKS, NA_TICKS):
        tick(t, t % 2)
    tick(NA_TICKS, 0, do_scores=False)
    tick(NA_TICKS + 1, 1, do_scores=False, do_softmax=False)


def _na(q3, k3, v3, bias):
    b = q3.shape[0]
    n_pairs = NA_HEADS // NA_HEADS_PER_STEP
    seq_spec = pl.BlockSpec((None, SEQ, LANES), lambda i, j: (i, 0, j))
    return pl.pallas_call(
        _na_kernel,
        out_shape=jax.ShapeDtypeStruct((b, SEQ, NA_DIM), BF16),
        grid=(b, n_pairs),
        in_specs=[seq_spec, seq_spec, seq_spec,
                  pl.BlockSpec((None, NA_BIAS_ROWS, LANES, LANES), lambda i, j: (j, 0, 0, 0))],
        out_specs=seq_spec,
        scratch_shapes=[pltpu.VMEM((2, NA_TICK_ROWS, LANES, NA_KEYS), F32),
                        pltpu.VMEM((2, NA_TICK_ROWS, LANES, NA_KEYS), BF16),
                        pltpu.VMEM((2, NA_TICK_ROWS, LANES, LANES), F32)],
        compiler_params=_params(2),
        name="na_attention",
    )(q3, k3, v3, bias)


def _swap_leading(x):
    return pltpu.einshape("abc->bac", x)


def _seq_dft_kernel(a_ref, b_ref, g_ref, h_ref, f_ref, t1_ref, t2_ref):
    cols = a_ref.shape[-1]
    t1_ref[:, :FFT_N1, :] = _swap_leading(a_ref[...].reshape(FFT_N1, FFT_N2, cols))
    t1_ref[:, FFT_N1:, :] = _swap_leading(b_ref[...].reshape(FFT_N1, FFT_N2, cols))

    for j in range(FFT_N2):
        t2_ref[j] = jnp.dot(g_ref[j], t1_ref[j], preferred_element_type=F32).astype(BF16)
    t1_ref[...] = _swap_leading(t2_ref[...])

    for k1 in range(FFT_N1):
        x = jnp.concatenate([t1_ref[k1], t1_ref[FFT_N1 + k1]], axis=0)
        t2_ref[k1] = jnp.dot(h_ref[...], x, preferred_element_type=F32).astype(BF16)
    f_ref[...] = _swap_leading(t2_ref[:FFT_N1]).reshape(SEQ, cols)


def _seq_dft_real(a3, b3, consts):
    b = a3.shape[1]
    slab = pl.BlockSpec((None, None, SEQ, FFT_COLS), lambda i, j: (j, i, 0, 0))
    work = pltpu.VMEM((FFT_N2, 2 * FFT_N1, FFT_COLS), BF16)
    return pl.pallas_call(
        _seq_dft_kernel,
        out_shape=jax.ShapeDtypeStruct((FFT_SLABS, b, SEQ, FFT_COLS), BF16),
        grid=(b, FFT_SLABS),
        in_specs=[slab, slab, _const_spec((FFT_N2, 2 * FFT_N1, 2 * FFT_N1)), _const_spec((FFT_N2, 2 * FFT_N2))],
        out_specs=slab,
        scratch_shapes=[work, work],
        compiler_params=_params(2),
        name="seq_dft",
    )(a3, b3, consts["dft_seq1"], consts["dft_seq2"])


def _memkv_kernel(mem_ref, g_ref, w_ref, gk_ref, e_ref, k_ref, v_ref):
    h = _rms(mem_ref[...], g_ref[...]).astype(BF16)
    kv = jnp.dot(h, w_ref[...], preferred_element_type=F32)
    k = _head_rmsnorm(kv[:, :MEM_DIM], gk_ref[...], e_ref[...])
    k_ref[...] = k.T.astype(BF16)
    v_ref[...] = kv[:, MEM_DIM:].astype(BF16)


def _memkv(mem, g, w, gk, consts):
    b = mem.shape[0]
    out_blk = pl.BlockSpec((None, N_MEM, MEM_DIM), lambda i: (i, 0, 0))
    return pl.pallas_call(
        _memkv_kernel,
        out_shape=(jax.ShapeDtypeStruct((b, N_MEM, MEM_DIM), BF16),) * 2,
        grid=(b,),
        in_specs=[
            pl.BlockSpec((None, N_MEM, D_MODEL), lambda i: (i, 0, 0)),
            _const_spec((1, D_MODEL)),
            _const_spec((D_MODEL, 2 * MEM_DIM)),
            _const_spec((1, MEM_DIM)),
            _const_spec((MXU_WIDTH, MXU_WIDTH)),
        ],
        out_specs=(out_blk, out_blk),
        compiler_params=_params(1),
        name="memory_kv",
    )(mem, g, w, gk, consts["e_blk"])


def _mixout_kernel(x_ref, a_ref, qm_ref, km_ref, vm_ref, w_ref, o_ref, *, slab_major):
    km = km_ref[...]
    vm = vm_ref[...]
    for s in range(LIGHT_SUBTILES):
        rows = slice(s * SUBTILE_ROWS, (s + 1) * SUBTILE_ROWS)
        qm = qm_ref[rows]
        zero = jnp.zeros_like(qm)
        mo = jnp.zeros(qm.shape, F32)
        for head in range(MEM_HEADS):
            in_head = _head_lane_mask(qm.shape, head, 1)
            sc = jnp.dot(jnp.where(in_head, qm, zero), km, preferred_element_type=F32)
            e = jnp.exp(sc - jnp.max(sc, axis=-1, keepdims=True))
            l = jnp.sum(e, axis=-1, keepdims=True)
            oh = jnp.dot(e.astype(BF16), vm, preferred_element_type=F32) * (1.0 / l)
            mo = jnp.where(in_head, oh, mo)
        y = jnp.dot(mo.astype(BF16), w_ref[D_MODEL - MEM_DIM:, :], preferred_element_type=F32)
        if slab_major:
            for c in range(FFT_SLABS):
                y = y + jnp.dot(a_ref[c, rows], w_ref[c * FFT_COLS:(c + 1) * FFT_COLS, :],
                                preferred_element_type=F32)
        else:
            y = y + jnp.dot(a_ref[rows], w_ref[:D_MODEL - MEM_DIM, :], preferred_element_type=F32)
        o_ref[rows] = x_ref[rows] + y


def _mixout(x3, a3, qm3, km, vm, w_out):
    b = x3.shape[0]
    tm = LIGHT_TILE
    tok = lambda n: pl.BlockSpec((None, tm, n), lambda i, j: (i, j, 0))
    slab_major = a3.ndim == 4
    a_spec = (pl.BlockSpec((FFT_SLABS, None, tm, FFT_COLS), lambda i, j: (0, i, j, 0)) if slab_major
              else tok(a3.shape[-1]))
    mem_blk = pl.BlockSpec((None, N_MEM, MEM_DIM), lambda i, j: (i, 0, 0))
    return pl.pallas_call(
        functools.partial(_mixout_kernel, slab_major=slab_major),
        out_shape=jax.ShapeDtypeStruct(x3.shape, F32),
        grid=(b, SEQ // tm),
        in_specs=[tok(D_MODEL), a_spec, tok(MEM_DIM), mem_blk, mem_blk,
                  _const_spec((D_MODEL, D_MODEL))],
        out_specs=tok(D_MODEL),
        compiler_params=_params(2),
        name="mixer_out",
    )(x3, a3, qm3, km, vm, w_out)


def _head_mean_matrix():
    head = np.arange(MXU_WIDTH) // HEAD_DIM
    return jnp.asarray((head[:, None] == head[None, :]).astype(np.float32) / HEAD_DIM, BF16)


def _dft_constants():
    d = np.arange(FOURIER_DIM)
    ang = 2.0 * np.pi * ((d[:, None] * d[None, :]) % FOURIER_DIM) / FOURIER_DIM
    feat = np.concatenate([np.cos(ang), np.sin(ang)], axis=1) / np.sqrt(FOURIER_DIM)
    k1 = np.arange(FFT_N1)
    n1 = np.arange(FFT_N1)
    n2 = np.arange(FFT_N2)
    n = FFT_N2 * n1[None, None, :] + n2[:, None, None]
    ang1 = 2.0 * np.pi * ((k1[None, :, None] * n) % SEQ) / SEQ
    c1, s1 = np.cos(ang1) / np.sqrt(FFT_N1), np.sin(ang1) / np.sqrt(FFT_N1)
    seq1 = np.concatenate([np.concatenate([c1, -s1], axis=2), np.concatenate([s1, c1], axis=2)], axis=1)
    k2 = np.arange(FFT_N2)
    ang2 = 2.0 * np.pi * ((k2[:, None] * n2[None, :]) % FFT_N2) / FFT_N2
    seq2 = np.concatenate([np.cos(ang2), -np.sin(ang2)], axis=1) / np.sqrt(FFT_N2)
    return (jnp.asarray(feat, BF16), jnp.asarray(seq1, BF16), jnp.asarray(seq2, BF16))


def _constants():
    feat, seq1, seq2 = _dft_constants()
    return dict(e_blk=_head_mean_matrix(), dft_feat=feat, dft_seq1=seq1, dft_seq2=seq2)


def _na_bias_table(rpb):
    c = np.arange(GRID_W)
    cs = np.clip(c - KERNEL_COLS // 2, 0, GRID_W - KERNEL_COLS)
    inside = (c[None, :] >= cs[:, None]) & (c[None, :] < cs[:, None] + KERNEL_COLS)
    r = rpb.astype(F32) * LOG2_E
    pad = GRID_W - KERNEL_COLS
    r = jnp.concatenate([jnp.repeat(r[..., :1], pad, axis=-1), r, jnp.repeat(r[..., -1:], pad, axis=-1)], axis=-1)
    t = jnp.stack([r[..., GRID_W - 1 - q:2 * GRID_W - 1 - q] for q in range(GRID_W)], axis=-2)
    t = jnp.where(jnp.asarray(inside)[None, None], t, NEG_INF)
    t = jnp.concatenate([t[:, :-1], t[:, 1:]], axis=-1)
    n_pairs = NA_HEADS // NA_HEADS_PER_STEP
    t = t.reshape(n_pairs, NA_HEADS_PER_STEP, NA_BIAS_ROWS, GRID_W, LANES).transpose(0, 2, 1, 3, 4)
    return t.reshape(n_pairs, NA_BIAS_ROWS, LANES, LANES)


def _row(v):
    return v.reshape(1, -1).astype(F32)


def _tile_heads(g, heads):
    return jnp.tile(g.astype(F32), heads).reshape(1, -1)


def _trunk(x_a, x_b, mem, p, consts):
    b_a, b = x_a.shape[0], x_a.shape[0] + x_b.shape[0]
    flat = lambda x: x.reshape(-1, D_MODEL)
    for i in range(DEPTH):
        if i == 0:
            x2 = _ffn(flat(x_a), p["norm_ffn1"][i], *p["ffn1"], i, p["norm_out"][i], False, second=flat(x_b))
        else:
            x2 = _ffn(x2, p["norm_ffn1"][i], *p["ffn1"], i, p["norm_out"][i], False)
        km, vm = _memkv(mem, p["norm_mem"][i], p["w_mem_kv"][i], p["mem_k_norm"][i], consts)
        if i % 2 == 0:
            q, k, v, qm = _mixin_a(x2, p["norm_mix"][i], p["w_in_a"], p["na_q_norm"], p["na_k_norm"],
                                   p["mem_q_norm"][i], consts)
            seq3 = lambda t: t.reshape(b, SEQ, t.shape[-1])
            mixed = _na(seq3(q), seq3(k), seq3(v), p["na_bias"])
            w_out = p["w_out_a"]
        else:
            a, bb, qm = _mixin_b(x2, p["norm_mix"][i], p["w_in_b"], p["mem_q_norm"][i], consts)
            slabs = lambda t: t.reshape(FFT_SLABS, b, SEQ, FFT_COLS)
            mixed = _seq_dft_real(slabs(a), slabs(bb), consts)
            w_out = p["w_out_b"]
        x3 = _mixout(x2.reshape(b, SEQ, D_MODEL), mixed, qm.reshape(b, SEQ, MEM_DIM), km, vm, w_out)
        x2 = x3.reshape(b * SEQ, D_MODEL)
        if i < DEPTH - 1:
            x2 = _ffn(x2, p["norm_ffn2"][i], *p["ffn2"], i, p["norm_out"][i], True)
    a_tiles = b_a * SEQ // TOKEN_TILE
    last = lambda tiles: _ffn(x2, p["norm_ffn2"][DEPTH - 1], *p["ffn2"], DEPTH - 1, p["norm_out"][DEPTH - 1], True,
                              tiles=tiles)
    y_a = last((0, a_tiles)).reshape(x_a.shape)
    y_b = last((a_tiles, b * SEQ // TOKEN_TILE - a_tiles)).reshape(x_b.shape)
    return y_a, y_b


def kernel(x_prompt, x_sample, mem_prompt, mem_sample, norm_ffn1, w_ffn1_in, w_ffn1_out, norm_mix, norm_mem,
           w_mem_kv, mem_q_norm, mem_k_norm, w_in_a, na_q_norm, na_k_norm, na_rpb, w_out_a, w_in_b, w_out_b,
           norm_ffn2, w_ffn2_in, w_ffn2_out, norm_out):
    assert x_prompt.shape[1:] == (SEQ, D_MODEL) and x_sample.shape[1:] == (SEQ, D_MODEL)
    consts = _constants()
    p = dict(
        norm_ffn1=[_row(norm_ffn1[i]) for i in range(DEPTH)],
        norm_ffn2=[_row(norm_ffn2[i]) for i in range(DEPTH)],
        norm_mix=[_row(norm_mix[i]) for i in range(DEPTH)],
        norm_mem=[_row(norm_mem[i]) for i in range(DEPTH)],
        norm_out=[_row(norm_out[i]) for i in range(DEPTH)],
        ffn1=(w_ffn1_in.astype(BF16), w_ffn1_out.astype(BF16)),
        ffn2=(w_ffn2_in.astype(BF16), w_ffn2_out.astype(BF16)),
        w_mem_kv=[w_mem_kv[i].astype(BF16) for i in range(DEPTH)],
        mem_q_norm=[_tile_heads(mem_q_norm[i], MEM_HEADS) for i in range(DEPTH)],
        mem_k_norm=[_tile_heads(mem_k_norm[i], MEM_HEADS) for i in range(DEPTH)],
        w_in_a=w_in_a[0].astype(BF16),
        na_q_norm=_tile_heads(na_q_norm[0], NA_HEADS),
        na_k_norm=_tile_heads(na_k_norm[0], NA_HEADS),
        na_bias=_na_bias_table(na_rpb[0]),
        w_out_a=w_out_a[0].astype(BF16),
        w_in_b=w_in_b[0].astype(BF16),
        w_out_b=w_out_b[0].astype(BF16),
    )
    return _trunk(x_prompt, x_sample, jnp.concatenate([mem_prompt, mem_sample], axis=0), p, consts)
```
